```python
import jax, jax.numpy as jnp
from jax import lax
import numpy as np

D_MODEL = 1024
BATCH = 4
SEQ = 8192
DEPTH = 2

NORM_EPS = 1e-6
ATT_GROUPS = ((128, 1), (512, 4), (2048, 16))
N_ATT_GROUPS = 3
ATT_SLOTS = 4
ATT_HEAD_DIM = 64
ATT_WIDTH = ATT_SLOTS * ATT_HEAD_DIM
ATT_BLOCK = 128
ALIBI_MAX_EXP = 8.0
DN_HEADS = 4
DN_HEAD_DIM = 128
DN_WIDTH = DN_HEADS * DN_HEAD_DIM
DN_CONV = 4
DN_CHUNK = 64
S5_GROUP = 16
S5_GROUPS = 16
S5_WIDTH = S5_GROUPS * S5_GROUP
S5_STATE = 64
PEER_HEADS = 8
PEER_KEYS = 128
PEER_EXPERTS = PEER_KEYS * PEER_KEYS
PEER_TOPK = 16
PEER_KEY_DIM = 128
PEER_BLOCK = 128
PLE_DIM = 256

SPLIT_SIZES = (N_ATT_GROUPS * ATT_WIDTH, ATT_WIDTH, ATT_WIDTH, 3 * DN_WIDTH, DN_WIDTH, DN_HEADS, DN_HEADS, S5_WIDTH, 3 * D_MODEL)
IN_COLS = sum(SPLIT_SIZES)
SPLIT_IDX = tuple(int(c) for c in np.cumsum(SPLIT_SIZES)[:-1])

kernel_name = 'hybrid_dilated_deltanet_s5_peer_block'


def rms_norm(x, gain):
    xf = x.astype(jnp.float32)
    y = xf * lax.rsqrt(jnp.mean(xf * xf, axis=-1, keepdims=True) + NORM_EPS)
    return (y * gain.astype(jnp.float32)).astype(x.dtype)


def l2_normalize(x):
    return x * lax.rsqrt(jnp.sum(x * x, axis=-1, keepdims=True) + NORM_EPS)


def alibi_slopes():
    n = N_ATT_GROUPS * ATT_SLOTS
    s = 2.0 ** (-ALIBI_MAX_EXP * np.arange(1, n + 1) / n)
    return jnp.asarray(s.reshape(N_ATT_GROUPS, ATT_SLOTS), dtype=jnp.float32)


def dilated_window_attention(q, k, v, slopes, dilation, span):
    b, h, seq, dh = q.shape
    sub = seq // dilation
    nb = -(-sub // ATT_BLOCK)
    pad = nb * ATT_BLOCK - sub

    def to_blocks(t):
        t = t.reshape(b, h, sub, dilation, dh).transpose(0, 1, 3, 2, 4)
        t = jnp.pad(t, ((0, 0), (0, 0), (0, 0), (0, pad), (0, 0)))
        return t.reshape(b, h, dilation, nb, ATT_BLOCK, dh)

    def with_prev(t):
        prev = jnp.pad(t, ((0, 0), (0, 0), (0, 0), (1, 0), (0, 0), (0, 0)))[:, :, :, :-1]
        return jnp.concatenate([prev, t], axis=4)

    qb = to_blocks(q)
    kc = with_prev(to_blocks(k))
    vc = with_prev(to_blocks(v))
    s = jnp.einsum('bhrnqd,bhrnkd->bhrnqk', qb, kc)
    qpos = jnp.arange(ATT_BLOCK)[:, None]
    kpos = jnp.arange(2 * ATT_BLOCK)[None, :]
    rel = qpos + ATT_BLOCK - kpos
    blk = jnp.arange(nb)[:, None, None]
    valid = (rel >= 0) & (rel <= span) & (blk * ATT_BLOCK + kpos >= ATT_BLOCK)
    bias = -slopes[:, None, None, None, None] * (rel * dilation).astype(jnp.float32)
    s = jnp.where(valid, s + bias, -jnp.inf)
    m = jnp.max(s, axis=-1, keepdims=True)
    e = jnp.exp(s - m)
    den = jnp.sum(e, axis=-1, keepdims=True)
    o = jnp.einsum('bhrnqk,bhrnkd->bhrnqd', e, vc) / den
    lse = (m + jnp.log(den))[..., 0]
    o = o.reshape(b, h, dilation, nb * ATT_BLOCK, dh)[:, :, :, :sub].transpose(0, 1, 3, 2, 4).reshape(b, h, seq, dh)
    lse = lse.reshape(b, h, dilation, nb * ATT_BLOCK)[..., :sub].transpose(0, 1, 3, 2).reshape(b, h, seq)
    return o, lse


def dilated_attention_mixer(qa, ka, va, q_gain, k_gain):
    b, seq, _ = qa.shape
    q = rms_norm(qa.reshape(b, seq, N_ATT_GROUPS, ATT_SLOTS, ATT_HEAD_DIM), q_gain).astype(jnp.float32)
    q = (q * ATT_HEAD_DIM ** -0.5).transpose(2, 0, 3, 1, 4)
    k = rms_norm(ka.reshape(b, seq, ATT_SLOTS, ATT_HEAD_DIM), k_gain).astype(jnp.float32).transpose(0, 2, 1, 3)
    v = va.reshape(b, seq, ATT_SLOTS, ATT_HEAD_DIM).astype(jnp.float32).transpose(0, 2, 1, 3)
    slopes = alibi_slopes()
    outs, lses = [], []
    for g, (window, dilation) in enumerate(ATT_GROUPS):
        o, lse = dilated_window_attention(q[g], k, v, slopes[g], dilation, window // dilation)
        outs.append(o)
        lses.append(lse)
    wts = jax.nn.softmax(jnp.stack(lses), axis=0)
    o = jnp.sum(wts[..., None] * jnp.stack(outs), axis=0)
    return o.transpose(0, 2, 1, 3).reshape(b, seq, ATT_WIDTH).astype(qa.dtype)


def causal_depthwise_conv(x, w):
    ch = x.shape[-1]
    return lax.conv_general_dilated(x, w[:, None, :].astype(x.dtype), window_strides=(1,), padding=((DN_CONV - 1, 0),), dimension_numbers=('NWC', 'WIO', 'NWC'), feature_group_count=ch)


def chunked_gated_delta_rule(q, k, v, beta, g):
    b, h, seq, dk = q.shape
    dv = v.shape[-1]
    c = DN_CHUNK
    n = seq // c
    q = (q * dk ** -0.5).reshape(b, h, n, c, dk)
    k = k.reshape(b, h, n, c, dk)
    v = v.reshape(b, h, n, c, dv)
    beta = beta.reshape(b, h, n, c)
    g = jnp.cumsum(g.reshape(b, h, n, c), axis=-1)
    incl = jnp.tril(jnp.ones((c, c), dtype=bool))
    strict = jnp.tril(jnp.ones((c, c), dtype=bool), k=-1)
    decay = jnp.exp(jnp.where(incl, g[..., :, None] - g[..., None, :], -jnp.inf))
    k_beta = k * beta[..., None]
    lower = jnp.where(strict, jnp.einsum('bhnid,bhnjd->bhnij', k_beta, k) * decay, 0.0)
    eye = jnp.eye(c, dtype=q.dtype)
    t_inv = lax.linalg.triangular_solve(lower + eye, jnp.broadcast_to(eye, lower.shape), left_side=True, lower=True, unit_diagonal=True)
    u = jnp.einsum('bhnij,bhnje->bhnie', t_inv, v * beta[..., None])
    w = jnp.einsum('bhnij,bhnjd->bhnid', t_inv, k_beta * jnp.exp(g)[..., None])
    attn = jnp.where(incl, jnp.einsum('bhnid,bhnjd->bhnij', q, k) * decay, 0.0)
    q_dec = q * jnp.exp(g)[..., None]
    k_dec = k * jnp.exp(g[..., -1:] - g)[..., None]
    g_last = jnp.exp(g[..., -1])

    def step(state, inp):
        u_c, w_c, q_c, k_c, a_c, gl = inp
        v_new = u_c - jnp.einsum('bhcd,bhde->bhce', w_c, state)
        o_c = jnp.einsum('bhcd,bhde->bhce', q_c, state) + jnp.einsum('bhij,bhje->bhie', a_c, v_new)
        state = state * gl[..., None, None] + jnp.einsum('bhcd,bhce->bhde', k_c, v_new)
        return state, o_c

    xs = tuple(jnp.moveaxis(t, 2, 0) for t in (u, w, q_dec, k_dec, attn, g_last))
    state0 = jnp.zeros((b, h, dk, dv), q.dtype)
    _, o = lax.scan(step, state0, xs)
    return jnp.moveaxis(o, 0, 2).reshape(b, h, seq, dv)


def gated_deltanet_mixer(qkv, z, beta_logit, a_in, conv_w, a_log, dt_bias, out_gain):
    b, seq, _ = qkv.shape
    qkv = jax.nn.silu(causal_depthwise_conv(qkv, conv_w)).astype(jnp.float32)

    def heads(t):
        return t.reshape(b, seq, DN_HEADS, DN_HEAD_DIM).transpose(0, 2, 1, 3)

    q, k, v = (heads(t) for t in jnp.split(qkv, 3, axis=-1))
    q, k = l2_normalize(q), l2_normalize(k)
    beta = jax.nn.sigmoid(beta_logit.astype(jnp.float32)).transpose(0, 2, 1)
    g = -(jnp.exp(a_log.astype(jnp.float32)) * jax.nn.softplus(a_in.astype(jnp.float32) + dt_bias.astype(jnp.float32))).transpose(0, 2, 1)
    o = chunked_gated_delta_rule(q, k, v, beta, g).transpose(0, 2, 1, 3)
    zf = z.astype(jnp.float32).reshape(b, seq, DN_HEADS, DN_HEAD_DIM)
    o = rms_norm(o, out_gain) * jax.nn.silu(zf)
    return o.reshape(b, seq, DN_WIDTH).astype(z.dtype)


def s5_mixer(u, a_re, a_im, log_dt, b_re, b_im, c_re, c_im, d_skip):
    bsz, seq, _ = u.shape
    uf = u.astype(jnp.float32).reshape(bsz, seq, S5_GROUPS, S5_GROUP)
    dt = jnp.exp(log_dt.astype(jnp.float32))[:, None]
    lr, li = a_re.astype(jnp.float32), a_im.astype(jnp.float32)
    mag = jnp.exp(lr * dt)
    ab_r, ab_i = mag * jnp.cos(li * dt), mag * jnp.sin(li * dt)
    den = lr * lr + li * li
    nr, ni = ab_r - 1.0, ab_i
    cr, ci = (nr * lr + ni * li) / den, (ni * lr - nr * li) / den
    br, bi = b_re.astype(jnp.float32), b_im.astype(jnp.float32)
    bb_r = cr[..., None] * br - ci[..., None] * bi
    bb_i = cr[..., None] * bi + ci[..., None] * br
    bu_r = jnp.einsum('blgp,gnp->blgn', uf, bb_r)
    bu_i = jnp.einsum('blgp,gnp->blgn', uf, bb_i)
    a_r = jnp.broadcast_to(ab_r, (seq, S5_GROUPS, S5_STATE))
    a_i = jnp.broadcast_to(ab_i, (seq, S5_GROUPS, S5_STATE))

    def combine(e1, e2):
        a1r, a1i, b1r, b1i = e1
        a2r, a2i, b2r, b2i = e2
        return (a2r * a1r - a2i * a1i, a2r * a1i + a2i * a1r, a2r * b1r - a2i * b1i + b2r, a2r * b1i + a2i * b1r + b2i)

    def scan_one(sr, si):
        _, _, xr, xi = lax.associative_scan(combine, (a_r, a_i, sr, si), axis=0)
        return xr, xi

    xr, xi = jax.vmap(scan_one)(bu_r, bu_i)
    y = (jnp.einsum('blgn,gpn->blgp', xr, c_re.astype(jnp.float32)) - jnp.einsum('blgn,gpn->blgp', xi, c_im.astype(jnp.float32)) + d_skip.astype(jnp.float32) * uf)
    return y.reshape(bsz, seq, S5_WIDTH).astype(u.dtype)


def peer_ffn(xf, w_query, keys1, keys2, expert_u, expert_v):
    tokens, d = xf.shape

    def block(xt):
        qry = (xt @ w_query).reshape(PEER_BLOCK, PEER_HEADS, 2, PEER_KEY_DIM)
        s1 = jnp.einsum('thd,kd->thk', qry[:, :, 0], keys1)
        s2 = jnp.einsum('thd,kd->thk', qry[:, :, 1], keys2)
        v1, i1 = lax.top_k(s1, PEER_TOPK)
        v2, i2 = lax.top_k(s2, PEER_TOPK)
        cand_s = (v1[..., :, None] + v2[..., None, :]).reshape(PEER_BLOCK, PEER_HEADS, PEER_TOPK * PEER_TOPK)
        cand_i = (i1[..., :, None] * PEER_KEYS + i2[..., None, :]).reshape(PEER_BLOCK, PEER_HEADS, PEER_TOPK * PEER_TOPK)
        top_s, pos = lax.top_k(cand_s, PEER_TOPK)
        idx = jnp.take_along_axis(cand_i, pos, axis=-1)
        gate = jax.nn.softmax(top_s.astype(jnp.float32), axis=-1).astype(xt.dtype)
        act = jax.nn.gelu(jnp.einsum('td,thkd->thk', xt, expert_u[idx]))
        return jnp.einsum('thk,thkd->td', gate * act, expert_v[idx])

    out = lax.map(block, xf.reshape(tokens // PEER_BLOCK, PEER_BLOCK, d))
    return out.reshape(tokens, d)


def hybrid_layer(x, pe, norm_mix, w_in, att_q_gain, att_k_gain, w_a_out, dn_conv, dn_a_log, dn_dt_bias, dn_out_gain, w_b_out, s5_a_re, s5_a_im, s5_log_dt, s5_b_re, s5_b_im, s5_c_re, s5_c_im, s5_d, w_c_val, w_c_gate, w_out, norm_ffn, peer_w_query, peer_keys1, peer_keys2, peer_u, peer_v, ple_w, ple_norm, ple_w_gate):
    b, seq, d = x.shape
    h = rms_norm(x, norm_mix)
    qa, ka, va, qkv_b, z_b, beta_b, alpha_b, u_c, gates = jnp.split(h @ w_in, SPLIT_IDX, axis=-1)
    y_a = dilated_attention_mixer(qa, ka, va, att_q_gain, att_k_gain) @ w_a_out
    y_b = gated_deltanet_mixer(qkv_b, z_b, beta_b, alpha_b, dn_conv, dn_a_log, dn_dt_bias, dn_out_gain) @ w_b_out
    s = jax.nn.gelu(s5_mixer(u_c, s5_a_re, s5_a_im, s5_log_dt, s5_b_re, s5_b_im, s5_c_re, s5_c_im, s5_d))
    y_c = (s @ w_c_val) * jax.nn.sigmoid(s @ w_c_gate)
    g_a, g_b, g_c = jnp.split(jax.nn.sigmoid(gates), 3, axis=-1)
    x = x + (g_a * y_a + g_b * y_b + g_c * y_c) @ w_out
    hf = rms_norm(x, norm_ffn).reshape(b * seq, d)
    x = x + peer_ffn(hf, peer_w_query, peer_keys1, peer_keys2, peer_u, peer_v).reshape(b, seq, d)
    x = x + jax.nn.sigmoid(rms_norm(x, ple_norm) @ ple_w_gate) * (pe @ ple_w)
    return x


def setup_inputs(seed: int = 0) -> dict:
    key = jax.random.key(seed)
    keys = iter(jax.random.split(key, 40))
    f32 = jnp.float32
    nl = DEPTH

    def normal(shape, scale):
        return jax.random.normal(next(keys), shape, f32) * scale

    def uniform(shape, lo, hi):
        return jax.random.uniform(next(keys), shape, f32, lo, hi)

    def gain(shape):
        return 1.0 + normal(shape, 0.02)

    log_dt_lo, log_dt_hi = float(np.log(1e-3)), float(np.log(1e-1))
    dn_dt = jnp.exp(uniform((nl, DN_HEADS), log_dt_lo, log_dt_hi))
    return {
        'x': normal((BATCH, SEQ, D_MODEL), 1.0),
        'p': normal((DEPTH, BATCH, SEQ, PLE_DIM), 1.0),
        'norm_mix': gain((nl, D_MODEL)),
        'w_in': normal((nl, D_MODEL, IN_COLS), D_MODEL ** -0.5),
        'att_q_gain': gain((nl, ATT_HEAD_DIM)),
        'att_k_gain': gain((nl, ATT_HEAD_DIM)),
        'w_a_out': normal((nl, ATT_WIDTH, D_MODEL), ATT_WIDTH ** -0.5),
        'dn_conv': normal((nl, DN_CONV, 3 * DN_WIDTH), DN_CONV ** -0.5),
        'dn_a_log': jnp.log(uniform((nl, DN_HEADS), 1.0, 16.0)),
        'dn_dt_bias': dn_dt + jnp.log(-jnp.expm1(-dn_dt)),
        'dn_out_gain': gain((nl, DN_HEAD_DIM)),
        'w_b_out': normal((nl, DN_WIDTH, D_MODEL), DN_WIDTH ** -0.5),
        's5_a_re': -0.5 + normal((nl, S5_GROUPS, S5_STATE), 0.01),
        's5_a_im': jnp.pi * jnp.arange(S5_STATE, dtype=f32) + normal((nl, S5_GROUPS, S5_STATE), 0.01),
        's5_log_dt': uniform((nl, S5_GROUPS), log_dt_lo, log_dt_hi),
        's5_b_re': normal((nl, S5_GROUPS, S5_STATE, S5_GROUP), (2 * S5_GROUP) ** -0.5),
        's5_b_im': normal((nl, S5_GROUPS, S5_STATE, S5_GROUP), (2 * S5_GROUP) ** -0.5),
        's5_c_re': normal((nl, S5_GROUPS, S5_GROUP, S5_STATE), S5_STATE ** -0.5),
        's5_c_im': normal((nl, S5_GROUPS, S5_GROUP, S5_STATE), S5_STATE ** -0.5),
        's5_d': normal((nl, S5_GROUPS, S5_GROUP), 1.0),
        'w_c_val': normal((nl, S5_WIDTH, D_MODEL), S5_WIDTH ** -0.5),
        'w_c_gate': normal((nl, S5_WIDTH, D_MODEL), S5_WIDTH ** -0.5),
        'w_out': normal((nl, D_MODEL, D_MODEL), D_MODEL ** -0.5),
        'norm_ffn': gain((nl, D_MODEL)),
        'peer_w_query': normal((nl, D_MODEL, PEER_HEADS * 2 * PEER_KEY_DIM), D_MODEL ** -0.5),
        'peer_keys1': normal((nl, PEER_KEYS, PEER_KEY_DIM), PEER_KEY_DIM ** -0.5),
        'peer_keys2': normal((nl, PEER_KEYS, PEER_KEY_DIM), PEER_KEY_DIM ** -0.5),
        'peer_u': normal((nl, PEER_EXPERTS, D_MODEL), D_MODEL ** -0.5),
        'peer_v': normal((nl, PEER_EXPERTS, D_MODEL), PEER_HEADS ** -0.5),
        'ple_w': normal((nl, PLE_DIM, D_MODEL), PLE_DIM ** -0.5),
        'ple_norm': gain((nl, D_MODEL)),
        'ple_w_gate': normal((nl, D_MODEL, D_MODEL), D_MODEL ** -0.5),
    }


def reference(x, p, norm_mix, w_in, att_q_gain, att_k_gain, w_a_out, dn_conv, dn_a_log, dn_dt_bias, dn_out_gain, w_b_out, s5_a_re, s5_a_im, s5_log_dt, s5_b_re, s5_b_im, s5_c_re, s5_c_im, s5_d, w_c_val, w_c_gate, w_out, norm_ffn, peer_w_query, peer_keys1, peer_keys2, peer_u, peer_v, ple_w, ple_norm, ple_w_gate):
    for i in range(DEPTH):
        x = hybrid_layer(x, p[i], norm_mix[i], w_in[i], att_q_gain[i], att_k_gain[i], w_a_out[i], dn_conv[i], dn_a_log[i], dn_dt_bias[i], dn_out_gain[i], w_b_out[i], s5_a_re[i], s5_a_im[i], s5_log_dt[i], s5_b_re[i], s5_b_im[i], s5_c_re[i], s5_c_im[i], s5_d[i], w_c_val[i], w_c_gate[i], w_out[i], norm_ffn[i], peer_w_query[i], peer_keys1[i], peer_keys2[i], peer_u[i], peer_v[i], ple_w[i], ple_norm[i], ple_w_gate[i])
    return x
```

```python
import functools

import numpy as np
import jax
import jax.numpy as jnp
from jax import lax
from jax.experimental import pallas as pl
from jax.experimental.pallas import tpu as pltpu

F32 = jnp.float32
BF16 = jnp.bfloat16
HI = lax.Precision.HIGHEST

D_MODEL = 1024
NORM_EPS = 1e-6
ATT_GROUPS = ((128, 1), (512, 4), (2048, 16))
ATT_SLOTS = 4
ATT_HEAD_DIM = 64
ATT_WIDTH = 256
ATT_BLOCK = 128
ALIBI_MAX_EXP = 8.0
DN_HEADS = 4
DN_HEAD_DIM = 128
DN_WIDTH = 512
DN_CONV = 4
DN_CHUNK = 64
S5_GROUP = 16
S5_GROUPS = 16
S5_WIDTH = 256
S5_STATE = 64
PEER_HEADS = 8
PEER_KEYS = 128
PEER_TOPK = 16
PEER_KEY_DIM = 128
PEER_BLOCK = 128
PLE_DIM = 256

OFF_QKV, OFF_Z, OFF_GATES, OFF_QA, OFF_KA, OFF_VA, OFF_U, OFF_BA = 0, 1536, 2048, 5120, 5888, 6144, 6400, 6656
PROJ_COLS = 7168
SRC_SIZES = (768, 256, 256, 1536, 512, 4, 4, 256, 3072)
SRC_DST = (OFF_QA, OFF_KA, OFF_VA, OFF_QKV, OFF_Z, OFF_BA, OFF_BA + 4, OFF_U, OFF_GATES)

VMEM_LIMIT = 56 * 1024 * 1024
NEG_BIG = -1e30


def _params(*sem):
    return pltpu.CompilerParams(dimension_semantics=sem, vmem_limit_bytes=VMEM_LIMIT)


def _const_spec(shape):
    nd = len(shape)
    return pl.BlockSpec(shape, lambda *_: (0,) * nd)


def _rms(x, gain):
    return x * lax.rsqrt(jnp.mean(x * x, axis=-1, keepdims=True) + NORM_EPS) * gain


def _sigmoid(x):
    return 1.0 / (1.0 + jnp.exp(-x))


def _gelu_tanh(x):
    return 0.5 * x * (1.0 + jnp.tanh(0.7978845608028654 * (x + 0.044715 * (x * x * x))))


def _dot(a, b, **kw):
    return jnp.dot(a, b, preferred_element_type=F32, **kw)


def _dot_nt(a, b):
    return lax.dot_general(a, b, (((1,), (1,)), ((), ())), preferred_element_type=F32)


def _dot_tn(a, b):
    return lax.dot_general(a, b, (((0,), (0,)), ((), ())), preferred_element_type=F32)


def _norm_matmul_kernel(x_ref, g_ref, w_ref, o_ref, h_ref):
    @pl.when(pl.program_id(1) == 0)
    def _():
        h_ref[...] = _rms(x_ref[...], g_ref[...]).astype(BF16)

    o_ref[...] = _dot(h_ref[...], w_ref[...])


def _norm_matmul(x2d, gain, w_bf16, tm=1024, tn=1024):
    t, d = x2d.shape
    n = w_bf16.shape[1]
    tm = min(tm, t)
    tn = min(tn, n)
    return pl.pallas_call(
        _norm_matmul_kernel,
        grid=(t // tm, n // tn),
        in_specs=[
            pl.BlockSpec((tm, d), lambda i, j: (i, 0)),
            pl.BlockSpec((1, d), lambda i, j: (0, 0)),
            pl.BlockSpec((d, tn), lambda i, j: (0, j)),
        ],
        out_specs=pl.BlockSpec((tm, tn), lambda i, j: (i, j)),
        out_shape=jax.ShapeDtypeStruct((t, n), F32),
        scratch_shapes=[pltpu.VMEM((tm, d), BF16)],
        compiler_params=_params("parallel", "arbitrary"),
        name="norm_matmul",
    )(x2d, gain.reshape(1, d), w_bf16)


def _attn_kernel(q_ref, kp_ref, kc_ref, vp_ref, vc_ref, qg_ref, kg_ref, bd_ref, o_ref, lse_ref, *,
                 dilation, span, slopes):
    n = pl.program_id(2)
    bd = bd_ref[...]

    def head_norm(x, gain):
        ss = _dot(x * x, bd, precision=HI)
        return x * lax.rsqrt(ss * (1.0 / ATT_HEAD_DIM) + NORM_EPS) * gain

    q = head_norm(q_ref[0], qg_ref[...]) * (ATT_HEAD_DIM ** -0.5)
    k = head_norm(jnp.concatenate([kp_ref[0], kc_ref[0]], axis=0), kg_ref[...])
    v = jnp.concatenate([vp_ref[0], vc_ref[0]], axis=0)
    qb, kb, vb = q.astype(BF16), k.astype(BF16), v.astype(BF16)

    qpos = lax.broadcasted_iota(jnp.int32, (ATT_BLOCK, 2 * ATT_BLOCK), 0)
    kpos = lax.broadcasted_iota(jnp.int32, (ATT_BLOCK, 2 * ATT_BLOCK), 1)
    rel = qpos + ATT_BLOCK - kpos
    valid = (rel >= 0) & (rel <= span) & ((kpos >= ATT_BLOCK) | (n > 0))
    dist = (rel * dilation).astype(F32)
    lane = lax.broadcasted_iota(jnp.int32, (ATT_BLOCK, 128), 1)

    lse_all = jnp.zeros((ATT_BLOCK, 128), F32)
    outs = []
    for h in range(ATT_SLOTS):
        sl = slice(h * ATT_HEAD_DIM, (h + 1) * ATT_HEAD_DIM)
        s = _dot_nt(qb[:, sl], kb[:, sl])
        s = jnp.where(valid, s - slopes[h] * dist, NEG_BIG)
        m = jnp.max(s, axis=-1, keepdims=True)
        e = jnp.exp(s - m)
        den = jnp.sum(e, axis=-1, keepdims=True)
        outs.append(_dot(e.astype(BF16), vb[:, sl]) / den)
        lse_all = jnp.where(lane == h, m + jnp.log(den), lse_all)
    o_ref[0] = jnp.concatenate(outs, axis=-1)
    lse_ref[0] = lse_all


def _attention_group(proj3, q_gain, k_gain, group):
    b, seq, _ = proj3.shape
    window, dilation = ATT_GROUPS[group]
    span = window // dilation
    sub = seq // dilation
    nb = sub // ATT_BLOCK
    view = proj3.reshape(b, sub, dilation * PROJ_COLS)
    cb = PROJ_COLS // ATT_WIDTH
    qc, kc, vc = OFF_QA // ATT_WIDTH + group, OFF_KA // ATT_WIDTH, OFF_VA // ATT_WIDTH
    n_heads = len(ATT_GROUPS) * ATT_SLOTS
    slopes = 2.0 ** (-ALIBI_MAX_EXP * np.arange(1, n_heads + 1) / n_heads)
    slopes = tuple(float(s) for s in slopes.reshape(len(ATT_GROUPS), ATT_SLOTS)[group])
    blk = (1, ATT_BLOCK, ATT_WIDTH)
    bd = (np.arange(ATT_WIDTH)[:, None] // ATT_HEAD_DIM == np.arange(ATT_WIDTH)[None, :] // ATT_HEAD_DIM)
    o, lse = pl.pallas_call(
        functools.partial(_attn_kernel, dilation=dilation, span=span, slopes=slopes),
        grid=(b, dilation, nb),
        in_specs=[
            pl.BlockSpec(blk, lambda i, r, n: (i, n, r * cb + qc)),
            pl.BlockSpec(blk, lambda i, r, n: (i, jnp.maximum(n - 1, 0), r * cb + kc)),
            pl.BlockSpec(blk, lambda i, r, n: (i, n, r * cb + kc)),
            pl.BlockSpec(blk, lambda i, r, n: (i, jnp.maximum(n - 1, 0), r * cb + vc)),
            pl.BlockSpec(blk, lambda i, r, n: (i, n, r * cb + vc)),
            _const_spec((1, ATT_WIDTH)),
            _const_spec((1, ATT_WIDTH)),
            _const_spec((ATT_WIDTH, ATT_WIDTH)),
        ],
        out_specs=[
            pl.BlockSpec(blk, lambda i, r, n: (i, n, r)),
            pl.BlockSpec((1, ATT_BLOCK, 128), lambda i, r, n: (i, n, r)),
        ],
        out_shape=[
            jax.ShapeDtypeStruct((b, sub, dilation * ATT_WIDTH), F32),
            jax.ShapeDtypeStruct((b, sub, dilation * 128), F32),
        ],
        compiler_params=_params("parallel", "parallel", "arbitrary"),
        name=f"dilated_attention_g{group}",
    )(view, view, view, view, view,
      jnp.tile(q_gain, ATT_SLOTS).reshape(1, ATT_WIDTH), jnp.tile(k_gain, ATT_SLOTS).reshape(1, ATT_WIDTH),
      jnp.asarray(bd, F32))
    return o.reshape(b * seq, ATT_WIDTH), lse.reshape(b * seq, 128)


DN_ROWS = 256


def _deltanet_kernel(qkv_ref, z_ref, ba_ref, cw_ref, alog_ref, dtb_ref, og_ref, tril_ref, o_ref,
                     state_ref, tail_ref):
    c = DN_CHUNK
    rows = qkv_ref.shape[1]

    @pl.when(pl.program_id(1) == 0)
    def _():
        state_ref[...] = jnp.zeros_like(state_ref)
        tail_ref[...] = jnp.zeros_like(tail_ref)

    x = qkv_ref[0]
    xc = jnp.concatenate([tail_ref[...], x], axis=0)
    cw = cw_ref[...]
    acc = cw[DN_CONV - 1:DN_CONV] * x
    for j in range(DN_CONV - 1):
        acc = acc + cw[j:j + 1] * xc[5 + j:5 + j + rows]
    tail_ref[...] = x[rows - 8:rows]
    a = acc * _sigmoid(acc)

    ba = ba_ref[0]
    beta_all = _sigmoid(ba)
    sp = ba + dtb_ref[...]
    sp = jnp.maximum(sp, 0.0) + jnp.log(1.0 + jnp.exp(-jnp.abs(sp)))
    g_all = -(jnp.exp(alog_ref[...]) * sp)
    tril = tril_ref[...]
    ii = lax.broadcasted_iota(jnp.int32, (c, c), 0)
    jj = lax.broadcasted_iota(jnp.int32, (c, c), 1)
    incl = ii >= jj
    strict = ii > jj
    eye = (ii == jj).astype(F32)
    z = z_ref[0]
    og = og_ref[...]

    outs = []
    for h in range(DN_HEADS):
        hs = slice(h * DN_HEAD_DIM, (h + 1) * DN_HEAD_DIM)
        qh = a[:, hs]
        kh = a[:, DN_WIDTH + h * DN_HEAD_DIM:DN_WIDTH + (h + 1) * DN_HEAD_DIM]
        vh = a[:, 2 * DN_WIDTH + h * DN_HEAD_DIM:2 * DN_WIDTH + (h + 1) * DN_HEAD_DIM]
        qh = qh * lax.rsqrt(jnp.sum(qh * qh, axis=-1, keepdims=True) + NORM_EPS) * (DN_HEAD_DIM ** -0.5)
        kh = kh * lax.rsqrt(jnp.sum(kh * kh, axis=-1, keepdims=True) + NORM_EPS)
        beta = beta_all[:, h:h + 1]
        gh = jnp.broadcast_to(g_all[:, DN_HEADS + h:DN_HEADS + h + 1], (rows, DN_HEAD_DIM))
        state = state_ref[h]
        o_chunks = []
        for ci in range(rows // c):
            rs = slice(ci * c, (ci + 1) * c)
            gc = _dot(tril, gh[rs], precision=HI)
            gc64 = gc[:, :c]
            diff = gc64 - gc64.T
            decay = jnp.where(incl, jnp.exp(jnp.where(incl, diff, 0.0)), 0.0)
            eg = jnp.exp(gc)
            g_last = gc[c - 1:c, :]
            kc_, qc_, vc_, bc_ = kh[rs], qh[rs], vh[rs], beta[rs]
            kbeta = kc_ * bc_
            kcb = kc_.astype(BF16)
            lower = jnp.where(strict, _dot_nt(kbeta.astype(BF16), kcb) * decay, 0.0)
            t_inv = eye - lower
            pw = _dot(lower, lower, precision=HI)
            for it in range(5):
                t_inv = t_inv + _dot(t_inv, pw, precision=HI)
                if it < 4:
                    pw = _dot(pw, pw, precision=HI)
            t_b = t_inv.astype(BF16)
            u = _dot(t_b, (vc_ * bc_).astype(BF16))
            w = _dot(t_b, (kbeta * eg).astype(BF16))
            attn = jnp.where(incl, _dot_nt(qc_.astype(BF16), kcb) * decay, 0.0)
            q_dec = qc_ * eg
            k_dec = kc_ * jnp.exp(g_last - gc)
            sb = state.astype(BF16)
            v_new = u - _dot(w.astype(BF16), sb)
            vnb = v_new.astype(BF16)
            o_chunks.append(_dot(q_dec.astype(BF16), sb) + _dot(attn.astype(BF16), vnb))
            state = state * jnp.exp(g_last) + _dot_tn(k_dec.astype(BF16), vnb)
        state_ref[h] = state
        oh = jnp.concatenate(o_chunks, axis=0)
        zh = z[:, hs]
        outs.append(_rms(oh, og) * (zh * _sigmoid(zh)))
    o_ref[0] = jnp.concatenate(outs, axis=-1)


def _deltanet(proj3, conv_w, a_log, dt_bias, out_gain):
    b, seq, _ = proj3.shape
    rows = min(DN_ROWS, seq)
    pad_a = jnp.zeros((1, 128), F32).at[0, DN_HEADS:2 * DN_HEADS].set(a_log)
    pad_d = jnp.zeros((1, 128), F32).at[0, DN_HEADS:2 * DN_HEADS].set(dt_bias)
    tril = jnp.asarray(np.tril(np.ones((DN_CHUNK, DN_CHUNK), np.float32)))
    out = pl.pallas_call(
        _deltanet_kernel,
        grid=(b, seq // rows),
        in_specs=[
            pl.BlockSpec((1, rows, 3 * DN_WIDTH), lambda i, t: (i, t, OFF_QKV // (3 * DN_WIDTH))),
            pl.BlockSpec((1, rows, DN_WIDTH), lambda i, t: (i, t, OFF_Z // DN_WIDTH)),
            pl.BlockSpec((1, rows, 128), lambda i, t: (i, t, OFF_BA // 128)),
            _const_spec((DN_CONV, 3 * DN_WIDTH)),
            _const_spec((1, 128)),
            _const_spec((1, 128)),
            _const_spec((1, DN_HEAD_DIM)),
            _const_spec((DN_CHUNK, DN_CHUNK)),
        ],
        out_specs=pl.BlockSpec((1, rows, DN_WIDTH), lambda i, t: (i, t, 0)),
        out_shape=jax.ShapeDtypeStruct((b, seq, DN_WIDTH), F32),
        scratch_shapes=[pltpu.VMEM((DN_HEADS, DN_HEAD_DIM, DN_HEAD_DIM), F32), pltpu.VMEM((8, 3 * DN_WIDTH), F32)],
        compiler_params=_params("parallel", "arbitrary"),
        name="gated_deltanet",
    )(proj3, proj3, proj3, conv_w, pad_a, pad_d, out_gain.reshape(1, DN_HEAD_DIM), tril)
    return out.reshape(b * seq, DN_WIDTH)


S5_ROWS = 128
S5_N = S5_GROUPS * S5_STATE


def _s5_kernel(u_ref, bbd_ref, pinv_r_ref, pinv_i_ref, pow_r_ref, pow_i_ref, pow1_r_ref, pow1_i_ref,
               cr_ref, ci_ref, d_ref, tril_ref, o_ref, sr_ref, si_ref):
    @pl.when(pl.program_id(1) == 0)
    def _():
        sr_ref[...] = jnp.zeros_like(sr_ref)
        si_ref[...] = jnp.zeros_like(si_ref)

    u = u_ref[0]
    bu = _dot(u.astype(BF16), bbd_ref[...])
    br, bi = bu[:, :S5_N], bu[:, S5_N:]
    pr, pi = pinv_r_ref[...], pinv_i_ref[...]
    zr = pr * br - pi * bi
    zi = pr * bi + pi * br
    tril = tril_ref[...]
    cr = _dot(tril, zr, precision=HI)
    ci = _dot(tril, zi, precision=HI)
    wr, wi = pow_r_ref[...], pow_i_ref[...]
    p1r, p1i = pow1_r_ref[...], pow1_i_ref[...]
    sr, si = sr_ref[...], si_ref[...]
    xr = wr * cr - wi * ci + (p1r * sr - p1i * si)
    xi = wr * ci + wi * cr + (p1r * si + p1i * sr)
    rows = xr.shape[0]
    sr_ref[...] = xr[rows - 1:rows]
    si_ref[...] = xi[rows - 1:rows]
    y = _dot(xr.astype(BF16), cr_ref[...]) - _dot(xi.astype(BF16), ci_ref[...]) + d_ref[...] * u
    o_ref[0] = _gelu_tanh(y)


def _s5(proj3, a_re, a_im, log_dt, b_re, b_im, c_re, c_im, d_skip):
    b, seq, _ = proj3.shape
    rows = min(S5_ROWS, seq)
    g, n, p = S5_GROUPS, S5_STATE, S5_GROUP
    dt = jnp.exp(log_dt)[:, None]
    lr, li = a_re, a_im
    mag = jnp.exp(lr * dt)
    ab_r, ab_i = mag * jnp.cos(li * dt), mag * jnp.sin(li * dt)
    den = lr * lr + li * li
    nr, ni = ab_r - 1.0, ab_i
    cr, ci = (nr * lr + ni * li) / den, (ni * lr - nr * li) / den
    bb_r = cr[..., None] * b_re - ci[..., None] * b_im
    bb_i = cr[..., None] * b_im + ci[..., None] * b_re
    eye_g = jnp.eye(g, dtype=F32)
    bbd_r = jnp.einsum('gnp,gh->gphn', bb_r, eye_g).reshape(g * p, g * n)
    bbd_i = jnp.einsum('gnp,gh->gphn', bb_i, eye_g).reshape(g * p, g * n)
    bbd = jnp.concatenate([bbd_r, bbd_i], axis=1).astype(BF16)
    cbd_r = jnp.einsum('gpn,gh->gnhp', c_re, eye_g).reshape(g * n, g * p).astype(BF16)
    cbd_i = jnp.einsum('gpn,gh->gnhp', c_im, eye_g).reshape(g * n, g * p).astype(BF16)
    steps = jnp.arange(rows, dtype=F32)[:, None]
    lrd = (lr * dt).reshape(1, g * n)
    lid = (li * dt).reshape(1, g * n)

    def power(k):
        m = jnp.exp(lrd * k)
        return m * jnp.cos(lid * k), m * jnp.sin(lid * k)

    pinv_r, pinv_i = power(-steps)
    pow_r, pow_i = power(steps)
    pow1_r, pow1_i = power(steps + 1.0)
    tril = jnp.asarray(np.tril(np.ones((rows, rows), np.float32)))
    tab = _const_spec((rows, g * n))
    out = pl.pallas_call(
        _s5_kernel,
        grid=(b, seq // rows),
        in_specs=[
            pl.BlockSpec((1, rows, S5_WIDTH), lambda i, t: (i, t, OFF_U // S5_WIDTH)),
            _const_spec((S5_WIDTH, 2 * g * n)),
            tab, tab, tab, tab, tab, tab,
            _const_spec((g * n, S5_WIDTH)),
            _const_spec((g * n, S5_WIDTH)),
            _const_spec((1, S5_WIDTH)),
            _const_spec((rows, rows)),
        ],
        out_specs=pl.BlockSpec((1, rows, S5_WIDTH), lambda i, t: (i, t, 0)),
        out_shape=jax.ShapeDtypeStruct((b, seq, S5_WIDTH), F32),
        scratch_shapes=[pltpu.VMEM((1, g * n), F32), pltpu.VMEM((1, g * n), F32)],
        compiler_params=_params("parallel", "arbitrary"),
        name="s5_mixer",
    )(proj3, bbd, pinv_r, pinv_i, pow_r, pow_i, pow1_r, pow1_i, cbd_r, cbd_i, d_skip.reshape(1, S5_WIDTH), tril)
    return out.reshape(b * seq, S5_WIDTH)


def _combine_kernel(x_ref, ga_ref, gb_ref, gc_ref, o1_ref, o2_ref, o3_ref, l1_ref, l2_ref, l3_ref,
                    ob_ref, s_ref, wa_ref, wb_ref, wcv_ref, wcg_ref, wo_ref, ex_ref, nf_ref,
                    x1_ref, hf_ref):
    l1, l2, l3 = l1_ref[...], l2_ref[...], l3_ref[...]
    m = jnp.maximum(jnp.maximum(l1, l2), l3)
    e1, e2, e3 = jnp.exp(l1 - m), jnp.exp(l2 - m), jnp.exp(l3 - m)
    inv = 1.0 / (e1 + e2 + e3)
    ex = ex_ref[...]
    mix = (_dot(e1 * inv, ex, precision=HI) * o1_ref[...]
           + _dot(e2 * inv, ex, precision=HI) * o2_ref[...]
           + _dot(e3 * inv, ex, precision=HI) * o3_ref[...])
    y_a = _dot(mix.astype(BF16), wa_ref[...])
    y_b = _dot(ob_ref[...].astype(BF16), wb_ref[...])
    sb = s_ref[...].astype(BF16)
    y_c = _dot(sb, wcv_ref[...]) * _sigmoid(_dot(sb, wcg_ref[...]))
    mixed = _sigmoid(ga_ref[...]) * y_a + _sigmoid(gb_ref[...]) * y_b + _sigmoid(gc_ref[...]) * y_c
    x1 = x_ref[...] + _dot(mixed.astype(BF16), wo_ref[...])
    x1_ref[...] = x1
    hf_ref[...] = _rms(x1, nf_ref[...])


def _combine(x2d, proj2, att, ob, s, w_a_out, w_b_out, w_c_val, w_c_gate, w_out, norm_ffn, tm=512):
    t, d = x2d.shape
    tm = min(tm, t)
    (o1, l1), (o2, l2), (o3, l3) = att
    ex = np.zeros((128, ATT_WIDTH), np.float32)
    for h in range(ATT_SLOTS):
        ex[h, h * ATT_HEAD_DIM:(h + 1) * ATT_HEAD_DIM] = 1.0
    row = lambda w: pl.BlockSpec((tm, w), lambda i: (i, 0))
    gate = lambda k: pl.BlockSpec((tm, d), lambda i: (i, OFF_GATES // d + k))
    bf = lambda w: w.astype(BF16)
    return pl.pallas_call(
        _combine_kernel,
        grid=(t // tm,),
        in_specs=[
            row(d), gate(0), gate(1), gate(2),
            row(ATT_WIDTH), row(ATT_WIDTH), row(ATT_WIDTH), row(128), row(128), row(128),
            row(DN_WIDTH), row(S5_WIDTH),
            _const_spec((ATT_WIDTH, d)), _const_spec((DN_WIDTH, d)), _const_spec((S5_WIDTH, d)),
            _const_spec((S5_WIDTH, d)), _const_spec((d, d)), _const_spec((128, ATT_WIDTH)), _const_spec((1, d)),
        ],
        out_specs=[row(d), row(d)],
        out_shape=[jax.ShapeDtypeStruct((t, d), F32), jax.ShapeDtypeStruct((t, d), F32)],
        compiler_params=_params("parallel"),
        name="branch_combine",
    )(x2d, proj2, proj2, proj2, o1, o2, o3, l1, l2, l3, ob, s,
      bf(w_a_out), bf(w_b_out), bf(w_c_val), bf(w_c_gate), bf(w_out), jnp.asarray(ex), norm_ffn.reshape(1, d))


def _ple_kernel(x_ref, f_ref, pe_ref, pn_ref, wg_ref, pw_ref, o_ref):
    x2 = x_ref[...] + f_ref[...]
    gate = _sigmoid(_dot(_rms(x2, pn_ref[...]).astype(BF16), wg_ref[...]))
    o_ref[...] = x2 + gate * _dot(pe_ref[...].astype(BF16), pw_ref[...])


def _ple(x1, ffn, pe2d, ple_norm, ple_w_gate, ple_w, tm=512):
    t, d = x1.shape
    tm = min(tm, t)
    row = lambda w: pl.BlockSpec((tm, w), lambda i: (i, 0))
    return pl.pallas_call(
        _ple_kernel,
        grid=(t // tm,),
        in_specs=[row(d), row(d), row(PLE_DIM), _const_spec((1, d)), _const_spec((d, d)), _const_spec((PLE_DIM, d))],
        out_specs=row(d),
        out_shape=jax.ShapeDtypeStruct((t, d), F32),
        compiler_params=_params("parallel"),
        name="ple_residual",
    )(x1, ffn, pe2d, ple_norm.reshape(1, d), ple_w_gate.astype(BF16), ple_w.astype(BF16))


def _peer(hf, w_query, keys1, keys2, expert_u, expert_v):
    tokens, d = hf.shape
    qry_all = _plain_matmul(hf, w_query.astype(BF16))

    def block(args):
        xt, qb = args
        qry = qb.reshape(PEER_BLOCK, PEER_HEADS, 2, PEER_KEY_DIM)
        s1 = jnp.einsum('thd,kd->thk', qry[:, :, 0], keys1)
        s2 = jnp.einsum('thd,kd->thk', qry[:, :, 1], keys2)
        v1, i1 = lax.top_k(s1, PEER_TOPK)
        v2, i2 = lax.top_k(s2, PEER_TOPK)
        cand_s = (v1[..., :, None] + v2[..., None, :]).reshape(PEER_BLOCK, PEER_HEADS, PEER_TOPK * PEER_TOPK)
        cand_i = (i1[..., :, None] * PEER_KEYS + i2[..., None, :]).reshape(PEER_BLOCK, PEER_HEADS, PEER_TOPK * PEER_TOPK)
        top_s, pos = lax.top_k(cand_s, PEER_TOPK)
        idx = jnp.take_along_axis(cand_i, pos, axis=-1)
        gate = jax.nn.softmax(top_s, axis=-1)
        act = jax.nn.gelu(jnp.einsum('td,thkd->thk', xt, expert_u[idx]))
        return jnp.einsum('thk,thkd->td', gate * act, expert_v[idx])

    nblk = tokens // PEER_BLOCK
    out = lax.map(block, (hf.reshape(nblk, PEER_BLOCK, d), qry_all.reshape(nblk, PEER_BLOCK, -1)))
    return out.reshape(tokens, d)


def _plain_matmul_kernel(x_ref, w_ref, o_ref):
    o_ref[...] = _dot(x_ref[...].astype(BF16), w_ref[...])


def _plain_matmul(x2d, w_bf16, tm=1024, tn=1024):
    t, d = x2d.shape
    n = w_bf16.shape[1]
    tm, tn = min(tm, t), min(tn, n)
    return pl.pallas_call(
        _plain_matmul_kernel,
        grid=(t // tm, n // tn),
        in_specs=[pl.BlockSpec((tm, d), lambda i, j: (i, 0)), pl.BlockSpec((d, tn), lambda i, j: (0, j))],
        out_specs=pl.BlockSpec((tm, tn), lambda i, j: (i, j)),
        out_shape=jax.ShapeDtypeStruct((t, n), F32),
        compiler_params=_params("parallel", "parallel"),
        name="peer_query",
    )(x2d, w_bf16)


def _pad_w_in(w_in):
    d = w_in.shape[0]
    out = jnp.zeros((d, PROJ_COLS), BF16)
    src = 0
    for size, dst in zip(SRC_SIZES, SRC_DST):
        out = out.at[:, dst:dst + size].set(w_in[:, src:src + size].astype(BF16))
        src += size
    return out


def _layer(x2d, pe2d, b, seq, norm_mix, w_in, att_q_gain, att_k_gain, w_a_out, dn_conv, dn_a_log, dn_dt_bias,
           dn_out_gain, w_b_out, s5_a_re, s5_a_im, s5_log_dt, s5_b_re, s5_b_im, s5_c_re, s5_c_im, s5_d,
           w_c_val, w_c_gate, w_out, norm_ffn, peer_w_query, peer_keys1, peer_keys2, peer_u, peer_v,
           ple_w, ple_norm, ple_w_gate):
    proj2 = _norm_matmul(x2d, norm_mix, _pad_w_in(w_in))
    proj3 = proj2.reshape(b, seq, PROJ_COLS)
    att = [_attention_group(proj3, att_q_gain, att_k_gain, g) for g in range(len(ATT_GROUPS))]
    ob = _deltanet(proj3, dn_conv, dn_a_log, dn_dt_bias, dn_out_gain)
    s = _s5(proj3, s5_a_re, s5_a_im, s5_log_dt, s5_b_re, s5_b_im, s5_c_re, s5_c_im, s5_d)
    x1, hf = _combine(x2d, proj2, att, ob, s, w_a_out, w_b_out, w_c_val, w_c_gate, w_out, norm_ffn)
    ffn = _peer(hf, peer_w_query, peer_keys1, peer_keys2, peer_u, peer_v)
    return _ple(x1, ffn, pe2d, ple_norm, ple_w_gate, ple_w)


def kernel(x, p, norm_mix, w_in, att_q_gain, att_k_gain, w_a_out, dn_conv, dn_a_log, dn_dt_bias, dn_out_gain, w_b_out, s5_a_re, s5_a_im, s5_log_dt, s5_b_re, s5_b_im, s5_c_re, s5_c_im, s5_d, w_c_val, w_c_gate, w_out, norm_ffn, peer_w_query, peer_keys1, peer_keys2, peer_u, peer_v, ple_w, ple_norm, ple_w_gate):
    b, seq, d = x.shape
    depth = p.shape[0]
    x2d = x.reshape(b * seq, d)
    for i in range(depth):
        x2d = _layer(x2d, p[i].reshape(b * seq, PLE_DIM), b, seq, norm_mix[i], w_in[i], att_q_gain[i], att_k_gain[i],
                     w_a_out[i], dn_conv[i], dn_a_log[i], dn_dt_bias[i], dn_out_gain[i], w_b_out[i], s5_a_re[i],
                     s5_a_im[i], s5_log_dt[i], s5_b_re[i], s5_b_im[i], s5_c_re[i], s5_c_im[i], s5_d[i], w_c_val[i],
                     w_c_gate[i], w_out[i], norm_ffn[i], peer_w_query[i], peer_keys1[i], peer_keys2[i], peer_u[i],
                     peer_v[i], ple_w[i], ple_norm[i], ple_w_gate[i])
    return x2d.reshape(b, seq, d)
```

```python
import functools

import numpy as np
import jax
import jax.numpy as jnp
from jax import lax
from jax.experimental import pallas as pl
from jax.experimental.pallas import tpu as pltpu

F32 = jnp.float32
BF16 = jnp.bfloat16
HI = lax.Precision.HIGHEST

D_MODEL = 1024
NORM_EPS = 1e-6
ATT_GROUPS = ((128, 1), (512, 4), (2048, 16))
ATT_SLOTS = 4
ATT_HEAD_DIM = 64
ATT_WIDTH = 256
ATT_BLOCK = 128
ALIBI_MAX_EXP = 8.0
DN_HEADS = 4
DN_HEAD_DIM = 128
DN_WIDTH = 512
DN_CONV = 4
DN_CHUNK = 64
S5_GROUP = 16
S5_GROUPS = 16
S5_WIDTH = 256
S5_STATE = 64
PEER_HEADS = 8
PEER_KEYS = 128
PEER_TOPK = 16
PEER_KEY_DIM = 128
PEER_BLOCK = 128
PLE_DIM = 256

OFF_QKV, OFF_Z, OFF_GATES, OFF_QA, OFF_KA, OFF_VA, OFF_U, OFF_BA = 0, 1536, 2048, 5120, 5888, 6144, 6400, 6656
PROJ_COLS = 7168
SRC_SIZES = (768, 256, 256, 1536, 512, 4, 4, 256, 3072)
SRC_DST = (OFF_QA, OFF_KA, OFF_VA, OFF_QKV, OFF_Z, OFF_BA, OFF_BA + 4, OFF_U, OFF_GATES)

VMEM_LIMIT = 56 * 1024 * 1024
NEG_BIG = -1e30


def _params(*sem):
    return pltpu.CompilerParams(dimension_semantics=sem, vmem_limit_bytes=VMEM_LIMIT)


def _const_spec(shape):
    nd = len(shape)
    return pl.BlockSpec(shape, lambda *_: (0,) * nd)


def _rms(x, gain):
    return x * lax.rsqrt(jnp.mean(x * x, axis=-1, keepdims=True) + NORM_EPS) * gain


def _sigmoid(x):
    return 1.0 / (1.0 + jnp.exp(-x))


def _gelu_tanh(x):
    return 0.5 * x * (1.0 + jnp.tanh(0.7978845608028654 * (x + 0.044715 * (x * x * x))))


def _dot(a, b, **kw):
    return jnp.dot(a, b, preferred_element_type=F32, **kw)


def _dot_nt(a, b):
    return lax.dot_general(a, b, (((1,), (1,)), ((), ())), preferred_element_type=F32)


def _dot_tn(a, b):
    return lax.dot_general(a, b, (((0,), (0,)), ((), ())), preferred_element_type=F32)


def _norm_matmul_kernel(x_ref, g_ref, w_ref, o_ref, h_ref):
    @pl.when(pl.program_id(1) == 0)
    def _():
        h_ref[...] = _rms(x_ref[...], g_ref[...]).astype(BF16)

    o_ref[...] = _dot(h_ref[...], w_ref[...])


def _norm_matmul(x2d, gain, w_bf16, tm=1024, tn=1024):
    t, d = x2d.shape
    n = w_bf16.shape[1]
    tm = min(tm, t)
    tn = min(tn, n)
    return pl.pallas_call(
        _norm_matmul_kernel,
        grid=(t // tm, n // tn),
        in_specs=[
            pl.BlockSpec((tm, d), lambda i, j: (i, 0)),
            pl.BlockSpec((1, d), lambda i, j: (0, 0)),
            pl.BlockSpec((d, tn), lambda i, j: (0, j)),
        ],
        out_specs=pl.BlockSpec((tm, tn), lambda i, j: (i, j)),
        out_shape=jax.ShapeDtypeStruct((t, n), F32),
        scratch_shapes=[pltpu.VMEM((tm, d), BF16)],
        compiler_params=_params("parallel", "arbitrary"),
        name="norm_matmul",
    )(x2d, gain.reshape(1, d), w_bf16)


def _attn_kernel(q_ref, kp_ref, kc_ref, vp_ref, vc_ref, qg_ref, kg_ref, bd_ref, o_ref, lse_ref, *,
                 dilation, span, slopes):
    n = pl.program_id(2)
    bd = bd_ref[...]

    def head_norm(x, gain):
        ss = _dot(x * x, bd, precision=HI)
        return x * lax.rsqrt(ss * (1.0 / ATT_HEAD_DIM) + NORM_EPS) * gain

    q = head_norm(q_ref[0], qg_ref[...]) * (ATT_HEAD_DIM ** -0.5)
    k = head_norm(jnp.concatenate([kp_ref[0], kc_ref[0]], axis=0), kg_ref[...])
    v = jnp.concatenate([vp_ref[0], vc_ref[0]], axis=0)
    qb, kb, vb = q.astype(BF16), k.astype(BF16), v.astype(BF16)

    qpos = lax.broadcasted_iota(jnp.int32, (ATT_BLOCK, 2 * ATT_BLOCK), 0)
    kpos = lax.broadcasted_iota(jnp.int32, (ATT_BLOCK, 2 * ATT_BLOCK), 1)
    rel = qpos + ATT_BLOCK - kpos
    valid = (rel >= 0) & (rel <= span) & ((kpos >= ATT_BLOCK) | (n > 0))
    dist = (rel * dilation).astype(F32)
    lane = lax.broadcasted_iota(jnp.int32, (ATT_BLOCK, 128), 1)

    lse_all = jnp.zeros((ATT_BLOCK, 128), F32)
    outs = []
    for h in range(ATT_SLOTS):
        sl = slice(h * ATT_HEAD_DIM, (h + 1) * ATT_HEAD_DIM)
        s = _dot_nt(qb[:, sl], kb[:, sl])
        s = jnp.where(valid, s - slopes[h] * dist, NEG_BIG)
        m = jnp.max(s, axis=-1, keepdims=True)
        e = jnp.exp(s - m)
        den = jnp.sum(e, axis=-1, keepdims=True)
        outs.append(_dot(e.astype(BF16), vb[:, sl]) / den)
        lse_all = jnp.where(lane == h, m + jnp.log(den), lse_all)
    o_ref[0] = jnp.concatenate(outs, axis=-1)
    lse_ref[0] = lse_all


def _attention_group(proj3, q_gain, k_gain, group):
    b, seq, _ = proj3.shape
    window, dilation = ATT_GROUPS[group]
    span = window // dilation
    sub = seq // dilation
    nb = sub // ATT_BLOCK
    view = proj3.reshape(b, sub, dilation * PROJ_COLS)
    cb = PROJ_COLS // ATT_WIDTH
    qc, kc, vc = OFF_QA // ATT_WIDTH + group, OFF_KA // ATT_WIDTH, OFF_VA // ATT_WIDTH
    n_heads = len(ATT_GROUPS) * ATT_SLOTS
    slopes = 2.0 ** (-ALIBI_MAX_EXP * np.arange(1, n_heads + 1) / n_heads)
    slopes = tuple(float(s) for s in slopes.reshape(len(ATT_GROUPS), ATT_SLOTS)[group])
    blk = (1, ATT_BLOCK, ATT_WIDTH)
    bd = (np.arange(ATT_WIDTH)[:, None] // ATT_HEAD_DIM == np.arange(ATT_WIDTH)[None, :] // ATT_HEAD_DIM)
    o, lse = pl.pallas_call(
        functools.partial(_attn_kernel, dilation=dilation, span=span, slopes=slopes),
        grid=(b, dilation, nb),
        in_specs=[
            pl.BlockSpec(blk, lambda i, r, n: (i, n, r * cb + qc)),
            pl.BlockSpec(blk, lambda i, r, n: (i, jnp.maximum(n - 1, 0), r * cb + kc)),
            pl.BlockSpec(blk, lambda i, r, n: (i, n, r * cb + kc)),
            pl.BlockSpec(blk, lambda i, r, n: (i, jnp.maximum(n - 1, 0), r * cb + vc)),
            pl.BlockSpec(blk, lambda i, r, n: (i, n, r * cb + vc)),
            _const_spec((1, ATT_WIDTH)),
            _const_spec((1, ATT_WIDTH)),
            _const_spec((ATT_WIDTH, ATT_WIDTH)),
        ],
        out_specs=[
            pl.BlockSpec(blk, lambda i, r, n: (i, n, r)),
            pl.BlockSpec((1, ATT_BLOCK, 128), lambda i, r, n: (i, n, r)),
        ],
        out_shape=[
            jax.ShapeDtypeStruct((b, sub, dilation * ATT_WIDTH), F32),
            jax.ShapeDtypeStruct((b, sub, dilation * 128), F32),
        ],
        compiler_params=_params("parallel", "parallel", "arbitrary"),
        name=f"dilated_attention_g{group}",
    )(view, view, view, view, view,
      jnp.tile(q_gain, ATT_SLOTS).reshape(1, ATT_WIDTH), jnp.tile(k_gain, ATT_SLOTS).reshape(1, ATT_WIDTH),
      jnp.asarray(bd, F32))
    return o.reshape(b * seq, ATT_WIDTH), lse.reshape(b * seq, 128)


DN_ROWS = 256


def _deltanet_kernel(qkv_ref, z_ref, ba_ref, cw_ref, alog_ref, dtb_ref, og_ref, tril_ref, o_ref,
                     state_ref, tail_ref):
    c = DN_CHUNK
    rows = qkv_ref.shape[1]

    @pl.when(pl.program_id(1) == 0)
    def _():
        state_ref[...] = jnp.zeros_like(state_ref)
        tail_ref[...] = jnp.zeros_like(tail_ref)

    x = qkv_ref[0]
    xc = jnp.concatenate([tail_ref[...], x], axis=0)
    cw = cw_ref[...]
    acc = cw[DN_CONV - 1:DN_CONV] * x
    for j in range(DN_CONV - 1):
        acc = acc + cw[j:j + 1] * xc[5 + j:5 + j + rows]
    tail_ref[...] = x[rows - 8:rows]
    a = acc * _sigmoid(acc)

    ba = ba_ref[0]
    beta_all = _sigmoid(ba)
    sp = ba + dtb_ref[...]
    sp = jnp.maximum(sp, 0.0) + jnp.log(1.0 + jnp.exp(-jnp.abs(sp)))
    g_all = -(jnp.exp(alog_ref[...]) * sp)
    tril = tril_ref[...]
    ii = lax.broadcasted_iota(jnp.int32, (c, c), 0)
    jj = lax.broadcasted_iota(jnp.int32, (c, c), 1)
    incl = ii >= jj
    strict = ii > jj
    eye = (ii == jj).astype(F32)
    z = z_ref[0]
    og = og_ref[...]

    outs = []
    for h in range(DN_HEADS):
        hs = slice(h * DN_HEAD_DIM, (h + 1) * DN_HEAD_DIM)
        qh = a[:, hs]
        kh = a[:, DN_WIDTH + h * DN_HEAD_DIM:DN_WIDTH + (h + 1) * DN_HEAD_DIM]
        vh = a[:, 2 * DN_WIDTH + h * DN_HEAD_DIM:2 * DN_WIDTH + (h + 1) * DN_HEAD_DIM]
        qh = qh * lax.rsqrt(jnp.sum(qh * qh, axis=-1, keepdims=True) + NORM_EPS) * (DN_HEAD_DIM ** -0.5)
        kh = kh * lax.rsqrt(jnp.sum(kh * kh, axis=-1, keepdims=True) + NORM_EPS)
        beta = beta_all[:, h:h + 1]
        gh = jnp.broadcast_to(g_all[:, DN_HEADS + h:DN_HEADS + h + 1], (rows, DN_HEAD_DIM))
        state = state_ref[h]
        o_chunks = []
        for ci in range(rows // c):
            rs = slice(ci * c, (ci + 1) * c)
            gc = _dot(tril, gh[rs], precision=HI)
            gc64 = gc[:, :c]
            diff = gc64 - gc64.T
            decay = jnp.where(incl, jnp.exp(jnp.where(incl, diff, 0.0)), 0.0)
            eg = jnp.exp(gc)
            g_last = gc[c - 1:c, :]
            kc_, qc_, vc_, bc_ = kh[rs], qh[rs], vh[rs], beta[rs]
            kbeta = kc_ * bc_
            kcb = kc_.astype(BF16)
            lower = jnp.where(strict, _dot_nt(kbeta.astype(BF16), kcb) * decay, 0.0)
            t_inv = eye - lower
            pw = _dot(lower, lower, precision=HI)
            for it in range(5):
                t_inv = t_inv + _dot(t_inv, pw, precision=HI)
                if it < 4:
                    pw = _dot(pw, pw, precision=HI)
            t_b = t_inv.astype(BF16)
            u = _dot(t_b, (vc_ * bc_).astype(BF16))
            w = _dot(t_b, (kbeta * eg).astype(BF16))
            attn = jnp.where(incl, _dot_nt(qc_.astype(BF16), kcb) * decay, 0.0)
            q_dec = qc_ * eg
            k_dec = kc_ * jnp.exp(g_last - gc)
            sb = state.astype(BF16)
            v_new = u - _dot(w.astype(BF16), sb)
            vnb = v_new.astype(BF16)
            o_chunks.append(_dot(q_dec.astype(BF16), sb) + _dot(attn.astype(BF16), vnb))
            state = state * jnp.exp(g_last) + _dot_tn(k_dec.astype(BF16), vnb)
        state_ref[h] = state
        oh = jnp.concatenate(o_chunks, axis=0)
        zh = z[:, hs]
        outs.append(_rms(oh, og) * (zh * _sigmoid(zh)))
    o_ref[0] = jnp.concatenate(outs, axis=-1)


def _deltanet(proj3, conv_w, a_log, dt_bias, out_gain):
    b, seq, _ = proj3.shape
    rows = min(DN_ROWS, seq)
    pad_a = jnp.zeros((1, 128), F32).at[0, DN_HEADS:2 * DN_HEADS].set(a_log)
    pad_d = jnp.zeros((1, 128), F32).at[0, DN_HEADS:2 * DN_HEADS].set(dt_bias)
    tril = jnp.asarray(np.tril(np.ones((DN_CHUNK, DN_CHUNK), np.float32)))
    out = pl.pallas_call(
        _deltanet_kernel,
        grid=(b, seq // rows),
        in_specs=[
            pl.BlockSpec((1, rows, 3 * DN_WIDTH), lambda i, t: (i, t, OFF_QKV // (3 * DN_WIDTH))),
            pl.BlockSpec((1, rows, DN_WIDTH), lambda i, t: (i, t, OFF_Z // DN_WIDTH)),
            pl.BlockSpec((1, rows, 128), lambda i, t: (i, t, OFF_BA // 128)),
            _const_spec((DN_CONV, 3 * DN_WIDTH)),
            _const_spec((1, 128)),
            _const_spec((1, 128)),
            _const_spec((1, DN_HEAD_DIM)),
            _const_spec((DN_CHUNK, DN_CHUNK)),
        ],
        out_specs=pl.BlockSpec((1, rows, DN_WIDTH), lambda i, t: (i, t, 0)),
        out_shape=jax.ShapeDtypeStruct((b, seq, DN_WIDTH), F32),
        scratch_shapes=[pltpu.VMEM((DN_HEADS, DN_HEAD_DIM, DN_HEAD_DIM), F32), pltpu.VMEM((8, 3 * DN_WIDTH), F32)],
        compiler_params=_params("parallel", "arbitrary"),
        name="gated_deltanet",
    )(proj3, proj3, proj3, conv_w, pad_a, pad_d, out_gain.reshape(1, DN_HEAD_DIM), tril)
    return out.reshape(b * seq, DN_WIDTH)


S5_ROWS = 128
S5_N = S5_GROUPS * S5_STATE


def _s5_kernel(u_ref, bbd_ref, pinv_r_ref, pinv_i_ref, pow_r_ref, pow_i_ref, pow1_r_ref, pow1_i_ref,
               cr_ref, ci_ref, d_ref, tril_ref, o_ref, sr_ref, si_ref):
    @pl.when(pl.program_id(1) == 0)
    def _():
        sr_ref[...] = jnp.zeros_like(sr_ref)
        si_ref[...] = jnp.zeros_like(si_ref)

    u = u_ref[0]
    bu = _dot(u.astype(BF16), bbd_ref[...])
    br, bi = bu[:, :S5_N], bu[:, S5_N:]
    pr, pi = pinv_r_ref[...], pinv_i_ref[...]
    zr = pr * br - pi * bi
    zi = pr * bi + pi * br
    tril = tril_ref[...]
    cr = _dot(tril, zr, precision=HI)
    ci = _dot(tril, zi, precision=HI)
    wr, wi = pow_r_ref[...], pow_i_ref[...]
    p1r, p1i = pow1_r_ref[...], pow1_i_ref[...]
    sr, si = sr_ref[...], si_ref[...]
    xr = wr * cr - wi * ci + (p1r * sr - p1i * si)
    xi = wr * ci + wi * cr + (p1r * si + p1i * sr)
    rows = xr.shape[0]
    sr_ref[...] = xr[rows - 1:rows]
    si_ref[...] = xi[rows - 1:rows]
    y = _dot(xr.astype(BF16), cr_ref[...]) - _dot(xi.astype(BF16), ci_ref[...]) + d_ref[...] * u
    o_ref[0] = _gelu_tanh(y)


def _s5(proj3, a_re, a_im, log_dt, b_re, b_im, c_re, c_im, d_skip):
    b, seq, _ = proj3.shape
    rows = min(S5_ROWS, seq)
    g, n, p = S5_GROUPS, S5_STATE, S5_GROUP
    dt = jnp.exp(log_dt)[:, None]
    lr, li = a_re, a_im
    mag = jnp.exp(lr * dt)
    ab_r, ab_i = mag * jnp.cos(li * dt), mag * jnp.sin(li * dt)
    den = lr * lr + li * li
    nr, ni = ab_r - 1.0, ab_i
    cr, ci = (nr * lr + ni * li) / den, (ni * lr - nr * li) / den
    bb_r = cr[..., None] * b_re - ci[..., None] * b_im
    bb_i = cr[..., None] * b_im + ci[..., None] * b_re
    eye_g = jnp.eye(g, dtype=F32)
    bbd_r = jnp.einsum('gnp,gh->gphn', bb_r, eye_g).reshape(g * p, g * n)
    bbd_i = jnp.einsum('gnp,gh->gphn', bb_i, eye_g).reshape(g * p, g * n)
    bbd = jnp.concatenate([bbd_r, bbd_i], axis=1).astype(BF16)
    cbd_r = jnp.einsum('gpn,gh->gnhp', c_re, eye_g).reshape(g * n, g * p).astype(BF16)
    cbd_i = jnp.einsum('gpn,gh->gnhp', c_im, eye_g).reshape(g * n, g * p).astype(BF16)
    steps = jnp.arange(rows, dtype=F32)[:, None]
    lrd = (lr * dt).reshape(1, g * n)
    lid = (li * dt).reshape(1, g * n)

    def power(k):
        m = jnp.exp(lrd * k)
        return m * jnp.cos(lid * k), m * jnp.sin(lid * k)

    pinv_r, pinv_i = power(-steps)
    pow_r, pow_i = power(steps)
    pow1_r, pow1_i = power(steps + 1.0)
    tril = jnp.asarray(np.tril(np.ones((rows, rows), np.float32)))
    tab = _const_spec((rows, g * n))
    out = pl.pallas_call(
        _s5_kernel,
        grid=(b, seq // rows),
        in_specs=[
            pl.BlockSpec((1, rows, S5_WIDTH), lambda i, t: (i, t, OFF_U // S5_WIDTH)),
            _const_spec((S5_WIDTH, 2 * g * n)),
            tab, tab, tab, tab, tab, tab,
            _const_spec((g * n, S5_WIDTH)),
            _const_spec((g * n, S5_WIDTH)),
            _const_spec((1, S5_WIDTH)),
            _const_spec((rows, rows)),
        ],
        out_specs=pl.BlockSpec((1, rows, S5_WIDTH), lambda i, t: (i, t, 0)),
        out_shape=jax.ShapeDtypeStruct((b, seq, S5_WIDTH), F32),
        scratch_shapes=[pltpu.VMEM((1, g * n), F32), pltpu.VMEM((1, g * n), F32)],
        compiler_params=_params("parallel", "arbitrary"),
        name="s5_mixer",
    )(proj3, bbd, pinv_r, pinv_i, pow_r, pow_i, pow1_r, pow1_i, cbd_r, cbd_i, d_skip.reshape(1, S5_WIDTH), tril)
    return out.reshape(b * seq, S5_WIDTH)


def _combine_kernel(x_ref, ga_ref, gb_ref, gc_ref, o1_ref, o2_ref, o3_ref, l1_ref, l2_ref, l3_ref,
                    ob_ref, s_ref, wa_ref, wb_ref, wcv_ref, wcg_ref, wo_ref, ex_ref, nf_ref,
                    x1_ref, hf_ref):
    l1, l2, l3 = l1_ref[...], l2_ref[...], l3_ref[...]
    m = jnp.maximum(jnp.maximum(l1, l2), l3)
    e1, e2, e3 = jnp.exp(l1 - m), jnp.exp(l2 - m), jnp.exp(l3 - m)
    inv = 1.0 / (e1 + e2 + e3)
    ex = ex_ref[...]
    mix = (_dot(e1 * inv, ex, precision=HI) * o1_ref[...]
           + _dot(e2 * inv, ex, precision=HI) * o2_ref[...]
           + _dot(e3 * inv, ex, precision=HI) * o3_ref[...])
    y_a = _dot(mix.astype(BF16), wa_ref[...])
    y_b = _dot(ob_ref[...].astype(BF16), wb_ref[...])
    sb = s_ref[...].astype(BF16)
    y_c = _dot(sb, wcv_ref[...]) * _sigmoid(_dot(sb, wcg_ref[...]))
    mixed = _sigmoid(ga_ref[...]) * y_a + _sigmoid(gb_ref[...]) * y_b + _sigmoid(gc_ref[...]) * y_c
    x1 = x_ref[...] + _dot(mixed.astype(BF16), wo_ref[...])
    x1_ref[...] = x1
    hf_ref[...] = _rms(x1, nf_ref[...])


def _combine(x2d, proj2, att, ob, s, w_a_out, w_b_out, w_c_val, w_c_gate, w_out, norm_ffn, tm=512):
    t, d = x2d.shape
    tm = min(tm, t)
    (o1, l1), (o2, l2), (o3, l3) = att
    ex = np.zeros((128, ATT_WIDTH), np.float32)
    for h in range(ATT_SLOTS):
        ex[h, h * ATT_HEAD_DIM:(h + 1) * ATT_HEAD_DIM] = 1.0
    row = lambda w: pl.BlockSpec((tm, w), lambda i: (i, 0))
    gate = lambda k: pl.BlockSpec((tm, d), lambda i: (i, OFF_GATES // d + k))
    bf = lambda w: w.astype(BF16)
    return pl.pallas_call(
        _combine_kernel,
        grid=(t // tm,),
        in_specs=[
            row(d), gate(0), gate(1), gate(2),
            row(ATT_WIDTH), row(ATT_WIDTH), row(ATT_WIDTH), row(128), row(128), row(128),
            row(DN_WIDTH), row(S5_WIDTH),
            _const_spec((ATT_WIDTH, d)), _const_spec((DN_WIDTH, d)), _const_spec((S5_WIDTH, d)),
            _const_spec((S5_WIDTH, d)), _const_spec((d, d)), _const_spec((128, ATT_WIDTH)), _const_spec((1, d)),
        ],
        out_specs=[row(d), row(d)],
        out_shape=[jax.ShapeDtypeStruct((t, d), F32), jax.ShapeDtypeStruct((t, d), F32)],
        compiler_params=_params("parallel"),
        name="branch_combine",
    )(x2d, proj2, proj2, proj2, o1, o2, o3, l1, l2, l3, ob, s,
      bf(w_a_out), bf(w_b_out), bf(w_c_val), bf(w_c_gate), bf(w_out), jnp.asarray(ex), norm_ffn.reshape(1, d))


def _ple_kernel(x_ref, f_ref, pe_ref, pn_ref, wg_ref, pw_ref, o_ref):
    x2 = x_ref[...] + f_ref[...]
    gate = _sigmoid(_dot(_rms(x2, pn_ref[...]).astype(BF16), wg_ref[...]))
    o_ref[...] = x2 + gate * _dot(pe_ref[...].astype(BF16), pw_ref[...])


def _ple(x1, ffn, pe2d, ple_norm, ple_w_gate, ple_w, tm=512):
    t, d = x1.shape
    tm = min(tm, t)
    row = lambda w: pl.BlockSpec((tm, w), lambda i: (i, 0))
    return pl.pallas_call(
        _ple_kernel,
        grid=(t // tm,),
        in_specs=[row(d), row(d), row(PLE_DIM), _const_spec((1, d)), _const_spec((d, d)), _const_spec((PLE_DIM, d))],
        out_specs=row(d),
        out_shape=jax.ShapeDtypeStruct((t, d), F32),
        compiler_params=_params("parallel"),
        name="ple_residual",
    )(x1, ffn, pe2d, ple_norm.reshape(1, d), ple_w_gate.astype(BF16), ple_w.astype(BF16))


PEER_PICKS = PEER_HEADS * PEER_TOPK
PEER_EXPERTS = PEER_KEYS * PEER_KEYS
PEER_HALF = D_MODEL // 2
ROUTE_ROWS = 256
EXPERT_ROWS = 256
UP_BATCH = 32


def _top_rows(work, row_id, k):
    vals, ids = [], []
    big = work.shape[0]
    for _ in range(k):
        m = jnp.max(work, axis=0, keepdims=True)
        pos = jnp.min(jnp.where(work == m, row_id, big), axis=0, keepdims=True)
        vals.append(m)
        ids.append(pos)
        work = jnp.where(row_id == pos, -jnp.inf, work)
    return vals, ids


def _peer_route_kernel(hf_ref, wq_ref, k1_ref, k2_ref, idx_ref, gate_ref):
    rows = hf_ref.shape[0]
    qry = _dot(hf_ref[...].astype(BF16), wq_ref[...]).astype(BF16)
    k1, k2 = k1_ref[...], k2_ref[...]
    key_id = lax.broadcasted_iota(jnp.int32, (PEER_KEYS, rows), 0)
    cand_id = lax.broadcasted_iota(jnp.int32, (PEER_TOPK * PEER_TOPK, rows), 0)
    idx_rows, gate_rows = [], []
    for h in range(PEER_HEADS):
        q1 = qry[:, (2 * h) * PEER_KEY_DIM:(2 * h + 1) * PEER_KEY_DIM]
        q2 = qry[:, (2 * h + 1) * PEER_KEY_DIM:(2 * h + 2) * PEER_KEY_DIM]
        v1, i1 = _top_rows(_dot_nt(k1, q1), key_id, PEER_TOPK)
        v2, i2 = _top_rows(_dot_nt(k2, q2), key_id, PEER_TOPK)
        v2c = jnp.concatenate(v2, axis=0)
        i2c = jnp.concatenate(i2, axis=0)
        cand_s = jnp.concatenate([a + v2c for a in v1], axis=0)
        cand_i = jnp.concatenate([a * PEER_KEYS + i2c for a in i1], axis=0)
        top_s, pos = _top_rows(cand_s, cand_id, PEER_TOPK)
        e = [jnp.exp(s - top_s[0]) for s in top_s]
        den = e[0]
        for x in e[1:]:
            den = den + x
        inv = 1.0 / den
        for s, p in zip(e, pos):
            gate_rows.append(s * inv)
            idx_rows.append(jnp.sum(jnp.where(cand_id == p, cand_i, 0), axis=0, keepdims=True))
    idx_ref[...] = jnp.concatenate(idx_rows, axis=0).T
    gate_ref[...] = jnp.concatenate(gate_rows, axis=0).T


def _peer_route(hf, w_query, keys1, keys2):
    t, d = hf.shape
    rows = min(ROUTE_ROWS, t)
    nq = w_query.shape[1]
    return pl.pallas_call(
        _peer_route_kernel,
        grid=(t // rows,),
        in_specs=[pl.BlockSpec((rows, d), lambda i: (i, 0)), _const_spec((d, nq)),
                  _const_spec((PEER_KEYS, PEER_KEY_DIM)), _const_spec((PEER_KEYS, PEER_KEY_DIM))],
        out_specs=[pl.BlockSpec((rows, PEER_PICKS), lambda i: (i, 0)), pl.BlockSpec((rows, PEER_PICKS), lambda i: (i, 0))],
        out_shape=[jax.ShapeDtypeStruct((t, PEER_PICKS), jnp.int32), jax.ShapeDtypeStruct((t, PEER_PICKS), F32)],
        compiler_params=_params("parallel"),
        name="peer_route",
    )(hf, w_query.astype(BF16), keys1.astype(BF16), keys2.astype(BF16))


def _pack_table(table):
    bits = lax.bitcast_convert_type(table.astype(BF16), jnp.uint16).astype(jnp.uint32)
    words = bits[:, :PEER_HALF] | (bits[:, PEER_HALF:] << 16)
    return words.reshape(table.shape[0], PEER_HALF // 128, 128)


def _unpack_row(words):
    lo = lax.bitcast_convert_type(words << 16, F32)
    hi = lax.bitcast_convert_type(words & jnp.uint32(0xFFFF0000), F32)
    return lo, hi


def _load_table_once(tab_hbm, tab_vmem, sem):
    @pl.when(pl.program_id(0) == 0)
    def _():
        cp = pltpu.make_async_copy(tab_hbm, tab_vmem, sem)
        cp.start()
        cp.wait()


def _to_smem(src_vmem, dst_smem, sem):
    cp = pltpu.make_async_copy(src_vmem, dst_smem, sem)
    cp.start()
    cp.wait()


def _peer_up_kernel(idx_ref, x_ref, gate_ref, sel_ref, tab_hbm, o_ref, tab_ref, idx_smem, part_ref, sems):
    rows = idx_ref.shape[0]
    _load_table_once(tab_hbm, tab_ref, sems.at[0])
    _to_smem(idx_ref, idx_smem, sems.at[1])
    half = PEER_HALF // 128
    batch = part_ref.shape[0] // 8
    r_id = lax.broadcasted_iota(jnp.int32, (8, PEER_PICKS), 0)
    c_id = lax.broadcasted_iota(jnp.int32, (8, PEER_PICKS), 1)
    diag = (c_id % 8 == r_id)[None]

    def token_batch(b, carry):
        base = pl.multiple_of(b * batch, batch)

        def token(tb, c):
            t = base + tb
            x_lo = x_ref[t, 0:half, :]
            x_hi = x_ref[t, half:2 * half, :]
            for j in range(PEER_PICKS // 8):
                parts = []
                for i in range(8):
                    lo, hi = _unpack_row(tab_ref[idx_smem[t, j * 8 + i]])
                    parts.append(jnp.sum(lo * x_lo + hi * x_hi, axis=0, keepdims=True))
                part_ref[pl.ds(pl.multiple_of(tb * 8, 8), 8), j * 128:(j + 1) * 128] = jnp.concatenate(parts, axis=0)
            return c

        lax.fori_loop(0, batch, token, 0)
        part = part_ref[...]
        p_hi = part.astype(BF16)
        p_lo = (part - p_hi.astype(F32)).astype(BF16)
        sel = sel_ref[...]
        sums = (_dot(p_hi, sel) + _dot(p_lo, sel)).reshape(batch, 8, PEER_PICKS)
        act = jnp.sum(jnp.where(diag, sums, 0.0), axis=1)
        o_ref[pl.ds(base, batch), :] = gate_ref[pl.ds(base, batch), :] * _gelu_tanh(act)
        return carry

    lax.fori_loop(0, rows // batch, token_batch, 0)


def _peer_down_kernel(idx_ref, w_ref, tab_hbm, o_ref, tab_ref, idx_smem, w_smem, sems):
    rows = idx_ref.shape[0]
    _load_table_once(tab_hbm, tab_ref, sems.at[0])
    _to_smem(idx_ref, idx_smem, sems.at[1])
    _to_smem(w_ref, w_smem, sems.at[2])
    half = PEER_HALF // 128
    n_acc = 4

    def token(t, carry):
        acc_lo = [jnp.zeros((half, 128), F32) for _ in range(n_acc)]
        acc_hi = [jnp.zeros((half, 128), F32) for _ in range(n_acc)]
        for p in range(PEER_PICKS):
            lo, hi = _unpack_row(tab_ref[idx_smem[t, p]])
            w = w_smem[t, p]
            acc_lo[p % n_acc] = acc_lo[p % n_acc] + w * lo
            acc_hi[p % n_acc] = acc_hi[p % n_acc] + w * hi
        o_ref[t, 0:half, :] = (acc_lo[0] + acc_lo[1]) + (acc_lo[2] + acc_lo[3])
        o_ref[t, half:2 * half, :] = (acc_hi[0] + acc_hi[1]) + (acc_hi[2] + acc_hi[3])
        return carry

    lax.fori_loop(0, rows, token, 0)


def _peer_experts(hf, idx, gate, expert_u, expert_v):
    t, d = hf.shape
    rows = min(EXPERT_ROWS, t)
    sub = d // 128
    x3 = hf.reshape(t, sub, 128)
    sel = np.zeros((PEER_PICKS // 8, 128, PEER_PICKS), np.float32)
    for j in range(PEER_PICKS // 8):
        sel[j, :, 8 * j:8 * j + 8] = 1.0
    sel = jnp.asarray(sel.reshape(PEER_PICKS // 8 * 128, PEER_PICKS), BF16)
    tab_shape = (PEER_EXPERTS, PEER_HALF // 128, 128)
    picks = pl.BlockSpec((rows, PEER_PICKS), lambda i: (i, 0))
    tok3 = pl.BlockSpec((rows, sub, 128), lambda i: (i, 0, 0))
    w = pl.pallas_call(
        _peer_up_kernel,
        grid=(t // rows,),
        in_specs=[picks, tok3, picks, _const_spec(sel.shape), pl.BlockSpec(memory_space=pl.ANY)],
        out_specs=picks,
        out_shape=jax.ShapeDtypeStruct((t, PEER_PICKS), F32),
        scratch_shapes=[pltpu.VMEM(tab_shape, jnp.uint32), pltpu.SMEM((rows, PEER_PICKS), jnp.int32),
                        pltpu.VMEM((UP_BATCH * 8, PEER_PICKS // 8 * 128), F32), pltpu.SemaphoreType.DMA((2,))],
        compiler_params=_params("arbitrary"),
        name="peer_up",
    )(idx, x3, gate, sel, _pack_table(expert_u))
    out = pl.pallas_call(
        _peer_down_kernel,
        grid=(t // rows,),
        in_specs=[picks, picks, pl.BlockSpec(memory_space=pl.ANY)],
        out_specs=tok3,
        out_shape=jax.ShapeDtypeStruct((t, sub, 128), F32),
        scratch_shapes=[pltpu.VMEM(tab_shape, jnp.uint32), pltpu.SMEM((rows, PEER_PICKS), jnp.int32),
                        pltpu.SMEM((rows, PEER_PICKS), F32), pltpu.SemaphoreType.DMA((3,))],
        compiler_params=_params("arbitrary"),
        name="peer_down",
    )(idx, w, _pack_table(expert_v))
    return out.reshape(t, d)


def _peer(hf, w_query, keys1, keys2, expert_u, expert_v):
    idx, gate = _peer_route(hf, w_query, keys1, keys2)
    return _peer_experts(hf, idx, gate, expert_u, expert_v)


def _pad_w_in(w_in):
    d = w_in.shape[0]
    out = jnp.zeros((d, PROJ_COLS), BF16)
    src = 0
    for size, dst in zip(SRC_SIZES, SRC_DST):
        out = out.at[:, dst:dst + size].set(w_in[:, src:src + size].astype(BF16))
        src += size
    return out


def _layer(x2d, pe2d, b, seq, norm_mix, w_in, att_q_gain, att_k_gain, w_a_out, dn_conv, dn_a_log, dn_dt_bias,
           dn_out_gain, w_b_out, s5_a_re, s5_a_im, s5_log_dt, s5_b_re, s5_b_im, s5_c_re, s5_c_im, s5_d,
           w_c_val, w_c_gate, w_out, norm_ffn, peer_w_query, peer_keys1, peer_keys2, peer_u, peer_v,
           ple_w, ple_norm, ple_w_gate):
    proj2 = _norm_matmul(x2d, norm_mix, _pad_w_in(w_in))
    proj3 = proj2.reshape(b, seq, PROJ_COLS)
    att = [_attention_group(proj3, att_q_gain, att_k_gain, g) for g in range(len(ATT_GROUPS))]
    ob = _deltanet(proj3, dn_conv, dn_a_log, dn_dt_bias, dn_out_gain)
    s = _s5(proj3, s5_a_re, s5_a_im, s5_log_dt, s5_b_re, s5_b_im, s5_c_re, s5_c_im, s5_d)
    x1, hf = _combine(x2d, proj2, att, ob, s, w_a_out, w_b_out, w_c_val, w_c_gate, w_out, norm_ffn)
    ffn = _peer(hf, peer_w_query, peer_keys1, peer_keys2, peer_u, peer_v)
    return _ple(x1, ffn, pe2d, ple_norm, ple_w_gate, ple_w)


def kernel(x, p, norm_mix, w_in, att_q_gain, att_k_gain, w_a_out, dn_conv, dn_a_log, dn_dt_bias, dn_out_gain, w_b_out, s5_a_re, s5_a_im, s5_log_dt, s5_b_re, s5_b_im, s5_c_re, s5_c_im, s5_d, w_c_val, w_c_gate, w_out, norm_ffn, peer_w_query, peer_keys1, peer_keys2, peer_u, peer_v, ple_w, ple_norm, ple_w_gate):
    b, seq, d = x.shape
    depth = p.shape[0]
    x2d = x.reshape(b * seq, d)
    for i in range(depth):
        x2d = _layer(x2d, p[i].reshape(b * seq, PLE_DIM), b, seq, norm_mix[i], w_in[i], att_q_gain[i], att_k_gain[i],
                     w_a_out[i], dn_conv[i], dn_a_log[i], dn_dt_bias[i], dn_out_gain[i], w_b_out[i], s5_a_re[i],
                     s5_a_im[i], s5_log_dt[i], s5_b_re[i], s5_b_im[i], s5_c_re[i], s5_c_im[i], s5_d[i], w_c_val[i],
                     w_c_gate[i], w_out[i], norm_ffn[i], peer_w_query[i], peer_keys1[i], peer_keys2[i], peer_u[i],
                     peer_v[i], ple_w[i], ple_norm[i], ple_w_gate[i])
    return x2d.reshape(b, seq, d)
```

```python
import functools

import numpy as np
import jax
import jax.numpy as jnp
from jax import lax
from jax.experimental import pallas as pl
from jax.experimental.pallas import tpu as pltpu

F32 = jnp.float32
BF16 = jnp.bfloat16
HI = lax.Precision.HIGHEST

D_MODEL = 1024
NORM_EPS = 1e-6
ATT_GROUPS = ((128, 1), (512, 4), (2048, 16))
ATT_SLOTS = 4
ATT_HEAD_DIM = 64
ATT_WIDTH = 256
ATT_BLOCK = 128
ALIBI_MAX_EXP = 8.0
DN_HEADS = 4
DN_HEAD_DIM = 128
DN_WIDTH = 512
DN_CONV = 4
DN_CHUNK = 64
S5_GROUP = 16
S5_GROUPS = 16
S5_WIDTH = 256
S5_STATE = 64
PEER_HEADS = 8
PEER_KEYS = 128
PEER_TOPK = 16
PEER_KEY_DIM = 128
PEER_BLOCK = 128
PLE_DIM = 256

OFF_QKV, OFF_Z, OFF_GATES, OFF_U, OFF_BA = 0, 1536, 2048, 5120, 5376
PROJ_COLS = 5632
OFF_QA, OFF_KA, OFF_VA = 0, 768, 1024
ATT_COLS = 1280
SRC_SIZES = (768, 256, 256, 1536, 512, 4, 4, 256, 3072)
SRC_DST = ((1, OFF_QA), (1, OFF_KA), (1, OFF_VA), (0, OFF_QKV), (0, OFF_Z), (0, OFF_BA), (0, OFF_BA + 4),
           (0, OFF_U), (0, OFF_GATES))

VMEM_LIMIT = 56 * 1024 * 1024
NEG_BIG = -1e30


def _params(*sem):
    return pltpu.CompilerParams(dimension_semantics=sem, vmem_limit_bytes=VMEM_LIMIT)


def _const_spec(shape):
    nd = len(shape)
    return pl.BlockSpec(shape, lambda *_: (0,) * nd)


def _rms(x, gain):
    return x * lax.rsqrt(jnp.mean(x * x, axis=-1, keepdims=True) + NORM_EPS) * gain


def _sigmoid(x):
    return 1.0 / (1.0 + jnp.exp(-x))


def _gelu_tanh(x):
    return 0.5 * x * (1.0 + jnp.tanh(0.7978845608028654 * (x + 0.044715 * (x * x * x))))


def _dot(a, b, **kw):
    return jnp.dot(a, b, preferred_element_type=F32, **kw)


def _dot3(a, b):
    a_hi, b_hi = a.astype(BF16), b.astype(BF16)
    a_lo = (a - a_hi.astype(F32)).astype(BF16)
    b_lo = (b - b_hi.astype(F32)).astype(BF16)
    return _dot(a_hi, b_hi) + (_dot(a_hi, b_lo) + _dot(a_lo, b_hi))


def _dot_nt(a, b):
    return lax.dot_general(a, b, (((1,), (1,)), ((), ())), preferred_element_type=F32)


def _dot_tn(a, b):
    return lax.dot_general(a, b, (((0,), (0,)), ((), ())), preferred_element_type=F32)


def _norm_matmul_kernel(x_ref, g_ref, w_ref, o_ref, h_ref):
    @pl.when(pl.program_id(1) == 0)
    def _():
        h_ref[...] = _rms(x_ref[...], g_ref[...]).astype(BF16)

    o_ref[...] = _dot(h_ref[...], w_ref[...])


def _norm_matmul(x2d, gain, w_bf16, tm=1024, tn=1024):
    t, d = x2d.shape
    n = w_bf16.shape[1]
    tm = min(tm, t)
    tn = min(tn, n)
    return pl.pallas_call(
        _norm_matmul_kernel,
        grid=(t // tm, n // tn),
        in_specs=[
            pl.BlockSpec((tm, d), lambda i, j: (i, 0)),
            pl.BlockSpec((1, d), lambda i, j: (0, 0)),
            pl.BlockSpec((d, tn), lambda i, j: (0, j)),
        ],
        out_specs=pl.BlockSpec((tm, tn), lambda i, j: (i, j)),
        out_shape=jax.ShapeDtypeStruct((t, n), F32),
        scratch_shapes=[pltpu.VMEM((tm, d), BF16)],
        compiler_params=_params("parallel", "arbitrary"),
        name="norm_matmul",
    )(x2d, gain.reshape(1, d), w_bf16)


def _attn_kernel(q_ref, kp_ref, kc_ref, vp_ref, vc_ref, qg_ref, kg_ref, bd_ref, o_ref, lse_ref, *,
                 dilation, span, slopes):
    n = pl.program_id(2)
    bd = bd_ref[...]

    def head_norm(x, gain):
        ss = _dot(x * x, bd, precision=HI)
        return x * lax.rsqrt(ss * (1.0 / ATT_HEAD_DIM) + NORM_EPS) * gain

    q = head_norm(q_ref[0], qg_ref[...]) * (ATT_HEAD_DIM ** -0.5)
    k = head_norm(jnp.concatenate([kp_ref[0], kc_ref[0]], axis=0), kg_ref[...])
    v = jnp.concatenate([vp_ref[0], vc_ref[0]], axis=0)
    qb, kb, vb = q.astype(BF16), k.astype(BF16), v.astype(BF16)

    qpos = lax.broadcasted_iota(jnp.int32, (ATT_BLOCK, 2 * ATT_BLOCK), 0)
    kpos = lax.broadcasted_iota(jnp.int32, (ATT_BLOCK, 2 * ATT_BLOCK), 1)
    rel = qpos + ATT_BLOCK - kpos
    valid = (rel >= 0) & (rel <= span) & ((kpos >= ATT_BLOCK) | (n > 0))
    dist = (rel * dilation).astype(F32)
    lane = lax.broadcasted_iota(jnp.int32, (ATT_BLOCK, 128), 1)

    lse_all = jnp.zeros((ATT_BLOCK, 128), F32)
    outs = []
    for h in range(ATT_SLOTS):
        sl = slice(h * ATT_HEAD_DIM, (h + 1) * ATT_HEAD_DIM)
        s = _dot_nt(qb[:, sl], kb[:, sl])
        s = jnp.where(valid, s - slopes[h] * dist, NEG_BIG)
        m = jnp.max(s, axis=-1, keepdims=True)
        e = jnp.exp(s - m)
        den = jnp.sum(e, axis=-1, keepdims=True)
        outs.append(_dot(e.astype(BF16), vb[:, sl]) / den)
        lse_all = jnp.where(lane == h, m + jnp.log(den), lse_all)
    o_ref[0] = jnp.concatenate(outs, axis=-1)
    lse_ref[0] = lse_all


def _attention_group(proj3, q_gain, k_gain, group):
    b, seq, _ = proj3.shape
    window, dilation = ATT_GROUPS[group]
    span = window // dilation
    sub = seq // dilation
    nb = sub // ATT_BLOCK
    view = proj3.reshape(b, sub, dilation * ATT_COLS)
    cb = ATT_COLS // ATT_WIDTH
    qc, kc, vc = OFF_QA // ATT_WIDTH + group, OFF_KA // ATT_WIDTH, OFF_VA // ATT_WIDTH
    n_heads = len(ATT_GROUPS) * ATT_SLOTS
    slopes = 2.0 ** (-ALIBI_MAX_EXP * np.arange(1, n_heads + 1) / n_heads)
    slopes = tuple(float(s) for s in slopes.reshape(len(ATT_GROUPS), ATT_SLOTS)[group])
    blk = (1, ATT_BLOCK, ATT_WIDTH)
    bd = (np.arange(ATT_WIDTH)[:, None] // ATT_HEAD_DIM == np.arange(ATT_WIDTH)[None, :] // ATT_HEAD_DIM)
    o, lse = pl.pallas_call(
        functools.partial(_attn_kernel, dilation=dilation, span=span, slopes=slopes),
        grid=(b, dilation, nb),
        in_specs=[
            pl.BlockSpec(blk, lambda i, r, n: (i, n, r * cb + qc)),
            pl.BlockSpec(blk, lambda i, r, n: (i, jnp.maximum(n - 1, 0), r * cb + kc)),
            pl.BlockSpec(blk, lambda i, r, n: (i, n, r * cb + kc)),
            pl.BlockSpec(blk, lambda i, r, n: (i, jnp.maximum(n - 1, 0), r * cb + vc)),
            pl.BlockSpec(blk, lambda i, r, n: (i, n, r * cb + vc)),
            _const_spec((1, ATT_WIDTH)),
            _const_spec((1, ATT_WIDTH)),
            _const_spec((ATT_WIDTH, ATT_WIDTH)),
        ],
        out_specs=[
            pl.BlockSpec(blk, lambda i, r, n: (i, n, r)),
            pl.BlockSpec((1, ATT_BLOCK, 128), lambda i, r, n: (i, n, r)),
        ],
        out_shape=[
            jax.ShapeDtypeStruct((b, sub, dilation * ATT_WIDTH), F32),
            jax.ShapeDtypeStruct((b, sub, dilation * 128), F32),
        ],
        compiler_params=_params("parallel", "parallel", "arbitrary"),
        name=f"dilated_attention_g{group}",
    )(view, view, view, view, view,
      jnp.tile(q_gain, ATT_SLOTS).reshape(1, ATT_WIDTH), jnp.tile(k_gain, ATT_SLOTS).reshape(1, ATT_WIDTH),
      jnp.asarray(bd, F32))
    return o.reshape(b * seq, ATT_WIDTH), lse.reshape(b * seq, 128)


DN_ROWS = 256


def _deltanet_kernel(qkv_ref, z_ref, ba_ref, cw_ref, alog_ref, dtb_ref, og_ref, tril_ref, o_ref,
                     state_ref, tail_ref):
    c = DN_CHUNK
    rows = qkv_ref.shape[1]

    @pl.when(pl.program_id(1) == 0)
    def _():
        state_ref[...] = jnp.zeros_like(state_ref)
        tail_ref[...] = jnp.zeros_like(tail_ref)

    x = qkv_ref[0]
    xc = jnp.concatenate([tail_ref[...], x], axis=0)
    cw = cw_ref[...]
    acc = cw[DN_CONV - 1:DN_CONV] * x
    for j in range(DN_CONV - 1):
        acc = acc + cw[j:j + 1] * xc[5 + j:5 + j + rows]
    tail_ref[...] = x[rows - 8:rows]
    a = acc * _sigmoid(acc)

    ba = ba_ref[0]
    beta_all = _sigmoid(ba)
    sp = ba + dtb_ref[...]
    sp = jnp.maximum(sp, 0.0) + jnp.log(1.0 + jnp.exp(-jnp.abs(sp)))
    g_all = -(jnp.exp(alog_ref[...]) * sp)
    tril = tril_ref[...]
    ii = lax.broadcasted_iota(jnp.int32, (c, c), 0)
    jj = lax.broadcasted_iota(jnp.int32, (c, c), 1)
    incl = ii >= jj
    strict = ii > jj
    eye = (ii == jj).astype(F32)
    z = z_ref[0]
    og = og_ref[...]

    n_chunks = rows // c
    units = [(h, ci) for ci in range(n_chunks) for h in range(DN_HEADS)]

    loc = {}
    for h in range(DN_HEADS):
        hs = slice(h * DN_HEAD_DIM, (h + 1) * DN_HEAD_DIM)
        qh = a[:, hs]
        kh = a[:, DN_WIDTH + h * DN_HEAD_DIM:DN_WIDTH + (h + 1) * DN_HEAD_DIM]
        vh = a[:, 2 * DN_WIDTH + h * DN_HEAD_DIM:2 * DN_WIDTH + (h + 1) * DN_HEAD_DIM]
        qh = qh * lax.rsqrt(jnp.sum(qh * qh, axis=-1, keepdims=True) + NORM_EPS) * (DN_HEAD_DIM ** -0.5)
        kh = kh * lax.rsqrt(jnp.sum(kh * kh, axis=-1, keepdims=True) + NORM_EPS)
        beta = beta_all[:, h:h + 1]
        gh = jnp.broadcast_to(g_all[:, DN_HEADS + h:DN_HEADS + h + 1], (rows, DN_HEAD_DIM))
        for ci in range(n_chunks):
            rs = slice(ci * c, (ci + 1) * c)
            gc = _dot(tril, gh[rs], precision=HI)
            gc64 = gc[:, :c]
            diff = gc64 - gc64.T
            decay = jnp.where(incl, jnp.exp(jnp.where(incl, diff, 0.0)), 0.0)
            eg = jnp.exp(gc)
            g_last = gc[c - 1:c, :]
            kc_, qc_, vc_, bc_ = kh[rs], qh[rs], vh[rs], beta[rs]
            kbeta = kc_ * bc_
            kcb = kc_.astype(BF16)
            loc[h, ci] = dict(
                lower=jnp.where(strict, _dot_nt(kbeta.astype(BF16), kcb) * decay, 0.0),
                vb=(vc_ * bc_).astype(BF16), kbe=(kbeta * eg).astype(BF16),
                attn=jnp.where(incl, _dot_nt(qc_.astype(BF16), kcb) * decay, 0.0).astype(BF16),
                q_dec=(qc_ * eg).astype(BF16), k_dec=(kc_ * jnp.exp(g_last - gc)).astype(BF16),
                g_end=jnp.exp(g_last))

    t_inv = {k: eye - loc[k]["lower"] for k in units}
    pw = {k: _dot3(loc[k]["lower"], loc[k]["lower"]) for k in units}
    for it in range(5):
        t_inv = {k: t_inv[k] + _dot3(t_inv[k], pw[k]) for k in units}
        if it < 4:
            pw = {k: _dot3(pw[k], pw[k]) for k in units}
    uw = {}
    for k in units:
        t_b = t_inv[k].astype(BF16)
        uw[k] = (_dot(t_b, loc[k]["vb"]), _dot(t_b, loc[k]["kbe"]).astype(BF16))

    state = [state_ref[h] for h in range(DN_HEADS)]
    o_chunks = [[] for _ in range(DN_HEADS)]
    for ci in range(n_chunks):
        for h in range(DN_HEADS):
            d = loc[h, ci]
            u, w = uw[h, ci]
            sb = state[h].astype(BF16)
            vnb = (u - _dot(w, sb)).astype(BF16)
            o_chunks[h].append(_dot(d["q_dec"], sb) + _dot(d["attn"], vnb))
            state[h] = state[h] * d["g_end"] + _dot_tn(d["k_dec"], vnb)
    outs = []
    for h in range(DN_HEADS):
        state_ref[h] = state[h]
        zh = z[:, h * DN_HEAD_DIM:(h + 1) * DN_HEAD_DIM]
        outs.append(_rms(jnp.concatenate(o_chunks[h], axis=0), og) * (zh * _sigmoid(zh)))
    o_ref[0] = jnp.concatenate(outs, axis=-1)


def _deltanet(proj3, conv_w, a_log, dt_bias, out_gain):
    b, seq, _ = proj3.shape
    rows = min(DN_ROWS, seq)
    pad_a = jnp.zeros((1, 128), F32).at[0, DN_HEADS:2 * DN_HEADS].set(a_log)
    pad_d = jnp.zeros((1, 128), F32).at[0, DN_HEADS:2 * DN_HEADS].set(dt_bias)
    tril = jnp.asarray(np.tril(np.ones((DN_CHUNK, DN_CHUNK), np.float32)))
    out = pl.pallas_call(
        _deltanet_kernel,
        grid=(b, seq // rows),
        in_specs=[
            pl.BlockSpec((1, rows, 3 * DN_WIDTH), lambda i, t: (i, t, OFF_QKV // (3 * DN_WIDTH))),
            pl.BlockSpec((1, rows, DN_WIDTH), lambda i, t: (i, t, OFF_Z // DN_WIDTH)),
            pl.BlockSpec((1, rows, 128), lambda i, t: (i, t, OFF_BA // 128)),
            _const_spec((DN_CONV, 3 * DN_WIDTH)),
            _const_spec((1, 128)),
            _const_spec((1, 128)),
            _const_spec((1, DN_HEAD_DIM)),
            _const_spec((DN_CHUNK, DN_CHUNK)),
        ],
        out_specs=pl.BlockSpec((1, rows, DN_WIDTH), lambda i, t: (i, t, 0)),
        out_shape=jax.ShapeDtypeStruct((b, seq, DN_WIDTH), F32),
        scratch_shapes=[pltpu.VMEM((DN_HEADS, DN_HEAD_DIM, DN_HEAD_DIM), F32), pltpu.VMEM((8, 3 * DN_WIDTH), F32)],
        compiler_params=_params("parallel", "arbitrary"),
        name="gated_deltanet",
    )(proj3, proj3, proj3, conv_w, pad_a, pad_d, out_gain.reshape(1, DN_HEAD_DIM), tril)
    return out.reshape(b * seq, DN_WIDTH)


S5_ROWS = 128
S5_N = S5_GROUPS * S5_STATE


def _s5_kernel(u_ref, bbd_ref, pinv_r_ref, pinv_i_ref, pow_r_ref, pow_i_ref, pow1_r_ref, pow1_i_ref,
               cr_ref, ci_ref, d_ref, tril_ref, o_ref, sr_ref, si_ref):
    @pl.when(pl.program_id(1) == 0)
    def _():
        sr_ref[...] = jnp.zeros_like(sr_ref)
        si_ref[...] = jnp.zeros_like(si_ref)

    u = u_ref[0]
    bu = _dot(u.astype(BF16), bbd_ref[...])
    br, bi = bu[:, :S5_N], bu[:, S5_N:]
    pr, pi = pinv_r_ref[...], pinv_i_ref[...]
    zr = pr * br - pi * bi
    zi = pr * bi + pi * br
    tril = tril_ref[...]
    cr = _dot(tril, zr, precision=HI)
    ci = _dot(tril, zi, precision=HI)
    wr, wi = pow_r_ref[...], pow_i_ref[...]
    p1r, p1i = pow1_r_ref[...], pow1_i_ref[...]
    sr, si = sr_ref[...], si_ref[...]
    xr = wr * cr - wi * ci + (p1r * sr - p1i * si)
    xi = wr * ci + wi * cr + (p1r * si + p1i * sr)
    rows = xr.shape[0]
    sr_ref[...] = xr[rows - 1:rows]
    si_ref[...] = xi[rows - 1:rows]
    y = _dot(xr.astype(BF16), cr_ref[...]) - _dot(xi.astype(BF16), ci_ref[...]) + d_ref[...] * u
    o_ref[0] = _gelu_tanh(y)


def _s5(proj3, a_re, a_im, log_dt, b_re, b_im, c_re, c_im, d_skip):
    b, seq, _ = proj3.shape
    rows = min(S5_ROWS, seq)
    g, n, p = S5_GROUPS, S5_STATE, S5_GROUP
    dt = jnp.exp(log_dt)[:, None]
    lr, li = a_re, a_im
    mag = jnp.exp(lr * dt)
    ab_r, ab_i = mag * jnp.cos(li * dt), mag * jnp.sin(li * dt)
    den = lr * lr + li * li
    nr, ni = ab_r - 1.0, ab_i
    cr, ci = (nr * lr + ni * li) / den, (ni * lr - nr * li) / den
    bb_r = cr[..., None] * b_re - ci[..., None] * b_im
    bb_i = cr[..., None] * b_im + ci[..., None] * b_re
    eye_g = jnp.eye(g, dtype=F32)
    bbd_r = jnp.einsum('gnp,gh->gphn', bb_r, eye_g).reshape(g * p, g * n)
    bbd_i = jnp.einsum('gnp,gh->gphn', bb_i, eye_g).reshape(g * p, g * n)
    bbd = jnp.concatenate([bbd_r, bbd_i], axis=1).astype(BF16)
    cbd_r = jnp.einsum('gpn,gh->gnhp', c_re, eye_g).reshape(g * n, g * p).astype(BF16)
    cbd_i = jnp.einsum('gpn,gh->gnhp', c_im, eye_g).reshape(g * n, g * p).astype(BF16)
    steps = jnp.arange(rows, dtype=F32)[:, None]
    lrd = (lr * dt).reshape(1, g * n)
    lid = (li * dt).reshape(1, g * n)

    def power(k):
        m = jnp.exp(lrd * k)
        return m * jnp.cos(lid * k), m * jnp.sin(lid * k)

    pinv_r, pinv_i = power(-steps)
    pow_r, pow_i = power(steps)
    pow1_r, pow1_i = power(steps + 1.0)
    tril = jnp.asarray(np.tril(np.ones((rows, rows), np.float32)))
    tab = _const_spec((rows, g * n))
    out = pl.pallas_call(
        _s5_kernel,
        grid=(b, seq // rows),
        in_specs=[
            pl.BlockSpec((1, rows, S5_WIDTH), lambda i, t: (i, t, OFF_U // S5_WIDTH)),
            _const_spec((S5_WIDTH, 2 * g * n)),
            tab, tab, tab, tab, tab, tab,
            _const_spec((g * n, S5_WIDTH)),
            _const_spec((g * n, S5_WIDTH)),
            _const_spec((1, S5_WIDTH)),
            _const_spec((rows, rows)),
        ],
        out_specs=pl.BlockSpec((1, rows, S5_WIDTH), lambda i, t: (i, t, 0)),
        out_shape=jax.ShapeDtypeStruct((b, seq, S5_WIDTH), F32),
        scratch_shapes=[pltpu.VMEM((1, g * n), F32), pltpu.VMEM((1, g * n), F32)],
        compiler_params=_params("parallel", "arbitrary"),
        name="s5_mixer",
    )(proj3, bbd, pinv_r, pinv_i, pow_r, pow_i, pow1_r, pow1_i, cbd_r, cbd_i, d_skip.reshape(1, S5_WIDTH), tril)
    return out.reshape(b * seq, S5_WIDTH)


def _combine_kernel(x_ref, ga_ref, gb_ref, gc_ref, o1_ref, o2_ref, o3_ref, l1_ref, l2_ref, l3_ref,
                    ob_ref, s_ref, wa_ref, wb_ref, wcv_ref, wcg_ref, wo_ref, ex_ref, nf_ref,
                    x1_ref, hf_ref):
    l1, l2, l3 = l1_ref[...], l2_ref[...], l3_ref[...]
    m = jnp.maximum(jnp.maximum(l1, l2), l3)
    e1, e2, e3 = jnp.exp(l1 - m), jnp.exp(l2 - m), jnp.exp(l3 - m)
    inv = 1.0 / (e1 + e2 + e3)
    ex = ex_ref[...]
    mix = (_dot(e1 * inv, ex, precision=HI) * o1_ref[...]
           + _dot(e2 * inv, ex, precision=HI) * o2_ref[...]
           + _dot(e3 * inv, ex, precision=HI) * o3_ref[...])
    y_a = _dot(mix.astype(BF16), wa_ref[...])
    y_b = _dot(ob_ref[...].astype(BF16), wb_ref[...])
    sb = s_ref[...].astype(BF16)
    y_c = _dot(sb, wcv_ref[...]) * _sigmoid(_dot(sb, wcg_ref[...]))
    mixed = _sigmoid(ga_ref[...]) * y_a + _sigmoid(gb_ref[...]) * y_b + _sigmoid(gc_ref[...]) * y_c
    x1 = x_ref[...] + _dot(mixed.astype(BF16), wo_ref[...])
    x1_ref[...] = x1
    hf_ref[...] = _rms(x1, nf_ref[...])


def _combine(x2d, proj2, att, ob, s, w_a_out, w_b_out, w_c_val, w_c_gate, w_out, norm_ffn, tm=512):
    t, d = x2d.shape
    tm = min(tm, t)
    (o1, l1), (o2, l2), (o3, l3) = att
    ex = np.zeros((128, ATT_WIDTH), np.float32)
    for h in range(ATT_SLOTS):
        ex[h, h * ATT_HEAD_DIM:(h + 1) * ATT_HEAD_DIM] = 1.0
    row = lambda w: pl.BlockSpec((tm, w), lambda i: (i, 0))
    gate = lambda k: pl.BlockSpec((tm, d), lambda i: (i, OFF_GATES // d + k))
    bf = lambda w: w.astype(BF16)
    return pl.pallas_call(
        _combine_kernel,
        grid=(t // tm,),
        in_specs=[
            row(d), gate(0), gate(1), gate(2),
            row(ATT_WIDTH), row(ATT_WIDTH), row(ATT_WIDTH), row(128), row(128), row(128),
            row(DN_WIDTH), row(S5_WIDTH),
            _const_spec((ATT_WIDTH, d)), _const_spec((DN_WIDTH, d)), _const_spec((S5_WIDTH, d)),
            _const_spec((S5_WIDTH, d)), _const_spec((d, d)), _const_spec((128, ATT_WIDTH)), _const_spec((1, d)),
        ],
        out_specs=[row(d), row(d)],
        out_shape=[jax.ShapeDtypeStruct((t, d), F32), jax.ShapeDtypeStruct((t, d), F32)],
        compiler_params=_params("parallel"),
        name="branch_combine",
    )(x2d, proj2, proj2, proj2, o1, o2, o3, l1, l2, l3, ob, s,
      bf(w_a_out), bf(w_b_out), bf(w_c_val), bf(w_c_gate), bf(w_out), jnp.asarray(ex), norm_ffn.reshape(1, d))


def _ple_kernel(x_ref, f_ref, pe_ref, pn_ref, wg_ref, pw_ref, o_ref):
    x2 = x_ref[...] + f_ref[...]
    gate = _sigmoid(_dot(_rms(x2, pn_ref[...]).astype(BF16), wg_ref[...]))
    o_ref[...] = x2 + gate * _dot(pe_ref[...].astype(BF16), pw_ref[...])


def _ple(x1, ffn, pe2d, ple_norm, ple_w_gate, ple_w, tm=512):
    t, d = x1.shape
    tm = min(tm, t)
    row = lambda w: pl.BlockSpec((tm, w), lambda i: (i, 0))
    return pl.pallas_call(
        _ple_kernel,
        grid=(t // tm,),
        in_specs=[row(d), row(d), row(PLE_DIM), _const_spec((1, d)), _const_spec((d, d)), _const_spec((PLE_DIM, d))],
        out_specs=row(d),
        out_shape=jax.ShapeDtypeStruct((t, d), F32),
        compiler_params=_params("parallel"),
        name="ple_residual",
    )(x1, ffn, pe2d, ple_norm.reshape(1, d), ple_w_gate.astype(BF16), ple_w.astype(BF16))


PEER_PICKS = PEER_HEADS * PEER_TOPK
PEER_EXPERTS = PEER_KEYS * PEER_KEYS
PEER_HALF = D_MODEL // 2
ROW_SUBLANES = PEER_HALF // 128
ROUTE_ROWS = 128
EXPERT_ROWS = 256
UP_BATCH = 32
DOWN_BATCH = 16


STAIR_COUNTS = tuple(PEER_TOPK // (a + 1) for a in range(PEER_TOPK))
STAIR_ROWS = -(-sum(STAIR_COUNTS) // 8) * 8
POS_PAD = 1 << 20


def _top_rows(work, row_id, k, big):
    vals, ids = [], []
    for _ in range(k):
        m = jnp.max(work, axis=0, keepdims=True)
        pos = jnp.min(jnp.where(work == m, row_id, big), axis=0, keepdims=True)
        vals.append(m)
        ids.append(pos)
        work = jnp.where(row_id == pos, -jnp.inf, work)
    return vals, ids


def _peer_route_kernel(hf_ref, wq_ref, k1_ref, k2_ref, pos_ref, idx_ref, gate_ref):
    rows = hf_ref.shape[0]
    qry = _dot(hf_ref[...].astype(BF16), wq_ref[...]).astype(BF16)
    k1, k2 = k1_ref[...], k2_ref[...]
    key_id = lax.broadcasted_iota(jnp.int32, (PEER_KEYS, rows), 0).astype(F32)
    cand_pos = pos_ref[...]
    pad = STAIR_ROWS - sum(STAIR_COUNTS)
    idx_rows, gate_rows = [], []
    for h in range(PEER_HEADS):
        q1 = qry[:, (2 * h) * PEER_KEY_DIM:(2 * h + 1) * PEER_KEY_DIM]
        q2 = qry[:, (2 * h + 1) * PEER_KEY_DIM:(2 * h + 2) * PEER_KEY_DIM]
        v1, i1 = _top_rows(_dot_nt(k1, q1), key_id, PEER_TOPK, float(PEER_KEYS))
        v2, i2 = _top_rows(_dot_nt(k2, q2), key_id, PEER_TOPK, float(PEER_KEYS))
        v2c = jnp.concatenate(v2, axis=0)
        i2c = jnp.concatenate(i2, axis=0)
        cand_s = jnp.concatenate([v1[a] + v2c[0:n] for a, n in enumerate(STAIR_COUNTS)]
                                 + [jnp.full((pad, rows), -jnp.inf, F32)], axis=0)
        cand_i = jnp.concatenate([i1[a] * PEER_KEYS + i2c[0:n] for a, n in enumerate(STAIR_COUNTS)]
                                 + [jnp.zeros((pad, rows), F32)], axis=0)
        top_s, pos = _top_rows(cand_s, cand_pos, PEER_TOPK, float(POS_PAD))
        e = [jnp.exp(s - top_s[0]) for s in top_s]
        den = e[0]
        for x in e[1:]:
            den = den + x
        inv = 1.0 / den
        for s, p in zip(e, pos):
            gate_rows.append(s * inv)
            idx_rows.append(jnp.sum(jnp.where(cand_pos == p, cand_i, 0.0), axis=0, keepdims=True))
    idx_ref[...] = (jnp.concatenate(idx_rows, axis=0).T * ROW_SUBLANES).astype(jnp.int32)
    gate_ref[...] = jnp.concatenate(gate_rows, axis=0).T


def _peer_route(hf, w_query, keys1, keys2):
    t, d = hf.shape
    rows = min(ROUTE_ROWS, t)
    nq = w_query.shape[1]
    flat_pos = [a * PEER_TOPK + b for a, n in enumerate(STAIR_COUNTS) for b in range(n)]
    flat_pos += [POS_PAD] * (STAIR_ROWS - len(flat_pos))
    cand_pos = jnp.asarray(np.tile(np.asarray(flat_pos, np.float32)[:, None], (1, rows)))
    return pl.pallas_call(
        _peer_route_kernel,
        grid=(t // rows,),
        in_specs=[pl.BlockSpec((rows, d), lambda i: (i, 0)), _const_spec((d, nq)),
                  _const_spec((PEER_KEYS, PEER_KEY_DIM)), _const_spec((PEER_KEYS, PEER_KEY_DIM)),
                  _const_spec((STAIR_ROWS, rows))],
        out_specs=[pl.BlockSpec((rows, PEER_PICKS), lambda i: (i, 0)), pl.BlockSpec((rows, PEER_PICKS), lambda i: (i, 0))],
        out_shape=[jax.ShapeDtypeStruct((t, PEER_PICKS), jnp.int32), jax.ShapeDtypeStruct((t, PEER_PICKS), F32)],
        compiler_params=_params("parallel"),
        name="peer_route",
    )(hf, w_query.astype(BF16), keys1.astype(BF16), keys2.astype(BF16), cand_pos)


def _pack_table(table):
    bits = lax.bitcast_convert_type(table.astype(BF16), jnp.uint16).astype(jnp.uint32)
    words = bits[:, :PEER_HALF] | (bits[:, PEER_HALF:] << 16)
    return words.reshape(table.shape[0] * ROW_SUBLANES, 128)


def _expert_row(tab_ref, row):
    return tab_ref[pl.ds(pl.multiple_of(row, ROW_SUBLANES), ROW_SUBLANES), :]


def _unpack_row(words):
    lo = lax.bitcast_convert_type(words << 16, F32)
    hi = lax.bitcast_convert_type(words & jnp.uint32(0xFFFF0000), F32)
    return lo, hi


def _load_table_once(tab_hbm, tab_vmem, sem):
    @pl.when(pl.program_id(0) == 0)
    def _():
        cp = pltpu.make_async_copy(tab_hbm, tab_vmem, sem)
        cp.start()
        cp.wait()


def _to_smem(src_vmem, dst_smem, sem):
    cp = pltpu.make_async_copy(src_vmem, dst_smem, sem)
    cp.start()
    cp.wait()


def _peer_up_kernel(idx_ref, x_ref, gate_ref, sel_ref, tab_hbm, o_ref, tab_ref, idx_smem, part_ref, sems):
    rows = gate_ref.shape[0]
    _load_table_once(tab_hbm, tab_ref, sems.at[0])
    _to_smem(idx_ref, idx_smem, sems.at[1])
    half = PEER_HALF // 128
    batch = part_ref.shape[0] // 8
    r_id = lax.broadcasted_iota(jnp.int32, (8, PEER_PICKS), 0)
    c_id = lax.broadcasted_iota(jnp.int32, (8, PEER_PICKS), 1)
    row_pick = 4 * (r_id % 2) + 2 * ((r_id // 2) % 2) + (r_id // 4)
    diag = (c_id % 8 == row_pick)[None]
    sub_id = lax.broadcasted_iota(jnp.int32, (8, 128), 0)
    low2 = (sub_id % 4) < 2
    even = (sub_id % 2) == 0
    groups_per_step = 8
    steps = PEER_PICKS // 8 // groups_per_step

    def fold_rows(p0, p1, p2, p3):
        def half_fold(a, b, mask, s):
            x = jnp.where(mask, a, pltpu.roll(b, s, 0))
            y = jnp.where(mask, pltpu.roll(a, 8 - s, 0), b)
            return x + y
        return half_fold(half_fold(p0, p1, low2, 2), half_fold(p2, p3, low2, 2), even, 1)

    def token_batch(b, carry):
        base = pl.multiple_of(b * batch, batch)

        def group_step(it, c):
            tb = it // steps
            js = it % steps
            t = base + tb
            x_lo = x_ref[t, 0:half, :]
            x_hi = x_ref[t, half:2 * half, :]
            x_lo2 = jnp.concatenate([x_lo, x_lo], axis=0)
            x_hi2 = jnp.concatenate([x_hi, x_hi], axis=0)
            ids = idx_smem.at[pl.ds(t * PEER_PICKS + js * (8 * groups_per_step), 8 * groups_per_step)]
            col0 = pl.multiple_of(js * (128 * groups_per_step), 128 * groups_per_step)
            dst = part_ref.at[pl.ds(pl.multiple_of(tb * 8, 8), 8), pl.ds(col0, 128 * groups_per_step)]
            for g in range(groups_per_step):
                tiles = []
                for i in range(4):
                    q = g * 8 + 2 * i
                    words = jnp.concatenate([_expert_row(tab_ref, ids[q]), _expert_row(tab_ref, ids[q + 1])], axis=0)
                    lo, hi = _unpack_row(words)
                    tiles.append(lo * x_lo2 + hi * x_hi2)
                dst[:, g * 128:(g + 1) * 128] = fold_rows(*tiles)
            return c

        lax.fori_loop(0, batch * steps, group_step, 0)
        part = part_ref[...]
        p_hi = part.astype(BF16)
        p_lo = (part - p_hi.astype(F32)).astype(BF16)
        sel = sel_ref[...]
        sums = (_dot(p_hi, sel) + _dot(p_lo, sel)).reshape(batch, 8, PEER_PICKS)
        act = jnp.sum(jnp.where(diag, sums, 0.0), axis=1)
        o_ref[pl.ds(base, batch), :] = gate_ref[pl.ds(base, batch), :] * _gelu_tanh(act)
        return carry

    lax.fori_loop(0, rows // batch, token_batch, 0)


def _peer_down_kernel(idx_ref, w_ref, ones_ref, tab_hbm, o_ref, tab_ref, idx_smem, splat_ref, sems):
    rows = w_ref.shape[0]
    _load_table_once(tab_hbm, tab_ref, sems.at[0])
    _to_smem(idx_ref, idx_smem, sems.at[1])
    half = PEER_HALF // 128
    batch = splat_ref.shape[0] // PEER_PICKS
    chunk = 64
    eye =(lax.broadcasted_iota(jnp.int32, (PEER_PICKS, PEER_PICKS), 0)
           == lax.broadcasted_iota(jnp.int32, (PEER_PICKS, PEER_PICKS), 1)).astype(F32)
    ones = ones_ref[...]

    def token_batch(b, carry):
        base = pl.multiple_of(b * batch, batch)
        for i in range(batch):
            d = eye * w_ref[pl.ds(base + i, 1), :]
            d_hi = d.astype(BF16)
            d_lo = (d - d_hi.astype(F32)).astype(BF16)
            splat_ref[i * PEER_PICKS:(i + 1) * PEER_PICKS, :] = _dot(d_hi, ones) + _dot(d_lo, ones)

        def token(tb, c):
            t = base + tb

            def pick_chunk(k, acc):
                a_lo0, a_hi0, a_lo1, a_hi1 = acc
                ids = idx_smem.at[pl.ds(t * PEER_PICKS + k * chunk, chunk)]
                sp = splat_ref.at[pl.ds(pl.multiple_of(tb * PEER_PICKS + k * chunk, chunk), chunk), :]
                for q in range(chunk):
                    lo, hi = _unpack_row(_expert_row(tab_ref, ids[q]))
                    w = jnp.broadcast_to(sp[q:q + 1, :], (half, 128))
                    if q % 2 == 0:
                        a_lo0, a_hi0 = a_lo0 + w * lo, a_hi0 + w * hi
                    else:
                        a_lo1, a_hi1 = a_lo1 + w * lo, a_hi1 + w * hi
                return a_lo0, a_hi0, a_lo1, a_hi1

            zero = jnp.zeros((half, 128), F32)
            a_lo0, a_hi0, a_lo1, a_hi1 = lax.fori_loop(0, PEER_PICKS // chunk, pick_chunk, (zero, zero, zero, zero))
            o_ref[t, 0:half, :] = a_lo0 + a_lo1
            o_ref[t, half:2 * half, :] = a_hi0 + a_hi1
            return c

        lax.fori_loop(0, batch, token, 0)
        return carry

    lax.fori_loop(0, rows // batch, token_batch, 0)


def _peer_experts(hf, idx, gate, expert_u, expert_v):
    t, d = hf.shape
    rows = min(EXPERT_ROWS, t)
    sub = d // 128
    x3 = hf.reshape(t, sub, 128)
    sel = np.zeros((PEER_PICKS // 8, 128, PEER_PICKS), np.float32)
    for j in range(PEER_PICKS // 8):
        sel[j, :, 8 * j:8 * j + 8] = 1.0
    sel = jnp.asarray(sel.reshape(PEER_PICKS // 8 * 128, PEER_PICKS), BF16)
    tab_shape = (PEER_EXPERTS * ROW_SUBLANES, 128)
    picks = pl.BlockSpec((rows, PEER_PICKS), lambda i: (i, 0))
    flat = pl.BlockSpec((rows * PEER_PICKS,), lambda i: (i,))
    tok3 = pl.BlockSpec((rows, sub, 128), lambda i: (i, 0, 0))
    idx_flat = idx.reshape(t * PEER_PICKS)
    w = pl.pallas_call(
        _peer_up_kernel,
        grid=(t // rows,),
        in_specs=[flat, tok3, picks, _const_spec(sel.shape), pl.BlockSpec(memory_space=pl.ANY)],
        out_specs=picks,
        out_shape=jax.ShapeDtypeStruct((t, PEER_PICKS), F32),
        scratch_shapes=[pltpu.VMEM(tab_shape, jnp.uint32), pltpu.SMEM((rows * PEER_PICKS,), jnp.int32),
                        pltpu.VMEM((UP_BATCH * 8, PEER_PICKS // 8 * 128), F32), pltpu.SemaphoreType.DMA((2,))],
        compiler_params=_params("arbitrary"),
        name="peer_up",
    )(idx_flat, x3, gate, sel, _pack_table(expert_u))
    out = pl.pallas_call(
        _peer_down_kernel,
        grid=(t // rows,),
        in_specs=[flat, picks, _const_spec((PEER_PICKS, 128)), pl.BlockSpec(memory_space=pl.ANY)],
        out_specs=tok3,
        out_shape=jax.ShapeDtypeStruct((t, sub, 128), F32),
        scratch_shapes=[pltpu.VMEM(tab_shape, jnp.uint32), pltpu.SMEM((rows * PEER_PICKS,), jnp.int32),
                        pltpu.VMEM((DOWN_BATCH * PEER_PICKS, 128), F32), pltpu.SemaphoreType.DMA((2,))],
        compiler_params=_params("arbitrary"),
        name="peer_down",
    )(idx_flat, w, jnp.ones((PEER_PICKS, 128), BF16), _pack_table(expert_v))
    return out.reshape(t, d)


def _peer(hf, w_query, keys1, keys2, expert_u, expert_v):
    idx, gate = _peer_route(hf, w_query, keys1, keys2)
    return _peer_experts(hf, idx, gate, expert_u, expert_v)


def _pad_w_in(w_in):
    d = w_in.shape[0]
    outs = [jnp.zeros((d, PROJ_COLS), BF16), jnp.zeros((d, ATT_COLS), BF16)]
    src = 0
    for size, (arr, dst) in zip(SRC_SIZES, SRC_DST):
        outs[arr] = outs[arr].at[:, dst:dst + size].set(w_in[:, src:src + size].astype(BF16))
        src += size
    return outs


def _layer(x2d, pe2d, b, seq, norm_mix, w_in, att_q_gain, att_k_gain, w_a_out, dn_conv, dn_a_log, dn_dt_bias,
           dn_out_gain, w_b_out, s5_a_re, s5_a_im, s5_log_dt, s5_b_re, s5_b_im, s5_c_re, s5_c_im, s5_d,
           w_c_val, w_c_gate, w_out, norm_ffn, peer_w_query, peer_keys1, peer_keys2, peer_u, peer_v,
           ple_w, ple_norm, ple_w_gate):
    w_main, w_att = _pad_w_in(w_in)
    proj2 = _norm_matmul(x2d, norm_mix, w_main, tn=512)
    proj3 = proj2.reshape(b, seq, PROJ_COLS)
    proj_att = _norm_matmul(x2d, norm_mix, w_att, tn=ATT_COLS).reshape(b, seq, ATT_COLS)
    att = [_attention_group(proj_att, att_q_gain, att_k_gain, g) for g in range(len(ATT_GROUPS))]
    ob = _deltanet(proj3, dn_conv, dn_a_log, dn_dt_bias, dn_out_gain)
    s = _s5(proj3, s5_a_re, s5_a_im, s5_log_dt, s5_b_re, s5_b_im, s5_c_re, s5_c_im, s5_d)
    x1, hf = _combine(x2d, proj2, att, ob, s, w_a_out, w_b_out, w_c_val, w_c_gate, w_out, norm_ffn)
    ffn = _peer(hf, peer_w_query, peer_keys1, peer_keys2, peer_u, peer_v)
    return _ple(x1, ffn, pe2d, ple_norm, ple_w_gate, ple_w)


def kernel(x, p, norm_mix, w_in, att_q_gain, att_k_gain, w_a_out, dn_conv, dn_a_log, dn_dt_bias, dn_out_gain, w_b_out, s5_a_re, s5_a_im, s5_log_dt, s5_b_re, s5_b_im, s5_c_re, s5_c_im, s5_d, w_c_val, w_c_gate, w_out, norm_ffn, peer_w_query, peer_keys1, peer_keys2, peer_u, peer_v, ple_w, ple_norm, ple_w_gate):
    b, seq, d = x.shape
    depth = p.shape[0]
    x2d = x.reshape(b * seq, d)
    for i in range(depth):
        x2d = _layer(x2d, p[i].reshape(b * seq, PLE_DIM), b, seq, norm_mix[i], w_in[i], att_q_gain[i], att_k_gain[i],
                     w_a_out[i], dn_conv[i], dn_a_log[i], dn_dt_bias[i], dn_out_gain[i], w_b_out[i], s5_a_re[i],
                     s5_a_im[i], s5_log_dt[i], s5_b_re[i], s5_b_im[i], s5_c_re[i], s5_c_im[i], s5_d[i], w_c_val[i],
                     w_c_gate[i], w_out[i], norm_ffn[i], peer_w_query[i], peer_keys1[i], peer_keys2[i], peer_u[i],
                     peer_v[i], ple_w[i], ple_norm[i], ple_w_gate[i])
    return x2d.reshape(b, seq, d)
```

```python
import functools

import numpy as np
import jax
import jax.numpy as jnp
from jax import lax
from jax.experimental import pallas as pl
from jax.experimental.pallas import tpu as pltpu

F32 = jnp.float32
BF16 = jnp.bfloat16
HI = lax.Precision.HIGHEST

D_MODEL = 1024
NORM_EPS = 1e-6
ATT_GROUPS = ((128, 1), (512, 4), (2048, 16))
ATT_SLOTS = 4
ATT_HEAD_DIM = 64
ATT_WIDTH = 256
ATT_BLOCK = 128
ALIBI_MAX_EXP = 8.0
DN_HEADS = 4
DN_HEAD_DIM = 128
DN_WIDTH = 512
DN_CONV = 4
DN_CHUNK = 64
S5_GROUP = 16
S5_GROUPS = 16
S5_WIDTH = 256
S5_STATE = 64
PEER_HEADS = 8
PEER_KEYS = 128
PEER_TOPK = 16
PEER_KEY_DIM = 128
PEER_BLOCK = 128
PLE_DIM = 256

OFF_QKV, OFF_Z, OFF_GATES, OFF_U, OFF_BA = 0, 1536, 2048, 5120, 5376
PROJ_COLS = 5632
OFF_QA, OFF_KA, OFF_VA = 0, 768, 1024
ATT_COLS = 1280
SRC_SIZES = (768, 256, 256, 1536, 512, 4, 4, 256, 3072)
SRC_DST = ((1, OFF_QA), (1, OFF_KA), (1, OFF_VA), (0, OFF_QKV), (0, OFF_Z), (0, OFF_BA), (0, OFF_BA + 4),
           (0, OFF_U), (0, OFF_GATES))

VMEM_LIMIT = 56 * 1024 * 1024
NEG_BIG = -1e30


def _params(*sem):
    return pltpu.CompilerParams(dimension_semantics=sem, vmem_limit_bytes=VMEM_LIMIT)


def _const_spec(shape):
    nd = len(shape)
    return pl.BlockSpec(shape, lambda *_: (0,) * nd)


def _rms(x, gain):
    return x * lax.rsqrt(jnp.mean(x * x, axis=-1, keepdims=True) + NORM_EPS) * gain


def _sigmoid(x):
    return 1.0 / (1.0 + jnp.exp(-x))


def _gelu_tanh(x):
    return 0.5 * x * (1.0 + jnp.tanh(0.7978845608028654 * (x + 0.044715 * (x * x * x))))


def _dot(a, b, **kw):
    return jnp.dot(a, b, preferred_element_type=F32, **kw)


def _dot3(a, b):
    a_hi, b_hi = a.astype(BF16), b.astype(BF16)
    a_lo = (a - a_hi.astype(F32)).astype(BF16)
    b_lo = (b - b_hi.astype(F32)).astype(BF16)
    return _dot(a_hi, b_hi) + (_dot(a_hi, b_lo) + _dot(a_lo, b_hi))


def _dot_nt(a, b):
    return lax.dot_general(a, b, (((1,), (1,)), ((), ())), preferred_element_type=F32)


def _dot_tn(a, b):
    return lax.dot_general(a, b, (((0,), (0,)), ((), ())), preferred_element_type=F32)


def _norm_matmul_kernel(x_ref, g_ref, w_ref, o_ref, h_ref):
    @pl.when(pl.program_id(1) == 0)
    def _():
        h_ref[...] = _rms(x_ref[...], g_ref[...]).astype(BF16)

    o_ref[...] = _dot(h_ref[...], w_ref[...])


def _norm_matmul(x2d, gain, w_bf16, tm=1024, tn=1024):
    t, d = x2d.shape
    n = w_bf16.shape[1]
    tm = min(tm, t)
    tn = min(tn, n)
    return pl.pallas_call(
        _norm_matmul_kernel,
        grid=(t // tm, n // tn),
        in_specs=[
            pl.BlockSpec((tm, d), lambda i, j: (i, 0)),
            pl.BlockSpec((1, d), lambda i, j: (0, 0)),
            pl.BlockSpec((d, tn), lambda i, j: (0, j)),
        ],
        out_specs=pl.BlockSpec((tm, tn), lambda i, j: (i, j)),
        out_shape=jax.ShapeDtypeStruct((t, n), F32),
        scratch_shapes=[pltpu.VMEM((tm, d), BF16)],
        compiler_params=_params("parallel", "arbitrary"),
        name="norm_matmul",
    )(x2d, gain.reshape(1, d), w_bf16)


def _attn_kernel(q_ref, kp_ref, kc_ref, vp_ref, vc_ref, qg_ref, kg_ref, bd_ref, o_ref, lse_ref, *,
                 dilation, span, slopes):
    n = pl.program_id(2)
    bd = bd_ref[...]

    def head_norm(x, gain):
        x2 = x * x
        hi = x2.astype(BF16)
        lo = (x2 - hi.astype(F32)).astype(BF16)
        ss = _dot(hi, bd) + _dot(lo, bd)
        return x * lax.rsqrt(ss * (1.0 / ATT_HEAD_DIM) + NORM_EPS) * gain

    q = head_norm(q_ref[0], qg_ref[...]) * (ATT_HEAD_DIM ** -0.5)
    k = head_norm(jnp.concatenate([kp_ref[0], kc_ref[0]], axis=0), kg_ref[...])
    v = jnp.concatenate([vp_ref[0], vc_ref[0]], axis=0)
    kb, vb = k.astype(BF16), v.astype(BF16)

    qpos = lax.broadcasted_iota(jnp.int32, (ATT_BLOCK, 2 * ATT_BLOCK), 0)
    kpos = lax.broadcasted_iota(jnp.int32, (ATT_BLOCK, 2 * ATT_BLOCK), 1)
    rel = qpos + ATT_BLOCK - kpos
    valid = (rel >= 0) & (rel <= span) & ((kpos >= ATT_BLOCK) | (n > 0))
    dist = (rel * dilation).astype(F32)
    lane = lax.broadcasted_iota(jnp.int32, (ATT_BLOCK, 128), 1)
    col_head = lax.broadcasted_iota(jnp.int32, (ATT_BLOCK, ATT_WIDTH), 1) // ATT_HEAD_DIM
    heads = range(ATT_SLOTS)

    s = [_dot_nt(jnp.where(col_head == h, q, 0.0).astype(BF16), kb) for h in heads]
    s = [jnp.where(valid, s[h] - slopes[h] * dist, NEG_BIG) for h in heads]
    m = [jnp.max(s[h], axis=-1, keepdims=True) for h in heads]
    e = [jnp.exp(s[h] - m[h]) for h in heads]
    den = [jnp.sum(e[h], axis=-1, keepdims=True) for h in heads]
    pv = [_dot(e[h].astype(BF16), vb) for h in heads]
    out = jnp.zeros((ATT_BLOCK, ATT_WIDTH), F32)
    lse_all = jnp.zeros((ATT_BLOCK, 128), F32)
    for h in heads:
        out = jnp.where(col_head == h, pv[h] / den[h], out)
        lse_all = jnp.where(lane == h, m[h] + jnp.log(den[h]), lse_all)
    o_ref[0] = out
    lse_ref[0] = lse_all


def _attention_group(proj3, q_gain, k_gain, group):
    b, seq, _ = proj3.shape
    window, dilation = ATT_GROUPS[group]
    span = window // dilation
    sub = seq // dilation
    nb = sub // ATT_BLOCK
    view = proj3.reshape(b, sub, dilation * ATT_COLS)
    cb = ATT_COLS // ATT_WIDTH
    qc, kc, vc = OFF_QA // ATT_WIDTH + group, OFF_KA // ATT_WIDTH, OFF_VA // ATT_WIDTH
    n_heads = len(ATT_GROUPS) * ATT_SLOTS
    slopes = 2.0 ** (-ALIBI_MAX_EXP * np.arange(1, n_heads + 1) / n_heads)
    slopes = tuple(float(s) for s in slopes.reshape(len(ATT_GROUPS), ATT_SLOTS)[group])
    blk = (1, ATT_BLOCK, ATT_WIDTH)
    bd = (np.arange(ATT_WIDTH)[:, None] // ATT_HEAD_DIM == np.arange(ATT_WIDTH)[None, :] // ATT_HEAD_DIM)
    o, lse = pl.pallas_call(
        functools.partial(_attn_kernel, dilation=dilation, span=span, slopes=slopes),
        grid=(b, dilation, nb),
        in_specs=[
            pl.BlockSpec(blk, lambda i, r, n: (i, n, r * cb + qc)),
            pl.BlockSpec(blk, lambda i, r, n: (i, jnp.maximum(n - 1, 0), r * cb + kc)),
            pl.BlockSpec(blk, lambda i, r, n: (i, n, r * cb + kc)),
            pl.BlockSpec(blk, lambda i, r, n: (i, jnp.maximum(n - 1, 0), r * cb + vc)),
            pl.BlockSpec(blk, lambda i, r, n: (i, n, r * cb + vc)),
            _const_spec((1, ATT_WIDTH)),
            _const_spec((1, ATT_WIDTH)),
            _const_spec((ATT_WIDTH, ATT_WIDTH)),
        ],
        out_specs=[
            pl.BlockSpec(blk, lambda i, r, n: (i, n, r)),
            pl.BlockSpec((1, ATT_BLOCK, 128), lambda i, r, n: (i, n, r)),
        ],
        out_shape=[
            jax.ShapeDtypeStruct((b, sub, dilation * ATT_WIDTH), F32),
            jax.ShapeDtypeStruct((b, sub, dilation * 128), F32),
        ],
        compiler_params=_params("parallel", "parallel", "arbitrary"),
        name=f"dilated_attention_g{group}",
    )(view, view, view, view, view,
      jnp.tile(q_gain, ATT_SLOTS).reshape(1, ATT_WIDTH), jnp.tile(k_gain, ATT_SLOTS).reshape(1, ATT_WIDTH),
      jnp.asarray(bd, BF16))
    return o.reshape(b * seq, ATT_WIDTH), lse.reshape(b * seq, 128)


DN_ROWS = 256


def _deltanet_kernel(qkv_ref, z_ref, ba_ref, cw_ref, alog_ref, dtb_ref, og_ref, tril_ref, o_ref,
                     state_ref, tail_ref):
    c = DN_CHUNK
    rows = qkv_ref.shape[1]

    @pl.when(pl.program_id(1) == 0)
    def _():
        state_ref[...] = jnp.zeros_like(state_ref)
        tail_ref[...] = jnp.zeros_like(tail_ref)

    x = qkv_ref[0]
    xc = jnp.concatenate([tail_ref[...], x], axis=0)
    cw = cw_ref[...]
    acc = cw[DN_CONV - 1:DN_CONV] * x
    for j in range(DN_CONV - 1):
        acc = acc + cw[j:j + 1] * xc[5 + j:5 + j + rows]
    tail_ref[...] = x[rows - 8:rows]
    a = acc * _sigmoid(acc)

    ba = ba_ref[0]
    beta_all = _sigmoid(ba)
    sp = ba + dtb_ref[...]
    sp = jnp.maximum(sp, 0.0) + jnp.log(1.0 + jnp.exp(-jnp.abs(sp)))
    g_all = -(jnp.exp(alog_ref[...]) * sp)
    tril = tril_ref[...]
    ii = lax.broadcasted_iota(jnp.int32, (c, c), 0)
    jj = lax.broadcasted_iota(jnp.int32, (c, c), 1)
    incl = ii >= jj
    strict = ii > jj
    eye = (ii == jj).astype(F32)
    z = z_ref[0]
    og = og_ref[...]

    n_chunks = rows // c
    units = [(h, ci) for ci in range(n_chunks) for h in range(DN_HEADS)]

    loc = {}
    for h in range(DN_HEADS):
        hs = slice(h * DN_HEAD_DIM, (h + 1) * DN_HEAD_DIM)
        qh = a[:, hs]
        kh = a[:, DN_WIDTH + h * DN_HEAD_DIM:DN_WIDTH + (h + 1) * DN_HEAD_DIM]
        vh = a[:, 2 * DN_WIDTH + h * DN_HEAD_DIM:2 * DN_WIDTH + (h + 1) * DN_HEAD_DIM]
        qh = qh * lax.rsqrt(jnp.sum(qh * qh, axis=-1, keepdims=True) + NORM_EPS) * (DN_HEAD_DIM ** -0.5)
        kh = kh * lax.rsqrt(jnp.sum(kh * kh, axis=-1, keepdims=True) + NORM_EPS)
        beta = beta_all[:, h:h + 1]
        gh = jnp.broadcast_to(g_all[:, DN_HEADS + h:DN_HEADS + h + 1], (rows, DN_HEAD_DIM))
        for ci in range(n_chunks):
            rs = slice(ci * c, (ci + 1) * c)
            gc = _dot(tril, gh[rs], precision=HI)
            gc64 = gc[:, :c]
            diff = gc64 - gc64.T
            decay = jnp.where(incl, jnp.exp(jnp.where(incl, diff, 0.0)), 0.0)
            eg = jnp.exp(gc)
            g_last = gc[c - 1:c, :]
            kc_, qc_, vc_, bc_ = kh[rs], qh[rs], vh[rs], beta[rs]
            kbeta = kc_ * bc_
            kcb = kc_.astype(BF16)
            loc[h, ci] = dict(
                lower=jnp.where(strict, _dot_nt(kbeta.astype(BF16), kcb) * decay, 0.0),
                vb=(vc_ * bc_).astype(BF16), kbe=(kbeta * eg).astype(BF16),
                attn=jnp.where(incl, _dot_nt(qc_.astype(BF16), kcb) * decay, 0.0).astype(BF16),
                q_dec=(qc_ * eg).astype(BF16), k_dec=(kc_ * jnp.exp(g_last - gc)).astype(BF16),
                g_end=jnp.exp(g_last))

    t_inv = {k: eye - loc[k]["lower"] for k in units}
    pw = {k: _dot3(loc[k]["lower"], loc[k]["lower"]) for k in units}
    for it in range(5):
        t_inv = {k: t_inv[k] + _dot3(t_inv[k], pw[k]) for k in units}
        if it < 4:
            pw = {k: _dot3(pw[k], pw[k]) for k in units}
    uw = {}
    for k in units:
        t_b = t_inv[k].astype(BF16)
        uw[k] = (_dot(t_b, loc[k]["vb"]), _dot(t_b, loc[k]["kbe"]).astype(BF16))

    state = [state_ref[h] for h in range(DN_HEADS)]
    o_chunks = [[] for _ in range(DN_HEADS)]
    for ci in range(n_chunks):
        for h in range(DN_HEADS):
            d = loc[h, ci]
            u, w = uw[h, ci]
            sb = state[h].astype(BF16)
            vnb = (u - _dot(w, sb)).astype(BF16)
            o_chunks[h].append(_dot(d["q_dec"], sb) + _dot(d["attn"], vnb))
            state[h] = state[h] * d["g_end"] + _dot_tn(d["k_dec"], vnb)
    outs = []
    for h in range(DN_HEADS):
        state_ref[h] = state[h]
        zh = z[:, h * DN_HEAD_DIM:(h + 1) * DN_HEAD_DIM]
        outs.append(_rms(jnp.concatenate(o_chunks[h], axis=0), og) * (zh * _sigmoid(zh)))
    o_ref[0] = jnp.concatenate(outs, axis=-1)


def _deltanet(proj3, conv_w, a_log, dt_bias, out_gain):
    b, seq, _ = proj3.shape
    rows = min(DN_ROWS, seq)
    pad_a = jnp.zeros((1, 128), F32).at[0, DN_HEADS:2 * DN_HEADS].set(a_log)
    pad_d = jnp.zeros((1, 128), F32).at[0, DN_HEADS:2 * DN_HEADS].set(dt_bias)
    tril = jnp.asarray(np.tril(np.ones((DN_CHUNK, DN_CHUNK), np.float32)))
    out = pl.pallas_call(
        _deltanet_kernel,
        grid=(b, seq // rows),
        in_specs=[
            pl.BlockSpec((1, rows, 3 * DN_WIDTH), lambda i, t: (i, t, OFF_QKV // (3 * DN_WIDTH))),
            pl.BlockSpec((1, rows, DN_WIDTH), lambda i, t: (i, t, OFF_Z // DN_WIDTH)),
            pl.BlockSpec((1, rows, 128), lambda i, t: (i, t, OFF_BA // 128)),
            _const_spec((DN_CONV, 3 * DN_WIDTH)),
            _const_spec((1, 128)),
            _const_spec((1, 128)),
            _const_spec((1, DN_HEAD_DIM)),
            _const_spec((DN_CHUNK, DN_CHUNK)),
        ],
        out_specs=pl.BlockSpec((1, rows, DN_WIDTH), lambda i, t: (i, t, 0)),
        out_shape=jax.ShapeDtypeStruct((b, seq, DN_WIDTH), F32),
        scratch_shapes=[pltpu.VMEM((DN_HEADS, DN_HEAD_DIM, DN_HEAD_DIM), F32), pltpu.VMEM((8, 3 * DN_WIDTH), F32)],
        compiler_params=_params("parallel", "arbitrary"),
        name="gated_deltanet",
    )(proj3, proj3, proj3, conv_w, pad_a, pad_d, out_gain.reshape(1, DN_HEAD_DIM), tril)
    return out.reshape(b * seq, DN_WIDTH)


S5_ROWS = 128
S5_N = S5_GROUPS * S5_STATE


def _s5_kernel(u_ref, bbd_ref, pinv_r_ref, pinv_i_ref, pow_r_ref, pow_i_ref, pow1_r_ref, pow1_i_ref,
               cr_ref, ci_ref, d_ref, tril_ref, o_ref, sr_ref, si_ref):
    @pl.when(pl.program_id(1) == 0)
    def _():
        sr_ref[...] = jnp.zeros_like(sr_ref)
        si_ref[...] = jnp.zeros_like(si_ref)

    u = u_ref[0]
    bu = _dot(u.astype(BF16), bbd_ref[...])
    br, bi = bu[:, :S5_N], bu[:, S5_N:]
    pr, pi = pinv_r_ref[...], pinv_i_ref[...]
    zr = pr * br - pi * bi
    zi = pr * bi + pi * br
    tril = tril_ref[...]

    def cumsum_rows(zz):
        hi = zz.astype(BF16)
        lo = (zz - hi.astype(F32)).astype(BF16)
        return _dot(tril, hi) + _dot(tril, lo)

    cr = cumsum_rows(zr)
    ci = cumsum_rows(zi)
    wr, wi = pow_r_ref[...], pow_i_ref[...]
    p1r, p1i = pow1_r_ref[...], pow1_i_ref[...]
    sr, si = sr_ref[...], si_ref[...]
    xr = wr * cr - wi * ci + (p1r * sr - p1i * si)
    xi = wr * ci + wi * cr + (p1r * si + p1i * sr)
    rows = xr.shape[0]
    sr_ref[...] = xr[rows - 1:rows]
    si_ref[...] = xi[rows - 1:rows]
    y = _dot(xr.astype(BF16), cr_ref[...]) - _dot(xi.astype(BF16), ci_ref[...]) + d_ref[...] * u
    o_ref[0] = _gelu_tanh(y)


def _s5(proj3, a_re, a_im, log_dt, b_re, b_im, c_re, c_im, d_skip):
    b, seq, _ = proj3.shape
    rows = min(S5_ROWS, seq)
    g, n, p = S5_GROUPS, S5_STATE, S5_GROUP
    dt = jnp.exp(log_dt)[:, None]
    lr, li = a_re, a_im
    mag = jnp.exp(lr * dt)
    ab_r, ab_i = mag * jnp.cos(li * dt), mag * jnp.sin(li * dt)
    den = lr * lr + li * li
    nr, ni = ab_r - 1.0, ab_i
    cr, ci = (nr * lr + ni * li) / den, (ni * lr - nr * li) / den
    bb_r = cr[..., None] * b_re - ci[..., None] * b_im
    bb_i = cr[..., None] * b_im + ci[..., None] * b_re
    eye_g = jnp.eye(g, dtype=F32)
    bbd_r = jnp.einsum('gnp,gh->gphn', bb_r, eye_g).reshape(g * p, g * n)
    bbd_i = jnp.einsum('gnp,gh->gphn', bb_i, eye_g).reshape(g * p, g * n)
    bbd = jnp.concatenate([bbd_r, bbd_i], axis=1).astype(BF16)
    cbd_r = jnp.einsum('gpn,gh->gnhp', c_re, eye_g).reshape(g * n, g * p).astype(BF16)
    cbd_i = jnp.einsum('gpn,gh->gnhp', c_im, eye_g).reshape(g * n, g * p).astype(BF16)
    steps = jnp.arange(rows, dtype=F32)[:, None]
    lrd = (lr * dt).reshape(1, g * n)
    lid = (li * dt).reshape(1, g * n)

    def power(k):
        m = jnp.exp(lrd * k)
        return m * jnp.cos(lid * k), m * jnp.sin(lid * k)

    pinv_r, pinv_i = power(-steps)
    pow_r, pow_i = power(steps)
    pow1_r, pow1_i = power(steps + 1.0)
    tril = jnp.asarray(np.tril(np.ones((rows, rows), np.float32)), BF16)
    tab = _const_spec((rows, g * n))
    out = pl.pallas_call(
        _s5_kernel,
        grid=(b, seq // rows),
        in_specs=[
            pl.BlockSpec((1, rows, S5_WIDTH), lambda i, t: (i, t, OFF_U // S5_WIDTH)),
            _const_spec((S5_WIDTH, 2 * g * n)),
            tab, tab, tab, tab, tab, tab,
            _const_spec((g * n, S5_WIDTH)),
            _const_spec((g * n, S5_WIDTH)),
            _const_spec((1, S5_WIDTH)),
            _const_spec((rows, rows)),
        ],
        out_specs=pl.BlockSpec((1, rows, S5_WIDTH), lambda i, t: (i, t, 0)),
        out_shape=jax.ShapeDtypeStruct((b, seq, S5_WIDTH), F32),
        scratch_shapes=[pltpu.VMEM((1, g * n), F32), pltpu.VMEM((1, g * n), F32)],
        compiler_params=_params("parallel", "arbitrary"),
        name="s5_mixer",
    )(proj3, bbd, pinv_r, pinv_i, pow_r, pow_i, pow1_r, pow1_i, cbd_r, cbd_i, d_skip.reshape(1, S5_WIDTH), tril)
    return out.reshape(b * seq, S5_WIDTH)


def _combine_kernel(x_ref, ga_ref, gb_ref, gc_ref, o1_ref, o2_ref, o3_ref, l1_ref, l2_ref, l3_ref,
                    ob_ref, s_ref, wa_ref, wb_ref, wcv_ref, wcg_ref, wo_ref, ex_ref, nf_ref,
                    x1_ref, hf_ref):
    l1, l2, l3 = l1_ref[...], l2_ref[...], l3_ref[...]
    m = jnp.maximum(jnp.maximum(l1, l2), l3)
    e1, e2, e3 = jnp.exp(l1 - m), jnp.exp(l2 - m), jnp.exp(l3 - m)
    inv = 1.0 / (e1 + e2 + e3)
    ex = ex_ref[...]

    def expand(wt):
        hi = wt.astype(BF16)
        lo = (wt - hi.astype(F32)).astype(BF16)
        return _dot(hi, ex) + _dot(lo, ex)

    mix = expand(e1 * inv) * o1_ref[...] + expand(e2 * inv) * o2_ref[...] + expand(e3 * inv) * o3_ref[...]
    y_a = _dot(mix.astype(BF16), wa_ref[...])
    y_b = _dot(ob_ref[...].astype(BF16), wb_ref[...])
    sb = s_ref[...].astype(BF16)
    y_c = _dot(sb, wcv_ref[...]) * _sigmoid(_dot(sb, wcg_ref[...]))
    mixed = _sigmoid(ga_ref[...]) * y_a + _sigmoid(gb_ref[...]) * y_b + _sigmoid(gc_ref[...]) * y_c
    x1 = x_ref[...] + _dot(mixed.astype(BF16), wo_ref[...])
    x1_ref[...] = x1
    hf_ref[...] = _rms(x1, nf_ref[...])


def _combine(x2d, proj2, att, ob, s, w_a_out, w_b_out, w_c_val, w_c_gate, w_out, norm_ffn, tm=512):
    t, d = x2d.shape
    tm = min(tm, t)
    (o1, l1), (o2, l2), (o3, l3) = att
    ex = np.zeros((128, ATT_WIDTH), np.float32)
    for h in range(ATT_SLOTS):
        ex[h, h * ATT_HEAD_DIM:(h + 1) * ATT_HEAD_DIM] = 1.0
    row = lambda w: pl.BlockSpec((tm, w), lambda i: (i, 0))
    gate = lambda k: pl.BlockSpec((tm, d), lambda i: (i, OFF_GATES // d + k))
    bf = lambda w: w.astype(BF16)
    return pl.pallas_call(
        _combine_kernel,
        grid=(t // tm,),
        in_specs=[
            row(d), gate(0), gate(1), gate(2),
            row(ATT_WIDTH), row(ATT_WIDTH), row(ATT_WIDTH), row(128), row(128), row(128),
            row(DN_WIDTH), row(S5_WIDTH),
            _const_spec((ATT_WIDTH, d)), _const_spec((DN_WIDTH, d)), _const_spec((S5_WIDTH, d)),
            _const_spec((S5_WIDTH, d)), _const_spec((d, d)), _const_spec((128, ATT_WIDTH)), _const_spec((1, d)),
        ],
        out_specs=[row(d), row(d)],
        out_shape=[jax.ShapeDtypeStruct((t, d), F32), jax.ShapeDtypeStruct((t, d), F32)],
        compiler_params=_params("parallel"),
        name="branch_combine",
    )(x2d, proj2, proj2, proj2, o1, o2, o3, l1, l2, l3, ob, s,
      bf(w_a_out), bf(w_b_out), bf(w_c_val), bf(w_c_gate), bf(w_out), jnp.asarray(ex, BF16), norm_ffn.reshape(1, d))


def _ple_kernel(x_ref, f_ref, pe_ref, pn_ref, wg_ref, pw_ref, o_ref):
    x2 = x_ref[...] + f_ref[...]
    gate = _sigmoid(_dot(_rms(x2, pn_ref[...]).astype(BF16), wg_ref[...]))
    o_ref[...] = x2 + gate * _dot(pe_ref[...].astype(BF16), pw_ref[...])


def _ple(x1, ffn, pe2d, ple_norm, ple_w_gate, ple_w, tm=512):
    t, d = x1.shape
    tm = min(tm, t)
    row = lambda w: pl.BlockSpec((tm, w), lambda i: (i, 0))
    return pl.pallas_call(
        _ple_kernel,
        grid=(t // tm,),
        in_specs=[row(d), row(d), row(PLE_DIM), _const_spec((1, d)), _const_spec((d, d)), _const_spec((PLE_DIM, d))],
        out_specs=row(d),
        out_shape=jax.ShapeDtypeStruct((t, d), F32),
        compiler_params=_params("parallel"),
        name="ple_residual",
    )(x1, ffn, pe2d, ple_norm.reshape(1, d), ple_w_gate.astype(BF16), ple_w.astype(BF16))


PEER_PICKS = PEER_HEADS * PEER_TOPK
PEER_EXPERTS = PEER_KEYS * PEER_KEYS
PEER_HALF = D_MODEL // 2
ROW_SUBLANES = PEER_HALF // 128
ROUTE_ROWS = 128
EXPERT_ROWS = 256
UP_BATCH = 32
DOWN_BATCH = 16


STAIR_COUNTS = tuple(PEER_TOPK // (a + 1) for a in range(PEER_TOPK))
STAIR_ROWS = -(-sum(STAIR_COUNTS) // 8) * 8
POS_PAD = 1 << 20


def _top_rows(work, row_id, k, big):
    vals, ids = [], []
    for _ in range(k):
        m = jnp.max(work, axis=0, keepdims=True)
        pos = jnp.min(jnp.where(work == m, row_id, big), axis=0, keepdims=True)
        vals.append(m)
        ids.append(pos)
        work = jnp.where(row_id == pos, -jnp.inf, work)
    return vals, ids


def _peer_route_kernel(hf_ref, wq_ref, k1_ref, k2_ref, pos_ref, idx_ref, gate_ref):
    rows = hf_ref.shape[0]
    qry = _dot(hf_ref[...].astype(BF16), wq_ref[...]).astype(BF16)
    k1, k2 = k1_ref[...], k2_ref[...]
    key_id = lax.broadcasted_iota(jnp.int32, (PEER_KEYS, rows), 0).astype(F32)
    cand_pos = pos_ref[...]
    pad = STAIR_ROWS - sum(STAIR_COUNTS)
    idx_rows, gate_rows = [], []
    for h in range(PEER_HEADS):
        q1 = qry[:, (2 * h) * PEER_KEY_DIM:(2 * h + 1) * PEER_KEY_DIM]
        q2 = qry[:, (2 * h + 1) * PEER_KEY_DIM:(2 * h + 2) * PEER_KEY_DIM]
        v1, i1 = _top_rows(_dot_nt(k1, q1), key_id, PEER_TOPK, float(PEER_KEYS))
        v2, i2 = _top_rows(_dot_nt(k2, q2), key_id, PEER_TOPK, float(PEER_KEYS))
        v2c = jnp.concatenate(v2, axis=0)
        i2c = jnp.concatenate(i2, axis=0)
        cand_s = jnp.concatenate([v1[a] + v2c[0:n] for a, n in enumerate(STAIR_COUNTS)]
                                 + [jnp.full((pad, rows), -jnp.inf, F32)], axis=0)
        cand_i = jnp.concatenate([i1[a] * PEER_KEYS + i2c[0:n] for a, n in enumerate(STAIR_COUNTS)]
                                 + [jnp.zeros((pad, rows), F32)], axis=0)
        top_s, pos = _top_rows(cand_s, cand_pos, PEER_TOPK, float(POS_PAD))
        e = [jnp.exp(s - top_s[0]) for s in top_s]
        den = e[0]
        for x in e[1:]:
            den = den + x
        inv = 1.0 / den
        for s, p in zip(e, pos):
            gate_rows.append(s * inv)
            idx_rows.append(jnp.sum(jnp.where(cand_pos == p, cand_i, 0.0), axis=0, keepdims=True))
    idx_ref[...] = (jnp.concatenate(idx_rows, axis=0).T * ROW_SUBLANES).astype(jnp.int32)
    gate_ref[...] = jnp.concatenate(gate_rows, axis=0).T


def _peer_route(hf, w_query, keys1, keys2):
    t, d = hf.shape
    rows = min(ROUTE_ROWS, t)
    nq = w_query.shape[1]
    flat_pos = [a * PEER_TOPK + b for a, n in enumerate(STAIR_COUNTS) for b in range(n)]
    flat_pos += [POS_PAD] * (STAIR_ROWS - len(flat_pos))
    cand_pos = jnp.asarray(np.tile(np.asarray(flat_pos, np.float32)[:, None], (1, rows)))
    return pl.pallas_call(
        _peer_route_kernel,
        grid=(t // rows,),
        in_specs=[pl.BlockSpec((rows, d), lambda i: (i, 0)), _const_spec((d, nq)),
                  _const_spec((PEER_KEYS, PEER_KEY_DIM)), _const_spec((PEER_KEYS, PEER_KEY_DIM)),
                  _const_spec((STAIR_ROWS, rows))],
        out_specs=[pl.BlockSpec((rows, PEER_PICKS), lambda i: (i, 0)), pl.BlockSpec((rows, PEER_PICKS), lambda i: (i, 0))],
        out_shape=[jax.ShapeDtypeStruct((t, PEER_PICKS), jnp.int32), jax.ShapeDtypeStruct((t, PEER_PICKS), F32)],
        compiler_params=_params("parallel"),
        name="peer_route",
    )(hf, w_query.astype(BF16), keys1.astype(BF16), keys2.astype(BF16), cand_pos)


def _pack_table(table):
    bits = lax.bitcast_convert_type(table.astype(BF16), jnp.uint16).astype(jnp.uint32)
    words = bits[:, :PEER_HALF] | (bits[:, PEER_HALF:] << 16)
    return words.reshape(table.shape[0] * ROW_SUBLANES, 128)


def _expert_row(tab_ref, row):
    return tab_ref[pl.ds(pl.multiple_of(row, ROW_SUBLANES), ROW_SUBLANES), :]


def _unpack_row(words):
    lo = lax.bitcast_convert_type(words << 16, F32)
    hi = lax.bitcast_convert_type(words & jnp.uint32(0xFFFF0000), F32)
    return lo, hi


def _load_table_once(tab_hbm, tab_vmem, sem):
    @pl.when(pl.program_id(0) == 0)
    def _():
        cp = pltpu.make_async_copy(tab_hbm, tab_vmem, sem)
        cp.start()
        cp.wait()


def _stage_ids(idx_hbm, idx_smem, sems, count):
    step = pl.program_id(0)
    slot = step % 2

    def copy(s, buf):
        return pltpu.make_async_copy(idx_hbm.at[pl.ds(s * count, count)],
                                     idx_smem.at[pl.ds(buf * count, count)], sems.at[1 + buf])

    @pl.when(step == 0)
    def _():
        copy(0, 0).start()

    @pl.when(step + 1 < pl.num_programs(0))
    def _():
        copy(step + 1, 1 - slot).start()

    copy(step, slot).wait()
    return slot * count


def _peer_up_kernel(idx_hbm, x_ref, gate_ref, sel_ref, tab_hbm, o_ref, tab_ref, idx_smem, part_ref, sems):
    rows = gate_ref.shape[0]
    _load_table_once(tab_hbm, tab_ref, sems.at[0])
    id0 = _stage_ids(idx_hbm, idx_smem, sems, rows * PEER_PICKS)
    half = PEER_HALF // 128
    batch = part_ref.shape[0] // 8
    r_id = lax.broadcasted_iota(jnp.int32, (8, PEER_PICKS), 0)
    c_id = lax.broadcasted_iota(jnp.int32, (8, PEER_PICKS), 1)
    row_pick = 4 * (r_id % 2) + 2 * ((r_id // 2) % 2) + (r_id // 4)
    diag = (c_id % 8 == row_pick)[None]
    sub_id = lax.broadcasted_iota(jnp.int32, (8, 128), 0)
    low2 = (sub_id % 4) < 2
    even = (sub_id % 2) == 0
    groups_per_step = 8
    steps = PEER_PICKS // 8 // groups_per_step

    def fold_rows(p0, p1, p2, p3):
        def half_fold(a, b, mask, s):
            x = jnp.where(mask, a, pltpu.roll(b, s, 0))
            y = jnp.where(mask, pltpu.roll(a, 8 - s, 0), b)
            return x + y
        return half_fold(half_fold(p0, p1, low2, 2), half_fold(p2, p3, low2, 2), even, 1)

    def token_batch(b, carry):
        base = pl.multiple_of(b * batch, batch)

        def group_step(it, c):
            tb = it // steps
            js = it % steps
            t = base + tb
            x_lo = x_ref[t, 0:half, :]
            x_hi = x_ref[t, half:2 * half, :]
            x_lo2 = jnp.concatenate([x_lo, x_lo], axis=0)
            x_hi2 = jnp.concatenate([x_hi, x_hi], axis=0)
            ids = idx_smem.at[pl.ds(id0 + t * PEER_PICKS + js * (8 * groups_per_step), 8 * groups_per_step)]
            col0 = pl.multiple_of(js * (128 * groups_per_step), 128 * groups_per_step)
            dst = part_ref.at[pl.ds(pl.multiple_of(tb * 8, 8), 8), pl.ds(col0, 128 * groups_per_step)]
            for g in range(groups_per_step):
                tiles = []
                for i in range(4):
                    q = g * 8 + 2 * i
                    words = jnp.concatenate([_expert_row(tab_ref, ids[q]), _expert_row(tab_ref, ids[q + 1])], axis=0)
                    lo, hi = _unpack_row(words)
                    tiles.append(lo * x_lo2 + hi * x_hi2)
                dst[:, g * 128:(g + 1) * 128] = fold_rows(*tiles)
            return c

        lax.fori_loop(0, batch * steps, group_step, 0)
        part = part_ref[...]
        p_hi = part.astype(BF16)
        p_lo = (part - p_hi.astype(F32)).astype(BF16)
        sel = sel_ref[...]
        sums = (_dot(p_hi, sel) + _dot(p_lo, sel)).reshape(batch, 8, PEER_PICKS)
        act = jnp.sum(jnp.where(diag, sums, 0.0), axis=1)
        o_ref[pl.ds(base, batch), :] = gate_ref[pl.ds(base, batch), :] * _gelu_tanh(act)
        return carry

    lax.fori_loop(0, rows // batch, token_batch, 0)


def _peer_down_kernel(idx_hbm, w_ref, ones_ref, tab_hbm, o_ref, tab_ref, idx_smem, splat_ref, sems):
    rows = w_ref.shape[0]
    _load_table_once(tab_hbm, tab_ref, sems.at[0])
    id0 = _stage_ids(idx_hbm, idx_smem, sems, rows * PEER_PICKS)
    half = PEER_HALF // 128
    batch = splat_ref.shape[0] // PEER_PICKS
    chunk = 64
    eye =(lax.broadcasted_iota(jnp.int32, (PEER_PICKS, PEER_PICKS), 0)
           == lax.broadcasted_iota(jnp.int32, (PEER_PICKS, PEER_PICKS), 1)).astype(F32)
    ones = ones_ref[...]

    def token_batch(b, carry):
        base = pl.multiple_of(b * batch, batch)
        for i in range(batch):
            d = eye * w_ref[pl.ds(base + i, 1), :]
            d_hi = d.astype(BF16)
            d_lo = (d - d_hi.astype(F32)).astype(BF16)
            splat_ref[i * PEER_PICKS:(i + 1) * PEER_PICKS, :] = _dot(d_hi, ones) + _dot(d_lo, ones)

        def token(tb, c):
            t = base + tb

            def pick_chunk(k, acc):
                a_lo0, a_hi0, a_lo1, a_hi1 = acc
                ids = idx_smem.at[pl.ds(id0 + t * PEER_PICKS + k * chunk, chunk)]
                sp = splat_ref.at[pl.ds(pl.multiple_of(tb * PEER_PICKS + k * chunk, chunk), chunk), :]
                for q in range(chunk):
                    lo, hi = _unpack_row(_expert_row(tab_ref, ids[q]))
                    w = jnp.broadcast_to(sp[q:q + 1, :], (half, 128))
                    if q % 2 == 0:
                        a_lo0, a_hi0 = a_lo0 + w * lo, a_hi0 + w * hi
                    else:
                        a_lo1, a_hi1 = a_lo1 + w * lo, a_hi1 + w * hi
                return a_lo0, a_hi0, a_lo1, a_hi1

            zero = jnp.zeros((half, 128), F32)
            a_lo0, a_hi0, a_lo1, a_hi1 = lax.fori_loop(0, PEER_PICKS // chunk, pick_chunk, (zero, zero, zero, zero))
            o_ref[t, 0:half, :] = a_lo0 + a_lo1
            o_ref[t, half:2 * half, :] = a_hi0 + a_hi1
            return c

        lax.fori_loop(0, batch, token, 0)
        return carry

    lax.fori_loop(0, rows // batch, token_batch, 0)


def _peer_experts(hf, idx, gate, expert_u, expert_v):
    t, d = hf.shape
    rows = min(EXPERT_ROWS, t)
    sub = d // 128
    x3 = hf.reshape(t, sub, 128)
    sel = np.zeros((PEER_PICKS // 8, 128, PEER_PICKS), np.float32)
    for j in range(PEER_PICKS // 8):
        sel[j, :, 8 * j:8 * j + 8] = 1.0
    sel = jnp.asarray(sel.reshape(PEER_PICKS // 8 * 128, PEER_PICKS), BF16)
    tab_shape = (PEER_EXPERTS * ROW_SUBLANES, 128)
    picks = pl.BlockSpec((rows, PEER_PICKS), lambda i: (i, 0))
    hbm = pl.BlockSpec(memory_space=pl.ANY)
    tok3 = pl.BlockSpec((rows, sub, 128), lambda i: (i, 0, 0))
    idx_flat = idx.reshape(t * PEER_PICKS)
    ids_smem = pltpu.SMEM((2 * rows * PEER_PICKS,), jnp.int32)
    w = pl.pallas_call(
        _peer_up_kernel,
        grid=(t // rows,),
        in_specs=[hbm, tok3, picks, _const_spec(sel.shape), hbm],
        out_specs=picks,
        out_shape=jax.ShapeDtypeStruct((t, PEER_PICKS), F32),
        scratch_shapes=[pltpu.VMEM(tab_shape, jnp.uint32), ids_smem,
                        pltpu.VMEM((UP_BATCH * 8, PEER_PICKS // 8 * 128), F32), pltpu.SemaphoreType.DMA((3,))],
        compiler_params=_params("arbitrary"),
        name="peer_up",
    )(idx_flat, x3, gate, sel, _pack_table(expert_u))
    out = pl.pallas_call(
        _peer_down_kernel,
        grid=(t // rows,),
        in_specs=[hbm, picks, _const_spec((PEER_PICKS, 128)), hbm],
        out_specs=tok3,
        out_shape=jax.ShapeDtypeStruct((t, sub, 128), F32),
        scratch_shapes=[pltpu.VMEM(tab_shape, jnp.uint32), ids_smem,
                        pltpu.VMEM((DOWN_BATCH * PEER_PICKS, 128), F32), pltpu.SemaphoreType.DMA((3,))],
        compiler_params=_params("arbitrary"),
        name="peer_down",
    )(idx_flat, w, jnp.ones((PEER_PICKS, 128), BF16), _pack_table(expert_v))
    return out.reshape(t, d)


def _peer(hf, w_query, keys1, keys2, expert_u, expert_v):
    idx, gate = _peer_route(hf, w_query, keys1, keys2)
    return _peer_experts(hf, idx, gate, expert_u, expert_v)


def _pad_w_in(w_in):
    d = w_in.shape[0]
    outs = [jnp.zeros((d, PROJ_COLS), BF16), jnp.zeros((d, ATT_COLS), BF16)]
    src = 0
    for size, (arr, dst) in zip(SRC_SIZES, SRC_DST):
        outs[arr] = outs[arr].at[:, dst:dst + size].set(w_in[:, src:src + size].astype(BF16))
        src += size
    return outs


def _layer(x2d, pe2d, b, seq, norm_mix, w_in, att_q_gain, att_k_gain, w_a_out, dn_conv, dn_a_log, dn_dt_bias,
           dn_out_gain, w_b_out, s5_a_re, s5_a_im, s5_log_dt, s5_b_re, s5_b_im, s5_c_re, s5_c_im, s5_d,
           w_c_val, w_c_gate, w_out, norm_ffn, peer_w_query, peer_keys1, peer_keys2, peer_u, peer_v,
           ple_w, ple_norm, ple_w_gate):
    w_main, w_att = _pad_w_in(w_in)
    proj2 = _norm_matmul(x2d, norm_mix, w_main, tn=512)
    proj3 = proj2.reshape(b, seq, PROJ_COLS)
    proj_att = _norm_matmul(x2d, norm_mix, w_att, tn=ATT_COLS).reshape(b, seq, ATT_COLS)
    att = [_attention_group(proj_att, att_q_gain, att_k_gain, g) for g in range(len(ATT_GROUPS))]
    ob = _deltanet(proj3, dn_conv, dn_a_log, dn_dt_bias, dn_out_gain)
    s = _s5(proj3, s5_a_re, s5_a_im, s5_log_dt, s5_b_re, s5_b_im, s5_c_re, s5_c_im, s5_d)
    x1, hf = _combine(x2d, proj2, att, ob, s, w_a_out, w_b_out, w_c_val, w_c_gate, w_out, norm_ffn)
    ffn = _peer(hf, peer_w_query, peer_keys1, peer_keys2, peer_u, peer_v)
    return _ple(x1, ffn, pe2d, ple_norm, ple_w_gate, ple_w)


def kernel(x, p, norm_mix, w_in, att_q_gain, att_k_gain, w_a_out, dn_conv, dn_a_log, dn_dt_bias, dn_out_gain, w_b_out, s5_a_re, s5_a_im, s5_log_dt, s5_b_re, s5_b_im, s5_c_re, s5_c_im, s5_d, w_c_val, w_c_gate, w_out, norm_ffn, peer_w_query, peer_keys1, peer_keys2, peer_u, peer_v, ple_w, ple_norm, ple_w_gate):
    b, seq, d = x.shape
    depth = p.shape[0]
    x2d = x.reshape(b * seq, d)
    for i in range(depth):
        x2d = _layer(x2d, p[i].reshape(b * seq, PLE_DIM), b, seq, norm_mix[i], w_in[i], att_q_gain[i], att_k_gain[i],
                     w_a_out[i], dn_conv[i], dn_a_log[i], dn_dt_bias[i], dn_out_gain[i], w_b_out[i], s5_a_re[i],
                     s5_a_im[i], s5_log_dt[i], s5_b_re[i], s5_b_im[i], s5_c_re[i], s5_c_im[i], s5_d[i], w_c_val[i],
                     w_c_gate[i], w_out[i], norm_ffn[i], peer_w_query[i], peer_keys1[i], peer_keys2[i], peer_u[i],
                     peer_v[i], ple_w[i], ple_norm[i], ple_w_gate[i])
    return x2d.reshape(b, seq, d)
```

```python
import functools

import numpy as np
import jax
import jax.numpy as jnp
from jax import lax
from jax.experimental import pallas as pl
from jax.experimental.pallas import tpu as pltpu

F32 = jnp.float32
BF16 = jnp.bfloat16
HI = lax.Precision.HIGHEST

D_MODEL = 1024
NORM_EPS = 1e-6
ATT_GROUPS = ((128, 1), (512, 4), (2048, 16))
ATT_SLOTS = 4
ATT_HEAD_DIM = 64
ATT_WIDTH = 256
ATT_BLOCK = 128
ALIBI_MAX_EXP = 8.0
DN_HEADS = 4
DN_HEAD_DIM = 128
DN_WIDTH = 512
DN_CONV = 4
DN_CHUNK = 64
S5_GROUP = 16
S5_GROUPS = 16
S5_WIDTH = 256
S5_STATE = 64
PEER_HEADS = 8
PEER_KEYS = 128
PEER_TOPK = 16
PEER_KEY_DIM = 128
PEER_BLOCK = 128
PLE_DIM = 256

OFF_QKV, OFF_Z, OFF_GATES, OFF_U, OFF_BA = 0, 1536, 2048, 5120, 5376
PROJ_COLS = 5632
OFF_QA, OFF_KA, OFF_VA = 0, 768, 1024
ATT_COLS = 1280
SRC_SIZES = (768, 256, 256, 1536, 512, 4, 4, 256, 3072)
SRC_DST = ((1, OFF_QA), (1, OFF_KA), (1, OFF_VA), (0, OFF_QKV), (0, OFF_Z), (0, OFF_BA), (0, OFF_BA + 4),
           (0, OFF_U), (0, OFF_GATES))

VMEM_LIMIT = 56 * 1024 * 1024
NEG_BIG = -1e30


def _params(*sem):
    return pltpu.CompilerParams(dimension_semantics=sem, vmem_limit_bytes=VMEM_LIMIT)


def _const_spec(shape):
    nd = len(shape)
    return pl.BlockSpec(shape, lambda *_: (0,) * nd)


def _rms(x, gain):
    return x * lax.rsqrt(jnp.mean(x * x, axis=-1, keepdims=True) + NORM_EPS) * gain


def _sigmoid(x):
    return 1.0 / (1.0 + jnp.exp(-x))


def _gelu_tanh(x):
    return 0.5 * x * (1.0 + jnp.tanh(0.7978845608028654 * (x + 0.044715 * (x * x * x))))


def _dot(a, b, **kw):
    return jnp.dot(a, b, preferred_element_type=F32, **kw)


def _dot3(a, b):
    a_hi, b_hi = a.astype(BF16), b.astype(BF16)
    a_lo = (a - a_hi.astype(F32)).astype(BF16)
    b_lo = (b - b_hi.astype(F32)).astype(BF16)
    return _dot(a_hi, b_hi) + (_dot(a_hi, b_lo) + _dot(a_lo, b_hi))


def _dot_nt(a, b):
    return lax.dot_general(a, b, (((1,), (1,)), ((), ())), preferred_element_type=F32)


def _dot_tn(a, b):
    return lax.dot_general(a, b, (((0,), (0,)), ((), ())), preferred_element_type=F32)


def _norm_matmul_kernel(x_ref, g_ref, w_ref, o_ref, h_ref):
    @pl.when(pl.program_id(1) == 0)
    def _():
        h_ref[...] = _rms(x_ref[...], g_ref[...]).astype(BF16)

    o_ref[...] = _dot(h_ref[...], w_ref[...])


def _norm_matmul(x2d, gain, w_bf16, tm=1024, tn=1024):
    t, d = x2d.shape
    n = w_bf16.shape[1]
    tm = min(tm, t)
    tn = min(tn, n)
    return pl.pallas_call(
        _norm_matmul_kernel,
        grid=(t // tm, n // tn),
        in_specs=[
            pl.BlockSpec((tm, d), lambda i, j: (i, 0)),
            pl.BlockSpec((1, d), lambda i, j: (0, 0)),
            pl.BlockSpec((d, tn), lambda i, j: (0, j)),
        ],
        out_specs=pl.BlockSpec((tm, tn), lambda i, j: (i, j)),
        out_shape=jax.ShapeDtypeStruct((t, n), F32),
        scratch_shapes=[pltpu.VMEM((tm, d), BF16)],
        compiler_params=_params("parallel", "arbitrary"),
        name="norm_matmul",
    )(x2d, gain.reshape(1, d), w_bf16)


def _attn_kernel(q_ref, kp_ref, kc_ref, vp_ref, vc_ref, qg_ref, kg_ref, bd_ref, o_ref, lse_ref, *,
                 dilation, span, slopes):
    n = pl.program_id(2)
    bd = bd_ref[...]

    def head_norm(x, gain):
        x2 = x * x
        hi = x2.astype(BF16)
        lo = (x2 - hi.astype(F32)).astype(BF16)
        ss = _dot(hi, bd) + _dot(lo, bd)
        return x * lax.rsqrt(ss * (1.0 / ATT_HEAD_DIM) + NORM_EPS) * gain

    q = head_norm(q_ref[0], qg_ref[...]) * (ATT_HEAD_DIM ** -0.5)
    k = head_norm(jnp.concatenate([kp_ref[0], kc_ref[0]], axis=0), kg_ref[...])
    v = jnp.concatenate([vp_ref[0], vc_ref[0]], axis=0)
    kb, vb = k.astype(BF16), v.astype(BF16)

    qpos = lax.broadcasted_iota(jnp.int32, (ATT_BLOCK, 2 * ATT_BLOCK), 0)
    kpos = lax.broadcasted_iota(jnp.int32, (ATT_BLOCK, 2 * ATT_BLOCK), 1)
    rel = qpos + ATT_BLOCK - kpos
    valid = (rel >= 0) & (rel <= span) & ((kpos >= ATT_BLOCK) | (n > 0))
    dist = (rel * dilation).astype(F32)
    lane = lax.broadcasted_iota(jnp.int32, (ATT_BLOCK, 128), 1)
    col_head = lax.broadcasted_iota(jnp.int32, (ATT_BLOCK, ATT_WIDTH), 1) // ATT_HEAD_DIM
    heads = range(ATT_SLOTS)

    s = [_dot_nt(jnp.where(col_head == h, q, 0.0).astype(BF16), kb) for h in heads]
    s = [jnp.where(valid, s[h] - slopes[h] * dist, NEG_BIG) for h in heads]
    m = [jnp.max(s[h], axis=-1, keepdims=True) for h in heads]
    e = [jnp.exp(s[h] - m[h]) for h in heads]
    den = [jnp.sum(e[h], axis=-1, keepdims=True) for h in heads]
    pv = [_dot(e[h].astype(BF16), vb) for h in heads]
    out = jnp.zeros((ATT_BLOCK, ATT_WIDTH), F32)
    lse_all = jnp.zeros((ATT_BLOCK, 128), F32)
    for h in heads:
        out = jnp.where(col_head == h, pv[h] / den[h], out)
        lse_all = jnp.where(lane == h, m[h] + jnp.log(den[h]), lse_all)
    o_ref[0] = out
    lse_ref[0] = lse_all


def _attention_group(proj3, q_gain, k_gain, group):
    b, seq, _ = proj3.shape
    window, dilation = ATT_GROUPS[group]
    span = window // dilation
    sub = seq // dilation
    nb = sub // ATT_BLOCK
    view = proj3.reshape(b, sub, dilation * ATT_COLS)
    cb = ATT_COLS // ATT_WIDTH
    qc, kc, vc = OFF_QA // ATT_WIDTH + group, OFF_KA // ATT_WIDTH, OFF_VA // ATT_WIDTH
    n_heads = len(ATT_GROUPS) * ATT_SLOTS
    slopes = 2.0 ** (-ALIBI_MAX_EXP * np.arange(1, n_heads + 1) / n_heads)
    slopes = tuple(float(s) for s in slopes.reshape(len(ATT_GROUPS), ATT_SLOTS)[group])
    blk = (1, ATT_BLOCK, ATT_WIDTH)
    bd = (np.arange(ATT_WIDTH)[:, None] // ATT_HEAD_DIM == np.arange(ATT_WIDTH)[None, :] // ATT_HEAD_DIM)
    o, lse = pl.pallas_call(
        functools.partial(_attn_kernel, dilation=dilation, span=span, slopes=slopes),
        grid=(b, dilation, nb),
        in_specs=[
            pl.BlockSpec(blk, lambda i, r, n: (i, n, r * cb + qc)),
            pl.BlockSpec(blk, lambda i, r, n: (i, jnp.maximum(n - 1, 0), r * cb + kc)),
            pl.BlockSpec(blk, lambda i, r, n: (i, n, r * cb + kc)),
            pl.BlockSpec(blk, lambda i, r, n: (i, jnp.maximum(n - 1, 0), r * cb + vc)),
            pl.BlockSpec(blk, lambda i, r, n: (i, n, r * cb + vc)),
            _const_spec((1, ATT_WIDTH)),
            _const_spec((1, ATT_WIDTH)),
            _const_spec((ATT_WIDTH, ATT_WIDTH)),
        ],
        out_specs=[
            pl.BlockSpec(blk, lambda i, r, n: (i, n, r)),
            pl.BlockSpec((1, ATT_BLOCK, 128), lambda i, r, n: (i, n, r)),
        ],
        out_shape=[
            jax.ShapeDtypeStruct((b, sub, dilation * ATT_WIDTH), F32),
            jax.ShapeDtypeStruct((b, sub, dilation * 128), F32),
        ],
        compiler_params=_params("parallel", "parallel", "arbitrary"),
        name=f"dilated_attention_g{group}",
    )(view, view, view, view, view,
      jnp.tile(q_gain, ATT_SLOTS).reshape(1, ATT_WIDTH), jnp.tile(k_gain, ATT_SLOTS).reshape(1, ATT_WIDTH),
      jnp.asarray(bd, BF16))
    return o.reshape(b * seq, ATT_WIDTH), lse.reshape(b * seq, 128)


DN_ROWS = 256


def _deltanet_kernel(qkv_ref, z_ref, ba_ref, cw_ref, alog_ref, dtb_ref, og_ref, tril_ref, o_ref,
                     state_ref, tail_ref):
    c = DN_CHUNK
    rows = qkv_ref.shape[1]

    @pl.when(pl.program_id(1) == 0)
    def _():
        state_ref[...] = jnp.zeros_like(state_ref)
        tail_ref[...] = jnp.zeros_like(tail_ref)

    x = qkv_ref[0]
    xc = jnp.concatenate([tail_ref[...], x], axis=0)
    cw = cw_ref[...]
    acc = cw[DN_CONV - 1:DN_CONV] * x
    for j in range(DN_CONV - 1):
        acc = acc + cw[j:j + 1] * xc[5 + j:5 + j + rows]
    tail_ref[...] = x[rows - 8:rows]
    a = acc * _sigmoid(acc)

    ba = ba_ref[0]
    beta_all = _sigmoid(ba)
    sp = ba + dtb_ref[...]
    sp = jnp.maximum(sp, 0.0) + jnp.log(1.0 + jnp.exp(-jnp.abs(sp)))
    g_all = -(jnp.exp(alog_ref[...]) * sp)
    tril = tril_ref[...]
    ii = lax.broadcasted_iota(jnp.int32, (c, c), 0)
    jj = lax.broadcasted_iota(jnp.int32, (c, c), 1)
    incl = ii >= jj
    strict = ii > jj
    eye = (ii == jj).astype(F32)
    z = z_ref[0]
    og = og_ref[...]

    n_chunks = rows // c
    units = [(h, ci) for ci in range(n_chunks) for h in range(DN_HEADS)]

    loc = {}
    for h in range(DN_HEADS):
        hs = slice(h * DN_HEAD_DIM, (h + 1) * DN_HEAD_DIM)
        qh = a[:, hs]
        kh = a[:, DN_WIDTH + h * DN_HEAD_DIM:DN_WIDTH + (h + 1) * DN_HEAD_DIM]
        vh = a[:, 2 * DN_WIDTH + h * DN_HEAD_DIM:2 * DN_WIDTH + (h + 1) * DN_HEAD_DIM]
        qh = qh * lax.rsqrt(jnp.sum(qh * qh, axis=-1, keepdims=True) + NORM_EPS) * (DN_HEAD_DIM ** -0.5)
        kh = kh * lax.rsqrt(jnp.sum(kh * kh, axis=-1, keepdims=True) + NORM_EPS)
        beta = beta_all[:, h:h + 1]
        gh = jnp.broadcast_to(g_all[:, DN_HEADS + h:DN_HEADS + h + 1], (rows, DN_HEAD_DIM))
        for ci in range(n_chunks):
            rs = slice(ci * c, (ci + 1) * c)
            gc = _dot(tril, gh[rs], precision=HI)
            gc64 = gc[:, :c]
            diff = gc64 - gc64.T
            decay = jnp.where(incl, jnp.exp(jnp.where(incl, diff, 0.0)), 0.0)
            eg = jnp.exp(gc)
            g_last = gc[c - 1:c, :]
            kc_, qc_, vc_, bc_ = kh[rs], qh[rs], vh[rs], beta[rs]
            kbeta = kc_ * bc_
            kcb = kc_.astype(BF16)
            loc[h, ci] = dict(
                lower=jnp.where(strict, _dot_nt(kbeta.astype(BF16), kcb) * decay, 0.0),
                vb=(vc_ * bc_).astype(BF16), kbe=(kbeta * eg).astype(BF16),
                attn=jnp.where(incl, _dot_nt(qc_.astype(BF16), kcb) * decay, 0.0).astype(BF16),
                q_dec=(qc_ * eg).astype(BF16), k_dec=(kc_ * jnp.exp(g_last - gc)).astype(BF16),
                g_end=jnp.exp(g_last))

    t_inv = {k: eye - loc[k]["lower"] for k in units}
    pw = {k: _dot3(loc[k]["lower"], loc[k]["lower"]) for k in units}
    for it in range(5):
        t_inv = {k: t_inv[k] + _dot3(t_inv[k], pw[k]) for k in units}
        if it < 4:
            pw = {k: _dot3(pw[k], pw[k]) for k in units}
    uw = {}
    for k in units:
        t_b = t_inv[k].astype(BF16)
        uw[k] = (_dot(t_b, loc[k]["vb"]), _dot(t_b, loc[k]["kbe"]).astype(BF16))

    state = [state_ref[h] for h in range(DN_HEADS)]
    o_chunks = [[] for _ in range(DN_HEADS)]
    for ci in range(n_chunks):
        for h in range(DN_HEADS):
            d = loc[h, ci]
            u, w = uw[h, ci]
            sb = state[h].astype(BF16)
            vnb = (u - _dot(w, sb)).astype(BF16)
            o_chunks[h].append(_dot(d["q_dec"], sb) + _dot(d["attn"], vnb))
            state[h] = state[h] * d["g_end"] + _dot_tn(d["k_dec"], vnb)
    outs = []
    for h in range(DN_HEADS):
        state_ref[h] = state[h]
        zh = z[:, h * DN_HEAD_DIM:(h + 1) * DN_HEAD_DIM]
        outs.append(_rms(jnp.concatenate(o_chunks[h], axis=0), og) * (zh * _sigmoid(zh)))
    o_ref[0] = jnp.concatenate(outs, axis=-1)


def _deltanet(proj3, conv_w, a_log, dt_bias, out_gain):
    b, seq, _ = proj3.shape
    rows = min(DN_ROWS, seq)
    pad_a = jnp.zeros((1, 128), F32).at[0, DN_HEADS:2 * DN_HEADS].set(a_log)
    pad_d = jnp.zeros((1, 128), F32).at[0, DN_HEADS:2 * DN_HEADS].set(dt_bias)
    tril = jnp.asarray(np.tril(np.ones((DN_CHUNK, DN_CHUNK), np.float32)))
    out = pl.pallas_call(
        _deltanet_kernel,
        grid=(b, seq // rows),
        in_specs=[
            pl.BlockSpec((1, rows, 3 * DN_WIDTH), lambda i, t: (i, t, OFF_QKV // (3 * DN_WIDTH))),
            pl.BlockSpec((1, rows, DN_WIDTH), lambda i, t: (i, t, OFF_Z // DN_WIDTH)),
            pl.BlockSpec((1, rows, 128), lambda i, t: (i, t, OFF_BA // 128)),
            _const_spec((DN_CONV, 3 * DN_WIDTH)),
            _const_spec((1, 128)),
            _const_spec((1, 128)),
            _const_spec((1, DN_HEAD_DIM)),
            _const_spec((DN_CHUNK, DN_CHUNK)),
        ],
        out_specs=pl.BlockSpec((1, rows, DN_WIDTH), lambda i, t: (i, t, 0)),
        out_shape=jax.ShapeDtypeStruct((b, seq, DN_WIDTH), F32),
        scratch_shapes=[pltpu.VMEM((DN_HEADS, DN_HEAD_DIM, DN_HEAD_DIM), F32), pltpu.VMEM((8, 3 * DN_WIDTH), F32)],
        compiler_params=_params("parallel", "arbitrary"),
        name="gated_deltanet",
    )(proj3, proj3, proj3, conv_w, pad_a, pad_d, out_gain.reshape(1, DN_HEAD_DIM), tril)
    return out.reshape(b * seq, DN_WIDTH)


S5_ROWS = 128
S5_N = S5_GROUPS * S5_STATE


def _s5_kernel(u_ref, bbd_ref, pinv_r_ref, pinv_i_ref, pow_r_ref, pow_i_ref, pow1_r_ref, pow1_i_ref,
               cr_ref, ci_ref, d_ref, tril_ref, o_ref, sr_ref, si_ref):
    @pl.when(pl.program_id(1) == 0)
    def _():
        sr_ref[...] = jnp.zeros_like(sr_ref)
        si_ref[...] = jnp.zeros_like(si_ref)

    u = u_ref[0]
    bu = _dot(u.astype(BF16), bbd_ref[...])
    br, bi = bu[:, :S5_N], bu[:, S5_N:]
    pr, pi = pinv_r_ref[...], pinv_i_ref[...]
    zr = pr * br - pi * bi
    zi = pr * bi + pi * br
    tril = tril_ref[...]

    def cumsum_rows(zz):
        hi = zz.astype(BF16)
        lo = (zz - hi.astype(F32)).astype(BF16)
        return _dot(tril, hi) + _dot(tril, lo)

    cr = cumsum_rows(zr)
    ci = cumsum_rows(zi)
    wr, wi = pow_r_ref[...], pow_i_ref[...]
    p1r, p1i = pow1_r_ref[...], pow1_i_ref[...]
    sr, si = sr_ref[...], si_ref[...]
    xr = wr * cr - wi * ci + (p1r * sr - p1i * si)
    xi = wr * ci + wi * cr + (p1r * si + p1i * sr)
    rows = xr.shape[0]
    sr_ref[...] = xr[rows - 1:rows]
    si_ref[...] = xi[rows - 1:rows]
    y = _dot(xr.astype(BF16), cr_ref[...]) - _dot(xi.astype(BF16), ci_ref[...]) + d_ref[...] * u
    o_ref[0] = _gelu_tanh(y)


def _s5(proj3, a_re, a_im, log_dt, b_re, b_im, c_re, c_im, d_skip):
    b, seq, _ = proj3.shape
    rows = min(S5_ROWS, seq)
    g, n, p = S5_GROUPS, S5_STATE, S5_GROUP
    dt = jnp.exp(log_dt)[:, None]
    lr, li = a_re, a_im
    mag = jnp.exp(lr * dt)
    ab_r, ab_i = mag * jnp.cos(li * dt), mag * jnp.sin(li * dt)
    den = lr * lr + li * li
    nr, ni = ab_r - 1.0, ab_i
    cr, ci = (nr * lr + ni * li) / den, (ni * lr - nr * li) / den
    bb_r = cr[..., None] * b_re - ci[..., None] * b_im
    bb_i = cr[..., None] * b_im + ci[..., None] * b_re
    eye_g = jnp.eye(g, dtype=F32)
    bbd_r = jnp.einsum('gnp,gh->gphn', bb_r, eye_g).reshape(g * p, g * n)
    bbd_i = jnp.einsum('gnp,gh->gphn', bb_i, eye_g).reshape(g * p, g * n)
    bbd = jnp.concatenate([bbd_r, bbd_i], axis=1).astype(BF16)
    cbd_r = jnp.einsum('gpn,gh->gnhp', c_re, eye_g).reshape(g * n, g * p).astype(BF16)
    cbd_i = jnp.einsum('gpn,gh->gnhp', c_im, eye_g).reshape(g * n, g * p).astype(BF16)
    steps = jnp.arange(rows, dtype=F32)[:, None]
    lrd = (lr * dt).reshape(1, g * n)
    lid = (li * dt).reshape(1, g * n)

    def power(k):
        m = jnp.exp(lrd * k)
        return m * jnp.cos(lid * k), m * jnp.sin(lid * k)

    pinv_r, pinv_i = power(-steps)
    pow_r, pow_i = power(steps)
    pow1_r, pow1_i = power(steps + 1.0)
    tril = jnp.asarray(np.tril(np.ones((rows, rows), np.float32)), BF16)
    tab = _const_spec((rows, g * n))
    out = pl.pallas_call(
        _s5_kernel,
        grid=(b, seq // rows),
        in_specs=[
            pl.BlockSpec((1, rows, S5_WIDTH), lambda i, t: (i, t, OFF_U // S5_WIDTH)),
            _const_spec((S5_WIDTH, 2 * g * n)),
            tab, tab, tab, tab, tab, tab,
            _const_spec((g * n, S5_WIDTH)),
            _const_spec((g * n, S5_WIDTH)),
            _const_spec((1, S5_WIDTH)),
            _const_spec((rows, rows)),
        ],
        out_specs=pl.BlockSpec((1, rows, S5_WIDTH), lambda i, t: (i, t, 0)),
        out_shape=jax.ShapeDtypeStruct((b, seq, S5_WIDTH), F32),
        scratch_shapes=[pltpu.VMEM((1, g * n), F32), pltpu.VMEM((1, g * n), F32)],
        compiler_params=_params("parallel", "arbitrary"),
        name="s5_mixer",
    )(proj3, bbd, pinv_r, pinv_i, pow_r, pow_i, pow1_r, pow1_i, cbd_r, cbd_i, d_skip.reshape(1, S5_WIDTH), tril)
    return out.reshape(b * seq, S5_WIDTH)


def _combine_kernel(x_ref, ga_ref, gb_ref, gc_ref, o1_ref, o2_ref, o3_ref, l1_ref, l2_ref, l3_ref,
                    ob_ref, s_ref, wa_ref, wb_ref, wcv_ref, wcg_ref, wo_ref, ex_ref, nf_ref,
                    x1_ref, hf_ref):
    l1, l2, l3 = l1_ref[...], l2_ref[...], l3_ref[...]
    m = jnp.maximum(jnp.maximum(l1, l2), l3)
    e1, e2, e3 = jnp.exp(l1 - m), jnp.exp(l2 - m), jnp.exp(l3 - m)
    inv = 1.0 / (e1 + e2 + e3)
    ex = ex_ref[...]

    def expand(wt):
        hi = wt.astype(BF16)
        lo = (wt - hi.astype(F32)).astype(BF16)
        return _dot(hi, ex) + _dot(lo, ex)

    mix = expand(e1 * inv) * o1_ref[...] + expand(e2 * inv) * o2_ref[...] + expand(e3 * inv) * o3_ref[...]
    y_a = _dot(mix.astype(BF16), wa_ref[...])
    y_b = _dot(ob_ref[...].astype(BF16), wb_ref[...])
    sb = s_ref[...].astype(BF16)
    y_c = _dot(sb, wcv_ref[...]) * _sigmoid(_dot(sb, wcg_ref[...]))
    mixed = _sigmoid(ga_ref[...]) * y_a + _sigmoid(gb_ref[...]) * y_b + _sigmoid(gc_ref[...]) * y_c
    x1 = x_ref[...] + _dot(mixed.astype(BF16), wo_ref[...])
    x1_ref[...] = x1
    hf_ref[...] = _rms(x1, nf_ref[...])


def _combine(x2d, proj2, att, ob, s, w_a_out, w_b_out, w_c_val, w_c_gate, w_out, norm_ffn, tm=512):
    t, d = x2d.shape
    tm = min(tm, t)
    (o1, l1), (o2, l2), (o3, l3) = att
    ex = np.zeros((128, ATT_WIDTH), np.float32)
    for h in range(ATT_SLOTS):
        ex[h, h * ATT_HEAD_DIM:(h + 1) * ATT_HEAD_DIM] = 1.0
    row = lambda w: pl.BlockSpec((tm, w), lambda i: (i, 0))
    gate = lambda k: pl.BlockSpec((tm, d), lambda i: (i, OFF_GATES // d + k))
    bf = lambda w: w.astype(BF16)
    return pl.pallas_call(
        _combine_kernel,
        grid=(t // tm,),
        in_specs=[
            row(d), gate(0), gate(1), gate(2),
            row(ATT_WIDTH), row(ATT_WIDTH), row(ATT_WIDTH), row(128), row(128), row(128),
            row(DN_WIDTH), row(S5_WIDTH),
            _const_spec((ATT_WIDTH, d)), _const_spec((DN_WIDTH, d)), _const_spec((S5_WIDTH, d)),
            _const_spec((S5_WIDTH, d)), _const_spec((d, d)), _const_spec((128, ATT_WIDTH)), _const_spec((1, d)),
        ],
        out_specs=[row(d), row(d)],
        out_shape=[jax.ShapeDtypeStruct((t, d), F32), jax.ShapeDtypeStruct((t, d), F32)],
        compiler_params=_params("parallel"),
        name="branch_combine",
    )(x2d, proj2, proj2, proj2, o1, o2, o3, l1, l2, l3, ob, s,
      bf(w_a_out), bf(w_b_out), bf(w_c_val), bf(w_c_gate), bf(w_out), jnp.asarray(ex, BF16), norm_ffn.reshape(1, d))


def _ple_kernel(x_ref, f_ref, pe_ref, pn_ref, wg_ref, pw_ref, o_ref):
    x2 = x_ref[...] + f_ref[...]
    gate = _sigmoid(_dot(_rms(x2, pn_ref[...]).astype(BF16), wg_ref[...]))
    o_ref[...] = x2 + gate * _dot(pe_ref[...].astype(BF16), pw_ref[...])


def _ple(x1, ffn, pe2d, ple_norm, ple_w_gate, ple_w, tm=512):
    t, d = x1.shape
    tm = min(tm, t)
    row = lambda w: pl.BlockSpec((tm, w), lambda i: (i, 0))
    return pl.pallas_call(
        _ple_kernel,
        grid=(t // tm,),
        in_specs=[row(d), row(d), row(PLE_DIM), _const_spec((1, d)), _const_spec((d, d)), _const_spec((PLE_DIM, d))],
        out_specs=row(d),
        out_shape=jax.ShapeDtypeStruct((t, d), F32),
        compiler_params=_params("parallel"),
        name="ple_residual",
    )(x1, ffn, pe2d, ple_norm.reshape(1, d), ple_w_gate.astype(BF16), ple_w.astype(BF16))


PEER_PICKS = PEER_HEADS * PEER_TOPK
PEER_EXPERTS = PEER_KEYS * PEER_KEYS
PEER_HALF = D_MODEL // 2
ROW_SUBLANES = PEER_HALF // 128
TABLE_PAD = 8
ROUTE_ROWS = 128
EXPERT_ROWS = 256
UP_BATCH = 32
DOWN_BATCH = 16


STAIR_COUNTS = tuple(PEER_TOPK // (a + 1) for a in range(PEER_TOPK))
STAIR_ROWS = -(-sum(STAIR_COUNTS) // 8) * 8
POS_PAD = 1 << 20


def _top_rows(work, row_id, k, big):
    vals, ids = [], []
    for _ in range(k):
        m = jnp.max(work, axis=0, keepdims=True)
        pos = jnp.min(jnp.where(work == m, row_id, big), axis=0, keepdims=True)
        vals.append(m)
        ids.append(pos)
        work = jnp.where(row_id == pos, -jnp.inf, work)
    return vals, ids


def _peer_route_kernel(hf_ref, wq_ref, k1_ref, k2_ref, pos_ref, idx_ref, gate_ref):
    rows = hf_ref.shape[0]
    qry = _dot(hf_ref[...].astype(BF16), wq_ref[...]).astype(BF16)
    k1, k2 = k1_ref[...], k2_ref[...]
    key_id = lax.broadcasted_iota(jnp.int32, (PEER_KEYS, rows), 0).astype(F32)
    cand_pos = pos_ref[...]
    pad = STAIR_ROWS - sum(STAIR_COUNTS)
    idx_rows, gate_rows = [], []
    for h in range(PEER_HEADS):
        q1 = qry[:, (2 * h) * PEER_KEY_DIM:(2 * h + 1) * PEER_KEY_DIM]
        q2 = qry[:, (2 * h + 1) * PEER_KEY_DIM:(2 * h + 2) * PEER_KEY_DIM]
        v1, i1 = _top_rows(_dot_nt(k1, q1), key_id, PEER_TOPK, float(PEER_KEYS))
        v2, i2 = _top_rows(_dot_nt(k2, q2), key_id, PEER_TOPK, float(PEER_KEYS))
        v2c = jnp.concatenate(v2, axis=0)
        i2c = jnp.concatenate(i2, axis=0)
        cand_s = jnp.concatenate([v1[a] + v2c[0:n] for a, n in enumerate(STAIR_COUNTS)]
                                 + [jnp.full((pad, rows), -jnp.inf, F32)], axis=0)
        cand_i = jnp.concatenate([i1[a] * PEER_KEYS + i2c[0:n] for a, n in enumerate(STAIR_COUNTS)]
                                 + [jnp.zeros((pad, rows), F32)], axis=0)
        top_s, pos = _top_rows(cand_s, cand_pos, PEER_TOPK, float(POS_PAD))
        e = [jnp.exp(s - top_s[0]) for s in top_s]
        den = e[0]
        for x in e[1:]:
            den = den + x
        inv = 1.0 / den
        for s, p in zip(e, pos):
            gate_rows.append(s * inv)
            idx_rows.append(jnp.sum(jnp.where(cand_pos == p, cand_i, 0.0), axis=0, keepdims=True))
    idx_ref[...] = (jnp.concatenate(idx_rows, axis=0).T * ROW_SUBLANES + TABLE_PAD).astype(jnp.int32)
    gate_ref[...] = jnp.concatenate(gate_rows, axis=0).T


def _peer_route(hf, w_query, keys1, keys2):
    t, d = hf.shape
    rows = min(ROUTE_ROWS, t)
    nq = w_query.shape[1]
    flat_pos = [a * PEER_TOPK + b for a, n in enumerate(STAIR_COUNTS) for b in range(n)]
    flat_pos += [POS_PAD] * (STAIR_ROWS - len(flat_pos))
    cand_pos = jnp.asarray(np.tile(np.asarray(flat_pos, np.float32)[:, None], (1, rows)))
    return pl.pallas_call(
        _peer_route_kernel,
        grid=(t // rows,),
        in_specs=[pl.BlockSpec((rows, d), lambda i: (i, 0)), _const_spec((d, nq)),
                  _const_spec((PEER_KEYS, PEER_KEY_DIM)), _const_spec((PEER_KEYS, PEER_KEY_DIM)),
                  _const_spec((STAIR_ROWS, rows))],
        out_specs=[pl.BlockSpec((rows, PEER_PICKS), lambda i: (i, 0)), pl.BlockSpec((rows, PEER_PICKS), lambda i: (i, 0))],
        out_shape=[jax.ShapeDtypeStruct((t, PEER_PICKS), jnp.int32), jax.ShapeDtypeStruct((t, PEER_PICKS), F32)],
        compiler_params=_params("parallel"),
        name="peer_route",
    )(hf, w_query.astype(BF16), keys1.astype(BF16), keys2.astype(BF16), cand_pos)


def _pack_table(table):
    bits = lax.bitcast_convert_type(table.astype(BF16), jnp.uint16).astype(jnp.uint32)
    words = bits[:, :PEER_HALF] | (bits[:, PEER_HALF:] << 16)
    return jnp.pad(words.reshape(table.shape[0] * ROW_SUBLANES, 128), ((TABLE_PAD, TABLE_PAD), (0, 0)))


def _unpack_row(words):
    lo = lax.bitcast_convert_type(words << 16, F32)
    hi = lax.bitcast_convert_type(words & jnp.uint32(0xFFFF0000), F32)
    return lo, hi


def _load_table_once(tab_hbm, tab_vmem, sem):
    @pl.when(pl.program_id(0) == 0)
    def _():
        cp = pltpu.make_async_copy(tab_hbm, tab_vmem, sem)
        cp.start()
        cp.wait()


def _stage_ids(idx_hbm, idx_smem, sems, count):
    step = pl.program_id(0)
    slot = step % 2

    def copy(s, buf):
        return pltpu.make_async_copy(idx_hbm.at[pl.ds(s * count, count)],
                                     idx_smem.at[pl.ds(buf * count, count)], sems.at[1 + buf])

    @pl.when(step == 0)
    def _():
        copy(0, 0).start()

    @pl.when(step + 1 < pl.num_programs(0))
    def _():
        copy(step + 1, 1 - slot).start()

    copy(step, slot).wait()
    return slot * count


def _peer_up_kernel(idx_hbm, x_ref, gate_ref, sel_ref, tab_hbm, o_ref, tab_ref, idx_smem, part_ref,
                    xx_ref, sems):
    rows = gate_ref.shape[0]
    _load_table_once(tab_hbm, tab_ref, sems.at[0])
    id0 = _stage_ids(idx_hbm, idx_smem, sems, rows * PEER_PICKS)
    half = PEER_HALF // 128
    batch = part_ref.shape[1] // 8
    r_id =lax.broadcasted_iota(jnp.int32, (8, PEER_PICKS), 0)
    c_id = lax.broadcasted_iota(jnp.int32, (8, PEER_PICKS), 1)
    row_pick = 4 * (r_id % 2) + 2 * ((r_id // 2) % 2) + (r_id // 4)
    diag = (c_id % 8 == row_pick)[None]
    sub_id = lax.broadcasted_iota(jnp.int32, (8, 128), 0)
    low2 = (sub_id % 4) < 2
    even = (sub_id % 2) == 0
    low = sub_id < ROW_SUBLANES
    group = 8
    unroll = 4

    def fold_rows(p0, p1, p2, p3):
        def half_fold(a, b, mask, s):
            x = jnp.where(mask, a, pltpu.roll(b, s, 0))
            y = jnp.where(mask, pltpu.roll(a, 8 - s, 0), b)
            return x + y
        return half_fold(half_fold(p0, p1, low2, 2), half_fold(p2, p3, low2, 2), even, 1)

    def token_batch(b, carry):
        base = pl.multiple_of(b * batch, batch)

        def token_group(tg, c):
            tg8 = pl.multiple_of(tg * group, group)
            id_refs = [idx_smem.at[pl.ds(pl.multiple_of(id0 + (base + tg8 + i) * PEER_PICKS, PEER_PICKS), PEER_PICKS)]
                       for i in range(group)]
            for i in range(group):
                x_lo = x_ref[base + tg8 + i, 0:half, :]
                x_hi = x_ref[base + tg8 + i, half:2 * half, :]
                xx_ref[i, 0:8, :] = jnp.concatenate([x_lo, x_lo], axis=0)
                xx_ref[i, 8:16, :] = jnp.concatenate([x_hi, x_hi], axis=0)

            def pick_step(k, cc):
                for u in range(unroll):
                    ku = k * unroll + u
                    k8 = pl.multiple_of(ku * 8, 8)
                    o = [k8 + j for j in range(8)]
                    dst = part_ref.at[ku, pl.ds(pl.multiple_of(tg8 * 8, 8 * group), 8 * group), :]
                    for i in range(group):
                        x_lo2 = xx_ref[i, 0:8, :]
                        x_hi2 = xx_ref[i, 8:16, :]
                        tiles = []
                        for j in range(0, 8, 2):
                            ra, rb = id_refs[i][o[j]], id_refs[i][o[j + 1]]
                            words = jnp.where(low, tab_ref[pl.ds(ra, 8), :], tab_ref[pl.ds(rb - ROW_SUBLANES, 8), :])
                            lo, hi = _unpack_row(words)
                            tiles.append(lo * x_lo2 + hi * x_hi2)
                        dst[i * 8:(i + 1) * 8, :] = fold_rows(*tiles)
                return cc

            lax.fori_loop(0, PEER_PICKS // 8 // unroll, pick_step, 0)
            return c

        lax.fori_loop(0, batch // group, token_group, 0)
        sums = jnp.zeros((batch * 8, PEER_PICKS), F32)
        for k in range(PEER_PICKS // 8):
            part = part_ref[k]
            p_hi = part.astype(BF16)
            p_lo = (part - p_hi.astype(F32)).astype(BF16)
            sums = sums + (_dot(p_hi, sel_ref[k]) + _dot(p_lo, sel_ref[k]))
        act = jnp.sum(jnp.where(diag, sums.reshape(batch, 8, PEER_PICKS), 0.0), axis=1)
        o_ref[pl.ds(base, batch), :] = gate_ref[pl.ds(base, batch), :] * _gelu_tanh(act)
        return carry

    lax.fori_loop(0, rows // batch, token_batch, 0)


def _peer_down_kernel(idx_hbm, w_ref, ones_ref, tab_hbm, o_ref, tab_ref, idx_smem, splat_ref, sems):
    rows = w_ref.shape[0]
    _load_table_once(tab_hbm, tab_ref, sems.at[0])
    id0 = _stage_ids(idx_hbm, idx_smem, sems, rows * PEER_PICKS)
    half = PEER_HALF // 128
    batch = splat_ref.shape[0]
    group = 8
    unroll = 2
    low = lax.broadcasted_iota(jnp.int32, (8, 128), 0) < ROW_SUBLANES
    eye =(lax.broadcasted_iota(jnp.int32, (PEER_PICKS, PEER_PICKS), 0)
           == lax.broadcasted_iota(jnp.int32, (PEER_PICKS, PEER_PICKS), 1)).astype(F32)
    ones = ones_ref[...]

    def token_batch(b, carry):
        base = pl.multiple_of(b * batch, batch)
        for i in range(batch):
            d = eye * w_ref[pl.ds(base + i, 1), :]
            d_hi = d.astype(BF16)
            d_lo = (d - d_hi.astype(F32)).astype(BF16)
            splat_ref[i] = _dot(d_hi, ones) + _dot(d_lo, ones)

        def token_group(tg, c):
            tg8 = pl.multiple_of(tg * group, group)
            id_refs = [idx_smem.at[pl.ds(pl.multiple_of(id0 + (base + tg8 + i) * PEER_PICKS, PEER_PICKS), PEER_PICKS)]
                       for i in range(group)]

            def pick_step(k, acc):
                acc = list(acc)
                for u in range(unroll):
                    k8 = pl.multiple_of((k * unroll + u) * 8, 8)
                    o = [k8 + j for j in range(8)]
                    sp = splat_ref.at[pl.ds(tg8, group), pl.ds(k8, 8), :]
                    for i in range(group):
                        for j in range(0, 8, 2):
                            ra, rb = id_refs[i][o[j]], id_refs[i][o[j + 1]]
                            words = jnp.where(low, tab_ref[pl.ds(ra, 8), :], tab_ref[pl.ds(rb - ROW_SUBLANES, 8), :])
                            lo, hi = _unpack_row(words)
                            w = jnp.where(low, jnp.broadcast_to(sp[i, j:j + 1, :], (8, 128)),
                                          jnp.broadcast_to(sp[i, j + 1:j + 2, :], (8, 128)))
                            acc[2 * i] = acc[2 * i] + w * lo
                            acc[2 * i + 1] = acc[2 * i + 1] + w * hi
                return tuple(acc)

            zero = jnp.zeros((8, 128), F32)
            acc = lax.fori_loop(0, PEER_PICKS // 8 // unroll, pick_step, (zero,) * (2 * group))
            for i in range(group):
                o_ref[base + tg8 + i, 0:half, :] = acc[2 * i][0:half] + acc[2 * i][half:2 * half]
                o_ref[base + tg8 + i, half:2 * half, :] = acc[2 * i + 1][0:half] + acc[2 * i + 1][half:2 * half]
            return c

        lax.fori_loop(0, batch // group, token_group, 0)
        return carry

    lax.fori_loop(0, rows // batch, token_batch, 0)


def _peer_experts(hf, idx, gate, expert_u, expert_v):
    t, d = hf.shape
    rows = min(EXPERT_ROWS, t)
    sub = d // 128
    x3 = hf.reshape(t, sub, 128)
    sel = np.zeros((PEER_PICKS // 8, 128, PEER_PICKS), np.float32)
    for j in range(PEER_PICKS // 8):
        sel[j, :, 8 * j:8 * j + 8] = 1.0
    sel = jnp.asarray(sel, BF16)
    tab_shape = (PEER_EXPERTS * ROW_SUBLANES + 2 * TABLE_PAD, 128)
    picks = pl.BlockSpec((rows, PEER_PICKS), lambda i: (i, 0))
    hbm = pl.BlockSpec(memory_space=pl.ANY)
    tok3 = pl.BlockSpec((rows, sub, 128), lambda i: (i, 0, 0))
    idx_flat = idx.reshape(t * PEER_PICKS)
    ids_smem = pltpu.SMEM((2 * rows * PEER_PICKS,), jnp.int32)
    w = pl.pallas_call(
        _peer_up_kernel,
        grid=(t // rows,),
        in_specs=[hbm, tok3, picks, _const_spec(sel.shape), hbm],
        out_specs=picks,
        out_shape=jax.ShapeDtypeStruct((t, PEER_PICKS), F32),
        scratch_shapes=[pltpu.VMEM(tab_shape, jnp.uint32), ids_smem,
                        pltpu.VMEM((PEER_PICKS // 8, UP_BATCH * 8, 128), F32),
                        pltpu.VMEM((8, 16, 128), F32), pltpu.SemaphoreType.DMA((3,))],
        compiler_params=_params("arbitrary"),
        name="peer_up",
    )(idx_flat, x3, gate, sel, _pack_table(expert_u))
    out = pl.pallas_call(
        _peer_down_kernel,
        grid=(t // rows,),
        in_specs=[hbm, picks, _const_spec((PEER_PICKS, 128)), hbm],
        out_specs=tok3,
        out_shape=jax.ShapeDtypeStruct((t, sub, 128), F32),
        scratch_shapes=[pltpu.VMEM(tab_shape, jnp.uint32), ids_smem,
                        pltpu.VMEM((DOWN_BATCH, PEER_PICKS, 128), F32), pltpu.SemaphoreType.DMA((3,))],
        compiler_params=_params("arbitrary"),
        name="peer_down",
    )(idx_flat, w, jnp.ones((PEER_PICKS, 128), BF16), _pack_table(expert_v))
    return out.reshape(t, d)


def _peer(hf, w_query, keys1, keys2, expert_u, expert_v):
    idx, gate = _peer_route(hf, w_query, keys1, keys2)
    return _peer_experts(hf, idx, gate, expert_u, expert_v)


def _pad_w_in(w_in):
    d = w_in.shape[0]
    outs = [jnp.zeros((d, PROJ_COLS), BF16), jnp.zeros((d, ATT_COLS), BF16)]
    src = 0
    for size, (arr, dst) in zip(SRC_SIZES, SRC_DST):
        outs[arr] = outs[arr].at[:, dst:dst + size].set(w_in[:, src:src + size].astype(BF16))
        src += size
    return outs


def _layer(x2d, pe2d, b, seq, norm_mix, w_in, att_q_gain, att_k_gain, w_a_out, dn_conv, dn_a_log, dn_dt_bias,
           dn_out_gain, w_b_out, s5_a_re, s5_a_im, s5_log_dt, s5_b_re, s5_b_im, s5_c_re, s5_c_im, s5_d,
           w_c_val, w_c_gate, w_out, norm_ffn, peer_w_query, peer_keys1, peer_keys2, peer_u, peer_v,
           ple_w, ple_norm, ple_w_gate):
    w_main, w_att = _pad_w_in(w_in)
    proj2 = _norm_matmul(x2d, norm_mix, w_main, tn=512)
    proj3 = proj2.reshape(b, seq, PROJ_COLS)
    proj_att = _norm_matmul(x2d, norm_mix, w_att, tn=ATT_COLS).reshape(b, seq, ATT_COLS)
    att = [_attention_group(proj_att, att_q_gain, att_k_gain, g) for g in range(len(ATT_GROUPS))]
    ob = _deltanet(proj3, dn_conv, dn_a_log, dn_dt_bias, dn_out_gain)
    s = _s5(proj3, s5_a_re, s5_a_im, s5_log_dt, s5_b_re, s5_b_im, s5_c_re, s5_c_im, s5_d)
    x1, hf = _combine(x2d, proj2, att, ob, s, w_a_out, w_b_out, w_c_val, w_c_gate, w_out, norm_ffn)
    ffn = _peer(hf, peer_w_query, peer_keys1, peer_keys2, peer_u, peer_v)
    return _ple(x1, ffn, pe2d, ple_norm, ple_w_gate, ple_w)


def kernel(x, p, norm_mix, w_in, att_q_gain, att_k_gain, w_a_out, dn_conv, dn_a_log, dn_dt_bias, dn_out_gain, w_b_out, s5_a_re, s5_a_im, s5_log_dt, s5_b_re, s5_b_im, s5_c_re, s5_c_im, s5_d, w_c_val, w_c_gate, w_out, norm_ffn, peer_w_query, peer_keys1, peer_keys2, peer_u, peer_v, ple_w, ple_norm, ple_w_gate):
    b, seq, d = x.shape
    depth = p.shape[0]
    x2d = x.reshape(b * seq, d)
    for i in range(depth):
        x2d = _layer(x2d, p[i].reshape(b * seq, PLE_DIM), b, seq, norm_mix[i], w_in[i], att_q_gain[i], att_k_gain[i],
                     w_a_out[i], dn_conv[i], dn_a_log[i], dn_dt_bias[i], dn_out_gain[i], w_b_out[i], s5_a_re[i],
                     s5_a_im[i], s5_log_dt[i], s5_b_re[i], s5_b_im[i], s5_c_re[i], s5_c_im[i], s5_d[i], w_c_val[i],
                     w_c_gate[i], w_out[i], norm_ffn[i], peer_w_query[i], peer_keys1[i], peer_keys2[i], peer_u[i],
                     peer_v[i], ple_w[i], ple_norm[i], ple_w_gate[i])
    return x2d.reshape(b, seq, d)
```

```python
import functools

import numpy as np
import jax
import jax.numpy as jnp
from jax import lax
from jax.experimental import pallas as pl
from jax.experimental.pallas import tpu as pltpu

F32 = jnp.float32
BF16 = jnp.bfloat16
HI = lax.Precision.HIGHEST

D_MODEL = 1024
NORM_EPS = 1e-6
ATT_GROUPS = ((128, 1), (512, 4), (2048, 16))
ATT_SLOTS = 4
ATT_HEAD_DIM = 64
ATT_WIDTH = 256
ATT_BLOCK = 128
ALIBI_MAX_EXP = 8.0
DN_HEADS = 4
DN_HEAD_DIM = 128
DN_WIDTH = 512
DN_CONV = 4
DN_CHUNK = 64
S5_GROUP = 16
S5_GROUPS = 16
S5_WIDTH = 256
S5_STATE = 64
PEER_HEADS = 8
PEER_KEYS = 128
PEER_TOPK = 16
PEER_KEY_DIM = 128
PEER_BLOCK = 128
PLE_DIM = 256

OFF_QKV, OFF_Z, OFF_GATES, OFF_U, OFF_BA = 0, 1536, 2048, 5120, 5376
PROJ_COLS = 5632
OFF_QA, OFF_KA, OFF_VA = 0, 768, 1024
ATT_COLS = 1280
SRC_SIZES = (768, 256, 256, 1536, 512, 4, 4, 256, 3072)
SRC_DST = ((1, OFF_QA), (1, OFF_KA), (1, OFF_VA), (0, OFF_QKV), (0, OFF_Z), (0, OFF_BA), (0, OFF_BA + 4),
           (0, OFF_U), (0, OFF_GATES))

VMEM_LIMIT = 56 * 1024 * 1024
NEG_BIG = -1e30


def _params(*sem):
    return pltpu.CompilerParams(dimension_semantics=sem, vmem_limit_bytes=VMEM_LIMIT)


def _const_spec(shape):
    nd = len(shape)
    return pl.BlockSpec(shape, lambda *_: (0,) * nd)


def _rms(x, gain):
    return x * lax.rsqrt(jnp.mean(x * x, axis=-1, keepdims=True) + NORM_EPS) * gain


def _sigmoid(x):
    return 1.0 / (1.0 + jnp.exp(-x))


def _gelu_tanh(x):
    return 0.5 * x * (1.0 + jnp.tanh(0.7978845608028654 * (x + 0.044715 * (x * x * x))))


def _dot(a, b, **kw):
    return jnp.dot(a, b, preferred_element_type=F32, **kw)


def _dot3(a, b):
    a_hi, b_hi = a.astype(BF16), b.astype(BF16)
    a_lo = (a - a_hi.astype(F32)).astype(BF16)
    b_lo = (b - b_hi.astype(F32)).astype(BF16)
    return _dot(a_hi, b_hi) + (_dot(a_hi, b_lo) + _dot(a_lo, b_hi))


def _dot_nt(a, b):
    return lax.dot_general(a, b, (((1,), (1,)), ((), ())), preferred_element_type=F32)


def _dot_tn(a, b):
    return lax.dot_general(a, b, (((0,), (0,)), ((), ())), preferred_element_type=F32)


def _norm_matmul_kernel(x_ref, g_ref, w_ref, o_ref, h_ref):
    @pl.when(pl.program_id(1) == 0)
    def _():
        h_ref[...] = _rms(x_ref[...], g_ref[...]).astype(BF16)

    o_ref[...] = _dot(h_ref[...], w_ref[...])


def _norm_matmul(x2d, gain, w_bf16, tm=1024, tn=1024):
    t, d = x2d.shape
    n = w_bf16.shape[1]
    tm = min(tm, t)
    tn = min(tn, n)
    return pl.pallas_call(
        _norm_matmul_kernel,
        grid=(t // tm, n // tn),
        in_specs=[
            pl.BlockSpec((tm, d), lambda i, j: (i, 0)),
            pl.BlockSpec((1, d), lambda i, j: (0, 0)),
            pl.BlockSpec((d, tn), lambda i, j: (0, j)),
        ],
        out_specs=pl.BlockSpec((tm, tn), lambda i, j: (i, j)),
        out_shape=jax.ShapeDtypeStruct((t, n), F32),
        scratch_shapes=[pltpu.VMEM((tm, d), BF16)],
        compiler_params=_params("parallel", "arbitrary"),
        name="norm_matmul",
    )(x2d, gain.reshape(1, d), w_bf16)


def _attn_kernel(q_ref, kp_ref, kc_ref, vp_ref, vc_ref, qg_ref, kg_ref, bd_ref, o_ref, lse_ref, *,
                 dilation, span, slopes):
    n = pl.program_id(2)
    bd = bd_ref[...]

    def head_norm(x, gain):
        x2 = x * x
        hi = x2.astype(BF16)
        lo = (x2 - hi.astype(F32)).astype(BF16)
        ss = _dot(hi, bd) + _dot(lo, bd)
        return x * lax.rsqrt(ss * (1.0 / ATT_HEAD_DIM) + NORM_EPS) * gain

    q = head_norm(q_ref[0], qg_ref[...]) * (ATT_HEAD_DIM ** -0.5)
    k = head_norm(jnp.concatenate([kp_ref[0], kc_ref[0]], axis=0), kg_ref[...])
    v = jnp.concatenate([vp_ref[0], vc_ref[0]], axis=0)
    kb, vb = k.astype(BF16), v.astype(BF16)

    qpos = lax.broadcasted_iota(jnp.int32, (ATT_BLOCK, 2 * ATT_BLOCK), 0)
    kpos = lax.broadcasted_iota(jnp.int32, (ATT_BLOCK, 2 * ATT_BLOCK), 1)
    rel = qpos + ATT_BLOCK - kpos
    valid = (rel >= 0) & (rel <= span) & ((kpos >= ATT_BLOCK) | (n > 0))
    dist = (rel * dilation).astype(F32)
    lane = lax.broadcasted_iota(jnp.int32, (ATT_BLOCK, 128), 1)
    col_head = lax.broadcasted_iota(jnp.int32, (ATT_BLOCK, ATT_WIDTH), 1) // ATT_HEAD_DIM
    heads = range(ATT_SLOTS)

    s = [_dot_nt(jnp.where(col_head == h, q, 0.0).astype(BF16), kb) for h in heads]
    s = [jnp.where(valid, s[h] - slopes[h] * dist, NEG_BIG) for h in heads]
    m = [jnp.max(s[h], axis=-1, keepdims=True) for h in heads]
    e = [jnp.exp(s[h] - m[h]) for h in heads]
    den = [jnp.sum(e[h], axis=-1, keepdims=True) for h in heads]
    pv = [_dot(e[h].astype(BF16), vb) for h in heads]
    out = jnp.zeros((ATT_BLOCK, ATT_WIDTH), F32)
    lse_all = jnp.zeros((ATT_BLOCK, 128), F32)
    for h in heads:
        out = jnp.where(col_head == h, pv[h] / den[h], out)
        lse_all = jnp.where(lane == h, m[h] + jnp.log(den[h]), lse_all)
    o_ref[0] = out
    lse_ref[0] = lse_all


def _attention_group(proj3, q_gain, k_gain, group):
    b, seq, _ = proj3.shape
    window, dilation = ATT_GROUPS[group]
    span = window // dilation
    sub = seq // dilation
    nb = sub // ATT_BLOCK
    view = proj3.reshape(b, sub, dilation * ATT_COLS)
    cb = ATT_COLS // ATT_WIDTH
    qc, kc, vc = OFF_QA // ATT_WIDTH + group, OFF_KA // ATT_WIDTH, OFF_VA // ATT_WIDTH
    n_heads = len(ATT_GROUPS) * ATT_SLOTS
    slopes = 2.0 ** (-ALIBI_MAX_EXP * np.arange(1, n_heads + 1) / n_heads)
    slopes = tuple(float(s) for s in slopes.reshape(len(ATT_GROUPS), ATT_SLOTS)[group])
    blk = (1, ATT_BLOCK, ATT_WIDTH)
    bd = (np.arange(ATT_WIDTH)[:, None] // ATT_HEAD_DIM == np.arange(ATT_WIDTH)[None, :] // ATT_HEAD_DIM)
    o, lse = pl.pallas_call(
        functools.partial(_attn_kernel, dilation=dilation, span=span, slopes=slopes),
        grid=(b, dilation, nb),
        in_specs=[
            pl.BlockSpec(blk, lambda i, r, n: (i, n, r * cb + qc)),
            pl.BlockSpec(blk, lambda i, r, n: (i, jnp.maximum(n - 1, 0), r * cb + kc)),
            pl.BlockSpec(blk, lambda i, r, n: (i, n, r * cb + kc)),
            pl.BlockSpec(blk, lambda i, r, n: (i, jnp.maximum(n - 1, 0), r * cb + vc)),
            pl.BlockSpec(blk, lambda i, r, n: (i, n, r * cb + vc)),
            _const_spec((1, ATT_WIDTH)),
            _const_spec((1, ATT_WIDTH)),
            _const_spec((ATT_WIDTH, ATT_WIDTH)),
        ],
        out_specs=[
            pl.BlockSpec(blk, lambda i, r, n: (i, n, r)),
            pl.BlockSpec((1, ATT_BLOCK, 128), lambda i, r, n: (i, n, r)),
        ],
        out_shape=[
            jax.ShapeDtypeStruct((b, sub, dilation * ATT_WIDTH), F32),
            jax.ShapeDtypeStruct((b, sub, dilation * 128), F32),
        ],
        compiler_params=_params("parallel", "parallel", "arbitrary"),
        name=f"dilated_attention_g{group}",
    )(view, view, view, view, view,
      jnp.tile(q_gain, ATT_SLOTS).reshape(1, ATT_WIDTH), jnp.tile(k_gain, ATT_SLOTS).reshape(1, ATT_WIDTH),
      jnp.asarray(bd, BF16))
    return o.reshape(b * seq, ATT_WIDTH), lse.reshape(b * seq, 128)


DN_ROWS = 256


def _deltanet_kernel(qkv_ref, z_ref, ba_ref, cw_ref, alog_ref, dtb_ref, og_ref, tril_ref, o_ref,
                     state_ref, tail_ref):
    c = DN_CHUNK
    rows = qkv_ref.shape[1]

    @pl.when(pl.program_id(1) == 0)
    def _():
        state_ref[...] = jnp.zeros_like(state_ref)
        tail_ref[...] = jnp.zeros_like(tail_ref)

    x = qkv_ref[0]
    xc = jnp.concatenate([tail_ref[...], x], axis=0)
    cw = cw_ref[...]
    acc = cw[DN_CONV - 1:DN_CONV] * x
    for j in range(DN_CONV - 1):
        acc = acc + cw[j:j + 1] * xc[5 + j:5 + j + rows]
    tail_ref[...] = x[rows - 8:rows]
    a = acc * _sigmoid(acc)

    ba = ba_ref[0]
    beta_all = _sigmoid(ba)
    sp = ba + dtb_ref[...]
    sp = jnp.maximum(sp, 0.0) + jnp.log(1.0 + jnp.exp(-jnp.abs(sp)))
    g_all = -(jnp.exp(alog_ref[...]) * sp)
    tril = tril_ref[...]
    ii = lax.broadcasted_iota(jnp.int32, (c, c), 0)
    jj = lax.broadcasted_iota(jnp.int32, (c, c), 1)
    incl = ii >= jj
    strict = ii > jj
    eye = (ii == jj).astype(F32)
    z = z_ref[0]
    og = og_ref[...]

    n_chunks = rows // c
    units = [(h, ci) for ci in range(n_chunks) for h in range(DN_HEADS)]

    loc = {}
    for h in range(DN_HEADS):
        hs = slice(h * DN_HEAD_DIM, (h + 1) * DN_HEAD_DIM)
        qh = a[:, hs]
        kh = a[:, DN_WIDTH + h * DN_HEAD_DIM:DN_WIDTH + (h + 1) * DN_HEAD_DIM]
        vh = a[:, 2 * DN_WIDTH + h * DN_HEAD_DIM:2 * DN_WIDTH + (h + 1) * DN_HEAD_DIM]
        qh = qh * lax.rsqrt(jnp.sum(qh * qh, axis=-1, keepdims=True) + NORM_EPS) * (DN_HEAD_DIM ** -0.5)
        kh = kh * lax.rsqrt(jnp.sum(kh * kh, axis=-1, keepdims=True) + NORM_EPS)
        beta = beta_all[:, h:h + 1]
        gh = jnp.broadcast_to(g_all[:, DN_HEADS + h:DN_HEADS + h + 1], (rows, DN_HEAD_DIM))
        for ci in range(n_chunks):
            rs = slice(ci * c, (ci + 1) * c)
            gc = _dot(tril, gh[rs], precision=HI)
            gc64 = gc[:, :c]
            diff = gc64 - gc64.T
            decay = jnp.where(incl, jnp.exp(jnp.where(incl, diff, 0.0)), 0.0)
            eg = jnp.exp(gc)
            g_last = gc[c - 1:c, :]
            kc_, qc_, vc_, bc_ = kh[rs], qh[rs], vh[rs], beta[rs]
            kbeta = kc_ * bc_
            kcb = kc_.astype(BF16)
            loc[h, ci] = dict(
                lower=jnp.where(strict, _dot_nt(kbeta.astype(BF16), kcb) * decay, 0.0),
                vb=(vc_ * bc_).astype(BF16), kbe=(kbeta * eg).astype(BF16),
                attn=jnp.where(incl, _dot_nt(qc_.astype(BF16), kcb) * decay, 0.0).astype(BF16),
                q_dec=(qc_ * eg).astype(BF16), k_dec=(kc_ * jnp.exp(g_last - gc)).astype(BF16),
                g_end=jnp.exp(g_last))

    t_inv = {k: eye - loc[k]["lower"] for k in units}
    pw = {k: _dot3(loc[k]["lower"], loc[k]["lower"]) for k in units}
    for it in range(5):
        t_inv = {k: t_inv[k] + _dot3(t_inv[k], pw[k]) for k in units}
        if it < 4:
            pw = {k: _dot3(pw[k], pw[k]) for k in units}
    uw = {}
    for k in units:
        t_b = t_inv[k].astype(BF16)
        uw[k] = (_dot(t_b, loc[k]["vb"]), _dot(t_b, loc[k]["kbe"]).astype(BF16))

    state = [state_ref[h] for h in range(DN_HEADS)]
    o_chunks = [[] for _ in range(DN_HEADS)]
    for ci in range(n_chunks):
        for h in range(DN_HEADS):
            d = loc[h, ci]
            u, w = uw[h, ci]
            sb = state[h].astype(BF16)
            vnb = (u - _dot(w, sb)).astype(BF16)
            o_chunks[h].append(_dot(d["q_dec"], sb) + _dot(d["attn"], vnb))
            state[h] = state[h] * d["g_end"] + _dot_tn(d["k_dec"], vnb)
    outs = []
    for h in range(DN_HEADS):
        state_ref[h] = state[h]
        zh = z[:, h * DN_HEAD_DIM:(h + 1) * DN_HEAD_DIM]
        outs.append(_rms(jnp.concatenate(o_chunks[h], axis=0), og) * (zh * _sigmoid(zh)))
    o_ref[0] = jnp.concatenate(outs, axis=-1)


def _deltanet(proj3, conv_w, a_log, dt_bias, out_gain):
    b, seq, _ = proj3.shape
    rows = min(DN_ROWS, seq)
    pad_a = jnp.zeros((1, 128), F32).at[0, DN_HEADS:2 * DN_HEADS].set(a_log)
    pad_d = jnp.zeros((1, 128), F32).at[0, DN_HEADS:2 * DN_HEADS].set(dt_bias)
    tril = jnp.asarray(np.tril(np.ones((DN_CHUNK, DN_CHUNK), np.float32)))
    out = pl.pallas_call(
        _deltanet_kernel,
        grid=(b, seq // rows),
        in_specs=[
            pl.BlockSpec((1, rows, 3 * DN_WIDTH), lambda i, t: (i, t, OFF_QKV // (3 * DN_WIDTH))),
            pl.BlockSpec((1, rows, DN_WIDTH), lambda i, t: (i, t, OFF_Z // DN_WIDTH)),
            pl.BlockSpec((1, rows, 128), lambda i, t: (i, t, OFF_BA // 128)),
            _const_spec((DN_CONV, 3 * DN_WIDTH)),
            _const_spec((1, 128)),
            _const_spec((1, 128)),
            _const_spec((1, DN_HEAD_DIM)),
            _const_spec((DN_CHUNK, DN_CHUNK)),
        ],
        out_specs=pl.BlockSpec((1, rows, DN_WIDTH), lambda i, t: (i, t, 0)),
        out_shape=jax.ShapeDtypeStruct((b, seq, DN_WIDTH), F32),
        scratch_shapes=[pltpu.VMEM((DN_HEADS, DN_HEAD_DIM, DN_HEAD_DIM), F32), pltpu.VMEM((8, 3 * DN_WIDTH), F32)],
        compiler_params=_params("parallel", "arbitrary"),
        name="gated_deltanet",
    )(proj3, proj3, proj3, conv_w, pad_a, pad_d, out_gain.reshape(1, DN_HEAD_DIM), tril)
    return out.reshape(b * seq, DN_WIDTH)


S5_ROWS = 128
S5_N = S5_GROUPS * S5_STATE


def _s5_kernel(u_ref, bbd_ref, pinv_r_ref, pinv_i_ref, pow_r_ref, pow_i_ref, pow1_r_ref, pow1_i_ref,
               cr_ref, ci_ref, d_ref, tril_ref, o_ref, sr_ref, si_ref):
    @pl.when(pl.program_id(1) == 0)
    def _():
        sr_ref[...] = jnp.zeros_like(sr_ref)
        si_ref[...] = jnp.zeros_like(si_ref)

    u = u_ref[0]
    bu = _dot(u.astype(BF16), bbd_ref[...])
    br, bi = bu[:, :S5_N], bu[:, S5_N:]
    pr, pi = pinv_r_ref[...], pinv_i_ref[...]
    zr = pr * br - pi * bi
    zi = pr * bi + pi * br
    tril = tril_ref[...]

    def cumsum_rows(zz):
        hi = zz.astype(BF16)
        lo = (zz - hi.astype(F32)).astype(BF16)
        return _dot(tril, hi) + _dot(tril, lo)

    cr = cumsum_rows(zr)
    ci = cumsum_rows(zi)
    wr, wi = pow_r_ref[...], pow_i_ref[...]
    p1r, p1i = pow1_r_ref[...], pow1_i_ref[...]
    sr, si = sr_ref[...], si_ref[...]
    xr = wr * cr - wi * ci + (p1r * sr - p1i * si)
    xi = wr * ci + wi * cr + (p1r * si + p1i * sr)
    rows = xr.shape[0]
    sr_ref[...] = xr[rows - 1:rows]
    si_ref[...] = xi[rows - 1:rows]
    y = _dot(xr.astype(BF16), cr_ref[...]) - _dot(xi.astype(BF16), ci_ref[...]) + d_ref[...] * u
    o_ref[0] = _gelu_tanh(y)


def _s5(proj3, a_re, a_im, log_dt, b_re, b_im, c_re, c_im, d_skip):
    b, seq, _ = proj3.shape
    rows = min(S5_ROWS, seq)
    g, n, p = S5_GROUPS, S5_STATE, S5_GROUP
    dt = jnp.exp(log_dt)[:, None]
    lr, li = a_re, a_im
    mag = jnp.exp(lr * dt)
    ab_r, ab_i = mag * jnp.cos(li * dt), mag * jnp.sin(li * dt)
    den = lr * lr + li * li
    nr, ni = ab_r - 1.0, ab_i
    cr, ci = (nr * lr + ni * li) / den, (ni * lr - nr * li) / den
    bb_r = cr[..., None] * b_re - ci[..., None] * b_im
    bb_i = cr[..., None] * b_im + ci[..., None] * b_re
    eye_g = jnp.eye(g, dtype=F32)
    bbd_r = jnp.einsum('gnp,gh->gphn', bb_r, eye_g).reshape(g * p, g * n)
    bbd_i = jnp.einsum('gnp,gh->gphn', bb_i, eye_g).reshape(g * p, g * n)
    bbd = jnp.concatenate([bbd_r, bbd_i], axis=1).astype(BF16)
    cbd_r = jnp.einsum('gpn,gh->gnhp', c_re, eye_g).reshape(g * n, g * p).astype(BF16)
    cbd_i = jnp.einsum('gpn,gh->gnhp', c_im, eye_g).reshape(g * n, g * p).astype(BF16)
    steps = jnp.arange(rows, dtype=F32)[:, None]
    lrd = (lr * dt).reshape(1, g * n)
    lid = (li * dt).reshape(1, g * n)

    def power(k):
        m = jnp.exp(lrd * k)
        return m * jnp.cos(lid * k), m * jnp.sin(lid * k)

    pinv_r, pinv_i = power(-steps)
    pow_r, pow_i = power(steps)
    pow1_r, pow1_i = power(steps + 1.0)
    tril = jnp.asarray(np.tril(np.ones((rows, rows), np.float32)), BF16)
    tab = _const_spec((rows, g * n))
    out = pl.pallas_call(
        _s5_kernel,
        grid=(b, seq // rows),
        in_specs=[
            pl.BlockSpec((1, rows, S5_WIDTH), lambda i, t: (i, t, OFF_U // S5_WIDTH)),
            _const_spec((S5_WIDTH, 2 * g * n)),
            tab, tab, tab, tab, tab, tab,
            _const_spec((g * n, S5_WIDTH)),
            _const_spec((g * n, S5_WIDTH)),
            _const_spec((1, S5_WIDTH)),
            _const_spec((rows, rows)),
        ],
        out_specs=pl.BlockSpec((1, rows, S5_WIDTH), lambda i, t: (i, t, 0)),
        out_shape=jax.ShapeDtypeStruct((b, seq, S5_WIDTH), F32),
        scratch_shapes=[pltpu.VMEM((1, g * n), F32), pltpu.VMEM((1, g * n), F32)],
        compiler_params=_params("parallel", "arbitrary"),
        name="s5_mixer",
    )(proj3, bbd, pinv_r, pinv_i, pow_r, pow_i, pow1_r, pow1_i, cbd_r, cbd_i, d_skip.reshape(1, S5_WIDTH), tril)
    return out.reshape(b * seq, S5_WIDTH)


def _combine_kernel(x_ref, ga_ref, gb_ref, gc_ref, o1_ref, o2_ref, o3_ref, l1_ref, l2_ref, l3_ref,
                    ob_ref, s_ref, wa_ref, wb_ref, wcv_ref, wcg_ref, wo_ref, ex_ref, nf_ref,
                    x1_ref, hf_ref):
    l1, l2, l3 = l1_ref[...], l2_ref[...], l3_ref[...]
    m = jnp.maximum(jnp.maximum(l1, l2), l3)
    e1, e2, e3 = jnp.exp(l1 - m), jnp.exp(l2 - m), jnp.exp(l3 - m)
    inv = 1.0 / (e1 + e2 + e3)
    ex = ex_ref[...]

    def expand(wt):
        hi = wt.astype(BF16)
        lo = (wt - hi.astype(F32)).astype(BF16)
        return _dot(hi, ex) + _dot(lo, ex)

    mix = expand(e1 * inv) * o1_ref[...] + expand(e2 * inv) * o2_ref[...] + expand(e3 * inv) * o3_ref[...]
    y_a = _dot(mix.astype(BF16), wa_ref[...])
    y_b = _dot(ob_ref[...].astype(BF16), wb_ref[...])
    sb = s_ref[...].astype(BF16)
    y_c = _dot(sb, wcv_ref[...]) * _sigmoid(_dot(sb, wcg_ref[...]))
    mixed = _sigmoid(ga_ref[...]) * y_a + _sigmoid(gb_ref[...]) * y_b + _sigmoid(gc_ref[...]) * y_c
    x1 = x_ref[...] + _dot(mixed.astype(BF16), wo_ref[...])
    x1_ref[...] = x1
    hf_ref[...] = _rms(x1, nf_ref[...])


def _combine(x2d, proj2, att, ob, s, w_a_out, w_b_out, w_c_val, w_c_gate, w_out, norm_ffn, tm=512):
    t, d = x2d.shape
    tm = min(tm, t)
    (o1, l1), (o2, l2), (o3, l3) = att
    ex = np.zeros((128, ATT_WIDTH), np.float32)
    for h in range(ATT_SLOTS):
        ex[h, h * ATT_HEAD_DIM:(h + 1) * ATT_HEAD_DIM] = 1.0
    row = lambda w: pl.BlockSpec((tm, w), lambda i: (i, 0))
    gate = lambda k: pl.BlockSpec((tm, d), lambda i: (i, OFF_GATES // d + k))
    bf = lambda w: w.astype(BF16)
    return pl.pallas_call(
        _combine_kernel,
        grid=(t // tm,),
        in_specs=[
            row(d), gate(0), gate(1), gate(2),
            row(ATT_WIDTH), row(ATT_WIDTH), row(ATT_WIDTH), row(128), row(128), row(128),
            row(DN_WIDTH), row(S5_WIDTH),
            _const_spec((ATT_WIDTH, d)), _const_spec((DN_WIDTH, d)), _const_spec((S5_WIDTH, d)),
            _const_spec((S5_WIDTH, d)), _const_spec((d, d)), _const_spec((128, ATT_WIDTH)), _const_spec((1, d)),
        ],
        out_specs=[row(d), row(d)],
        out_shape=[jax.ShapeDtypeStruct((t, d), F32), jax.ShapeDtypeStruct((t, d), F32)],
        compiler_params=_params("parallel"),
        name="branch_combine",
    )(x2d, proj2, proj2, proj2, o1, o2, o3, l1, l2, l3, ob, s,
      bf(w_a_out), bf(w_b_out), bf(w_c_val), bf(w_c_gate), bf(w_out), jnp.asarray(ex, BF16), norm_ffn.reshape(1, d))


def _ple_kernel(x_ref, f_ref, pe_ref, pn_ref, wg_ref, pw_ref, o_ref):
    x2 = x_ref[...] + f_ref[...]
    gate = _sigmoid(_dot(_rms(x2, pn_ref[...]).astype(BF16), wg_ref[...]))
    o_ref[...] = x2 + gate * _dot(pe_ref[...].astype(BF16), pw_ref[...])


def _ple(x1, ffn, pe2d, ple_norm, ple_w_gate, ple_w, tm=512):
    t, d = x1.shape
    tm = min(tm, t)
    row = lambda w: pl.BlockSpec((tm, w), lambda i: (i, 0))
    return pl.pallas_call(
        _ple_kernel,
        grid=(t // tm,),
        in_specs=[row(d), row(d), row(PLE_DIM), _const_spec((1, d)), _const_spec((d, d)), _const_spec((PLE_DIM, d))],
        out_specs=row(d),
        out_shape=jax.ShapeDtypeStruct((t, d), F32),
        compiler_params=_params("parallel"),
        name="ple_residual",
    )(x1, ffn, pe2d, ple_norm.reshape(1, d), ple_w_gate.astype(BF16), ple_w.astype(BF16))


PEER_PICKS = PEER_HEADS * PEER_TOPK
PEER_EXPERTS = PEER_KEYS * PEER_KEYS
PEER_HALF = D_MODEL // 2
ROW_SUBLANES = PEER_HALF // 128
TABLE_PAD = 8
ROUTE_ROWS = 128
EXPERT_ROWS = 256
UP_BATCH = 32
DOWN_BATCH = 16


STAIR_COUNTS = tuple(PEER_TOPK // (a + 1) for a in range(PEER_TOPK))
STAIR_ROWS = -(-sum(STAIR_COUNTS) // 8) * 8
POS_PAD = 1 << 20


def _top_rows(work, row_id, k, big):
    vals, ids = [], []
    for _ in range(k):
        m = jnp.max(work, axis=0, keepdims=True)
        pos = jnp.min(jnp.where(work == m, row_id, big), axis=0, keepdims=True)
        vals.append(m)
        ids.append(pos)
        work = jnp.where(row_id == pos, -jnp.inf, work)
    return vals, ids


def _peer_route_kernel(hf_ref, wq_ref, k1_ref, k2_ref, pos_ref, idx_ref, gate_ref):
    rows = hf_ref.shape[0]
    qry = _dot(hf_ref[...].astype(BF16), wq_ref[...]).astype(BF16)
    k1, k2 = k1_ref[...], k2_ref[...]
    key_id = lax.broadcasted_iota(jnp.int32, (PEER_KEYS, rows), 0).astype(F32)
    cand_pos = pos_ref[...]
    pad = STAIR_ROWS - sum(STAIR_COUNTS)
    idx_rows, gate_rows = [], []
    for h in range(PEER_HEADS):
        q1 = qry[:, (2 * h) * PEER_KEY_DIM:(2 * h + 1) * PEER_KEY_DIM]
        q2 = qry[:, (2 * h + 1) * PEER_KEY_DIM:(2 * h + 2) * PEER_KEY_DIM]
        v1, i1 = _top_rows(_dot_nt(k1, q1), key_id, PEER_TOPK, float(PEER_KEYS))
        v2, i2 = _top_rows(_dot_nt(k2, q2), key_id, PEER_TOPK, float(PEER_KEYS))
        v2c = jnp.concatenate(v2, axis=0)
        i2c = jnp.concatenate(i2, axis=0)
        cand_s = jnp.concatenate([v1[a] + v2c[0:n] for a, n in enumerate(STAIR_COUNTS)]
                                 + [jnp.full((pad, rows), -jnp.inf, F32)], axis=0)
        cand_i = jnp.concatenate([i1[a] * PEER_KEYS + i2c[0:n] for a, n in enumerate(STAIR_COUNTS)]
                                 + [jnp.zeros((pad, rows), F32)], axis=0)
        top_s, pos = _top_rows(cand_s, cand_pos, PEER_TOPK, float(POS_PAD))
        e = [jnp.exp(s - top_s[0]) for s in top_s]
        den = e[0]
        for x in e[1:]:
            den = den + x
        inv = 1.0 / den
        for s, p in zip(e, pos):
            gate_rows.append(s * inv)
            idx_rows.append(jnp.sum(jnp.where(cand_pos == p, cand_i, 0.0), axis=0, keepdims=True))
    idx_ref[...] = (jnp.concatenate(idx_rows, axis=0).T * ROW_SUBLANES + TABLE_PAD).astype(jnp.int32)
    gate_ref[...] = jnp.concatenate(gate_rows, axis=0).T


def _peer_route(hf, w_query, keys1, keys2):
    t, d = hf.shape
    rows = min(ROUTE_ROWS, t)
    nq = w_query.shape[1]
    flat_pos = [a * PEER_TOPK + b for a, n in enumerate(STAIR_COUNTS) for b in range(n)]
    flat_pos += [POS_PAD] * (STAIR_ROWS - len(flat_pos))
    cand_pos = jnp.asarray(np.tile(np.asarray(flat_pos, np.float32)[:, None], (1, rows)))
    return pl.pallas_call(
        _peer_route_kernel,
        grid=(t // rows,),
        in_specs=[pl.BlockSpec((rows, d), lambda i: (i, 0)), _const_spec((d, nq)),
                  _const_spec((PEER_KEYS, PEER_KEY_DIM)), _const_spec((PEER_KEYS, PEER_KEY_DIM)),
                  _const_spec((STAIR_ROWS, rows))],
        out_specs=[pl.BlockSpec((rows, PEER_PICKS), lambda i: (i, 0)), pl.BlockSpec((rows, PEER_PICKS), lambda i: (i, 0))],
        out_shape=[jax.ShapeDtypeStruct((t, PEER_PICKS), jnp.int32), jax.ShapeDtypeStruct((t, PEER_PICKS), F32)],
        compiler_params=_params("parallel"),
        name="peer_route",
    )(hf, w_query.astype(BF16), keys1.astype(BF16), keys2.astype(BF16), cand_pos)


def _pack_table(table):
    bits = lax.bitcast_convert_type(table.astype(BF16), jnp.uint16).astype(jnp.uint32)
    words = bits[:, :PEER_HALF] | (bits[:, PEER_HALF:] << 16)
    return jnp.pad(words.reshape(table.shape[0] * ROW_SUBLANES, 128), ((TABLE_PAD, TABLE_PAD), (0, 0)))


def _unpack_row(words):
    lo = lax.bitcast_convert_type(words << 16, F32)
    hi = lax.bitcast_convert_type(words & jnp.uint32(0xFFFF0000), F32)
    return lo, hi


def _load_table_once(tab_hbm, tab_vmem, sem):
    @pl.when(pl.program_id(0) == 0)
    def _():
        cp = pltpu.make_async_copy(tab_hbm, tab_vmem, sem)
        cp.start()
        cp.wait()


def _stage_ids(idx_hbm, idx_smem, sems, count):
    step = pl.program_id(0)
    slot = step % 2

    def copy(s, buf):
        return pltpu.make_async_copy(idx_hbm.at[pl.ds(s * count, count)],
                                     idx_smem.at[pl.ds(buf * count, count)], sems.at[1 + buf])

    @pl.when(step == 0)
    def _():
        copy(0, 0).start()

    @pl.when(step + 1 < pl.num_programs(0))
    def _():
        copy(step + 1, 1 - slot).start()

    copy(step, slot).wait()
    return slot * count


def _peer_up_kernel(idx_hbm, x_ref, gate_ref, sel_ref, tab_hbm, o_ref, tab_ref, idx_smem, part_ref,
                    xx_ref, sems):
    rows = gate_ref.shape[0]
    _load_table_once(tab_hbm, tab_ref, sems.at[0])
    id0 = _stage_ids(idx_hbm, idx_smem, sems, rows * PEER_PICKS)
    half = PEER_HALF // 128
    batch = part_ref.shape[1] // 8
    r_id =lax.broadcasted_iota(jnp.int32, (8, PEER_PICKS), 0)
    c_id = lax.broadcasted_iota(jnp.int32, (8, PEER_PICKS), 1)
    row_pick = 4 * (r_id % 2) + 2 * ((r_id // 2) % 2) + (r_id // 4)
    diag = (c_id % 8 == row_pick)[None]
    sub_id = lax.broadcasted_iota(jnp.int32, (8, 128), 0)
    low2 = (sub_id % 4) < 2
    even = (sub_id % 2) == 0
    low = sub_id < ROW_SUBLANES
    group = 8
    unroll = 4

    def fold_rows(p0, p1, p2, p3):
        def half_fold(a, b, mask, s):
            x = jnp.where(mask, a, pltpu.roll(b, s, 0))
            y = jnp.where(mask, pltpu.roll(a, 8 - s, 0), b)
            return x + y
        return half_fold(half_fold(p0, p1, low2, 2), half_fold(p2, p3, low2, 2), even, 1)

    def token_batch(b, carry):
        base = pl.multiple_of(b * batch, batch)

        def token_group(tg, c):
            tg8 = pl.multiple_of(tg * group, group)
            id_refs = [idx_smem.at[pl.ds(pl.multiple_of(id0 + (base + tg8 + i) * PEER_PICKS, PEER_PICKS), PEER_PICKS)]
                       for i in range(group)]
            for i in range(group):
                x_lo = x_ref[base + tg8 + i, 0:half, :]
                x_hi = x_ref[base + tg8 + i, half:2 * half, :]
                xx_ref[i, 0:8, :] = jnp.concatenate([x_lo, x_lo], axis=0)
                xx_ref[i, 8:16, :] = jnp.concatenate([x_hi, x_hi], axis=0)

            def pick_step(k, cc):
                for u in range(unroll):
                    ku = k * unroll + u
                    k8 = pl.multiple_of(ku * 8, 8)
                    o = [k8 + j for j in range(8)]
                    dst = part_ref.at[ku, pl.ds(pl.multiple_of(tg8 * 8, 8 * group), 8 * group), :]
                    for i in range(group):
                        x_lo2 = xx_ref[i, 0:8, :]
                        x_hi2 = xx_ref[i, 8:16, :]
                        tiles = []
                        for j in range(0, 8, 2):
                            ra, rb = id_refs[i][o[j]], id_refs[i][o[j + 1]]
                            words = jnp.where(low, tab_ref[pl.ds(ra, 8), :], tab_ref[pl.ds(rb - ROW_SUBLANES, 8), :])
                            lo, hi = _unpack_row(words)
                            tiles.append(lo * x_lo2 + hi * x_hi2)
                        dst[i * 8:(i + 1) * 8, :] = fold_rows(*tiles)
                return cc

            lax.fori_loop(0, PEER_PICKS // 8 // unroll, pick_step, 0)
            return c

        lax.fori_loop(0, batch // group, token_group, 0)
        sums = jnp.zeros((batch * 8, PEER_PICKS), F32)
        for k in range(PEER_PICKS // 8):
            part = part_ref[k]
            sums = sums + _dot(part.astype(BF16), sel_ref[k])
        act = jnp.sum(jnp.where(diag, sums.reshape(batch, 8, PEER_PICKS), 0.0), axis=1)
        o_ref[pl.ds(base, batch), :] = gate_ref[pl.ds(base, batch), :] * _gelu_tanh(act)
        return carry

    lax.fori_loop(0, rows // batch, token_batch, 0)


def _peer_down_kernel(idx_hbm, w_ref, ones_ref, tab_hbm, o_ref, tab_ref, idx_smem, splat_ref, sems):
    rows = w_ref.shape[0]
    _load_table_once(tab_hbm, tab_ref, sems.at[0])
    id0 = _stage_ids(idx_hbm, idx_smem, sems, rows * PEER_PICKS)
    half = PEER_HALF // 128
    batch = splat_ref.shape[0]
    group = 4
    unroll = 4
    low = lax.broadcasted_iota(jnp.int32, (8, 128), 0) < ROW_SUBLANES
    eye =(lax.broadcasted_iota(jnp.int32, (PEER_PICKS, PEER_PICKS), 0)
           == lax.broadcasted_iota(jnp.int32, (PEER_PICKS, PEER_PICKS), 1)).astype(F32)
    ones = ones_ref[...]

    def token_batch(b, carry):
        base = pl.multiple_of(b * batch, batch)
        for i in range(batch):
            d = eye * w_ref[pl.ds(base + i, 1), :]
            splat_ref[i] = _dot(d.astype(BF16), ones)

        def token_group(tg, c):
            tg8 = pl.multiple_of(tg * group, group)
            id_refs = [idx_smem.at[pl.ds(pl.multiple_of(id0 + (base + tg8 + i) * PEER_PICKS, PEER_PICKS), PEER_PICKS)]
                       for i in range(group)]

            def pick_step(k, acc):
                acc = list(acc)
                for u in range(unroll):
                    k8 = pl.multiple_of((k * unroll + u) * 8, 8)
                    o = [k8 + j for j in range(8)]
                    sp = splat_ref.at[pl.ds(tg8, group), pl.ds(k8, 8), :]
                    for i in range(group):
                        for j in range(0, 8, 2):
                            ra, rb = id_refs[i][o[j]], id_refs[i][o[j + 1]]
                            words = jnp.where(low, tab_ref[pl.ds(ra, 8), :], tab_ref[pl.ds(rb - ROW_SUBLANES, 8), :])
                            lo, hi = _unpack_row(words)
                            w = jnp.where(low, jnp.broadcast_to(sp[i, j:j + 1, :], (8, 128)),
                                          jnp.broadcast_to(sp[i, j + 1:j + 2, :], (8, 128)))
                            acc[2 * i] = acc[2 * i] + w * lo
                            acc[2 * i + 1] = acc[2 * i + 1] + w * hi
                return tuple(acc)

            zero = jnp.zeros((8, 128), F32)
            acc = lax.fori_loop(0, PEER_PICKS // 8 // unroll, pick_step, (zero,) * (2 * group))
            for i in range(group):
                o_ref[base + tg8 + i, 0:half, :] = acc[2 * i][0:half] + acc[2 * i][half:2 * half]
                o_ref[base + tg8 + i, half:2 * half, :] = acc[2 * i + 1][0:half] + acc[2 * i + 1][half:2 * half]
            return c

        lax.fori_loop(0, batch // group, token_group, 0)
        return carry

    lax.fori_loop(0, rows // batch, token_batch, 0)


def _peer_experts(hf, idx, gate, expert_u, expert_v):
    t, d = hf.shape
    rows = min(EXPERT_ROWS, t)
    sub = d // 128
    x3 = hf.reshape(t, sub, 128)
    sel = np.zeros((PEER_PICKS // 8, 128, PEER_PICKS), np.float32)
    for j in range(PEER_PICKS // 8):
        sel[j, :, 8 * j:8 * j + 8] = 1.0
    sel = jnp.asarray(sel, BF16)
    tab_shape = (PEER_EXPERTS * ROW_SUBLANES + 2 * TABLE_PAD, 128)
    picks = pl.BlockSpec((rows, PEER_PICKS), lambda i: (i, 0))
    hbm = pl.BlockSpec(memory_space=pl.ANY)
    tok3 = pl.BlockSpec((rows, sub, 128), lambda i: (i, 0, 0))
    idx_flat = idx.reshape(t * PEER_PICKS)
    ids_smem = pltpu.SMEM((2 * rows * PEER_PICKS,), jnp.int32)
    w = pl.pallas_call(
        _peer_up_kernel,
        grid=(t // rows,),
        in_specs=[hbm, tok3, picks, _const_spec(sel.shape), hbm],
        out_specs=picks,
        out_shape=jax.ShapeDtypeStruct((t, PEER_PICKS), F32),
        scratch_shapes=[pltpu.VMEM(tab_shape, jnp.uint32), ids_smem,
                        pltpu.VMEM((PEER_PICKS // 8, UP_BATCH * 8, 128), F32),
                        pltpu.VMEM((8, 16, 128), F32), pltpu.SemaphoreType.DMA((3,))],
        compiler_params=_params("arbitrary"),
        name="peer_up",
    )(idx_flat, x3, gate, sel, _pack_table(expert_u))
    out = pl.pallas_call(
        _peer_down_kernel,
        grid=(t // rows,),
        in_specs=[hbm, picks, _const_spec((PEER_PICKS, 128)), hbm],
        out_specs=tok3,
        out_shape=jax.ShapeDtypeStruct((t, sub, 128), F32),
        scratch_shapes=[pltpu.VMEM(tab_shape, jnp.uint32), ids_smem,
                        pltpu.VMEM((DOWN_BATCH, PEER_PICKS, 128), F32), pltpu.SemaphoreType.DMA((3,))],
        compiler_params=_params("arbitrary"),
        name="peer_down",
    )(idx_flat, w, jnp.ones((PEER_PICKS, 128), BF16), _pack_table(expert_v))
    return out.reshape(t, d)


def _peer(hf, w_query, keys1, keys2, expert_u, expert_v):
    idx, gate = _peer_route(hf, w_query, keys1, keys2)
    return _peer_experts(hf, idx, gate, expert_u, expert_v)


def _pad_w_in(w_in):
    d = w_in.shape[0]
    outs = [jnp.zeros((d, PROJ_COLS), BF16), jnp.zeros((d, ATT_COLS), BF16)]
    src = 0
    for size, (arr, dst) in zip(SRC_SIZES, SRC_DST):
        outs[arr] = outs[arr].at[:, dst:dst + size].set(w_in[:, src:src + size].astype(BF16))
        src += size
    return outs


def _layer(x2d, pe2d, b, seq, norm_mix, w_in, att_q_gain, att_k_gain, w_a_out, dn_conv, dn_a_log, dn_dt_bias,
           dn_out_gain, w_b_out, s5_a_re, s5_a_im, s5_log_dt, s5_b_re, s5_b_im, s5_c_re, s5_c_im, s5_d,
           w_c_val, w_c_gate, w_out, norm_ffn, peer_w_query, peer_keys1, peer_keys2, peer_u, peer_v,
           ple_w, ple_norm, ple_w_gate):
    w_main, w_att = _pad_w_in(w_in)
    proj2 = _norm_matmul(x2d, norm_mix, w_main, tn=512)
    proj3 = proj2.reshape(b, seq, PROJ_COLS)
    proj_att = _norm_matmul(x2d, norm_mix, w_att, tn=ATT_COLS).reshape(b, seq, ATT_COLS)
    att = [_attention_group(proj_att, att_q_gain, att_k_gain, g) for g in range(len(ATT_GROUPS))]
    ob = _deltanet(proj3, dn_conv, dn_a_log, dn_dt_bias, dn_out_gain)
    s = _s5(proj3, s5_a_re, s5_a_im, s5_log_dt, s5_b_re, s5_b_im, s5_c_re, s5_c_im, s5_d)
    x1, hf = _combine(x2d, proj2, att, ob, s, w_a_out, w_b_out, w_c_val, w_c_gate, w_out, norm_ffn)
    ffn = _peer(hf, peer_w_query, peer_keys1, peer_keys2, peer_u, peer_v)
    return _ple(x1, ffn, pe2d, ple_norm, ple_w_gate, ple_w)


def kernel(x, p, norm_mix, w_in, att_q_gain, att_k_gain, w_a_out, dn_conv, dn_a_log, dn_dt_bias, dn_out_gain, w_b_out, s5_a_re, s5_a_im, s5_log_dt, s5_b_re, s5_b_im, s5_c_re, s5_c_im, s5_d, w_c_val, w_c_gate, w_out, norm_ffn, peer_w_query, peer_keys1, peer_keys2, peer_u, peer_v, ple_w, ple_norm, ple_w_gate):
    b, seq, d = x.shape
    depth = p.shape[0]
    x2d = x.reshape(b * seq, d)
    for i in range(depth):
        x2d = _layer(x2d, p[i].reshape(b * seq, PLE_DIM), b, seq, norm_mix[i], w_in[i], att_q_gain[i], att_k_gain[i],
                     w_a_out[i], dn_conv[i], dn_a_log[i], dn_dt_bias[i], dn_out_gain[i], w_b_out[i], s5_a_re[i],
                     s5_a_im[i], s5_log_dt[i], s5_b_re[i], s5_b_im[i], s5_c_re[i], s5_c_im[i], s5_d[i], w_c_val[i],
                     w_c_gate[i], w_out[i], norm_ffn[i], peer_w_query[i], peer_keys1[i], peer_keys2[i], peer_u[i],
                     peer_v[i], ple_w[i], ple_norm[i], ple_w_gate[i])
    return x2d.reshape(b, seq, d)
```

```python
import functools

import numpy as np
import jax
import jax.numpy as jnp
from jax import lax
from jax.experimental import pallas as pl
from jax.experimental.pallas import tpu as pltpu

F32 = jnp.float32
BF16 = jnp.bfloat16
HI = lax.Precision.HIGHEST

D_MODEL = 1024
NORM_EPS = 1e-6
ATT_GROUPS = ((128, 1), (512, 4), (2048, 16))
ATT_SLOTS = 4
ATT_HEAD_DIM = 64
ATT_WIDTH = 256
ATT_BLOCK = 128
ALIBI_MAX_EXP = 8.0
DN_HEADS = 4
DN_HEAD_DIM = 128
DN_WIDTH = 512
DN_CONV = 4
DN_CHUNK = 64
S5_GROUP = 16
S5_GROUPS = 16
S5_WIDTH = 256
S5_STATE = 64
PEER_HEADS = 8
PEER_KEYS = 128
PEER_TOPK = 16
PEER_KEY_DIM = 128
PEER_BLOCK = 128
PLE_DIM = 256

OFF_QKV, OFF_Z, OFF_GATES, OFF_U, OFF_BA = 0, 1536, 2048, 5120, 5376
PROJ_COLS = 5632
OFF_QA, OFF_KA, OFF_VA = 0, 768, 1024
ATT_COLS = 1280
SRC_SIZES = (768, 256, 256, 1536, 512, 4, 4, 256, 3072)
SRC_DST = ((1, OFF_QA), (1, OFF_KA), (1, OFF_VA), (0, OFF_QKV), (0, OFF_Z), (0, OFF_BA), (0, OFF_BA + 4),
           (0, OFF_U), (0, OFF_GATES))

VMEM_LIMIT = 56 * 1024 * 1024
NEG_BIG = -1e30


def _params(*sem):
    return pltpu.CompilerParams(dimension_semantics=sem, vmem_limit_bytes=VMEM_LIMIT)


def _const_spec(shape):
    nd = len(shape)
    return pl.BlockSpec(shape, lambda *_: (0,) * nd)


def _rms(x, gain):
    return x * lax.rsqrt(jnp.mean(x * x, axis=-1, keepdims=True) + NORM_EPS) * gain


def _sigmoid(x):
    return 1.0 / (1.0 + jnp.exp(-x))


def _gelu_tanh(x):
    return 0.5 * x * (1.0 + jnp.tanh(0.7978845608028654 * (x + 0.044715 * (x * x * x))))


def _dot(a, b, **kw):
    return jnp.dot(a, b, preferred_element_type=F32, **kw)


def _dot3(a, b):
    a_hi, b_hi = a.astype(BF16), b.astype(BF16)
    a_lo = (a - a_hi.astype(F32)).astype(BF16)
    b_lo = (b - b_hi.astype(F32)).astype(BF16)
    return _dot(a_hi, b_hi) + (_dot(a_hi, b_lo) + _dot(a_lo, b_hi))


def _dot_nt(a, b):
    return lax.dot_general(a, b, (((1,), (1,)), ((), ())), preferred_element_type=F32)


def _dot_tn(a, b):
    return lax.dot_general(a, b, (((0,), (0,)), ((), ())), preferred_element_type=F32)


def _norm_matmul_kernel(x_ref, g_ref, w_ref, o_ref, h_ref):
    @pl.when(pl.program_id(1) == 0)
    def _():
        h_ref[...] = _rms(x_ref[...], g_ref[...]).astype(BF16)

    o_ref[...] = _dot(h_ref[...], w_ref[...])


def _norm_matmul(x2d, gain, w_bf16, tm=1024, tn=1024):
    t, d = x2d.shape
    n = w_bf16.shape[1]
    tm = min(tm, t)
    tn = min(tn, n)
    return pl.pallas_call(
        _norm_matmul_kernel,
        grid=(t // tm, n // tn),
        in_specs=[
            pl.BlockSpec((tm, d), lambda i, j: (i, 0)),
            pl.BlockSpec((1, d), lambda i, j: (0, 0)),
            pl.BlockSpec((d, tn), lambda i, j: (0, j)),
        ],
        out_specs=pl.BlockSpec((tm, tn), lambda i, j: (i, j)),
        out_shape=jax.ShapeDtypeStruct((t, n), F32),
        scratch_shapes=[pltpu.VMEM((tm, d), BF16)],
        compiler_params=_params("parallel", "arbitrary"),
        name="norm_matmul",
    )(x2d, gain.reshape(1, d), w_bf16)


def _attn_kernel(q_ref, kp_ref, kc_ref, vp_ref, vc_ref, qg_ref, kg_ref, bd_ref, o_ref, lse_ref, *,
                 dilation, span, slopes):
    n = pl.program_id(2)
    bd = bd_ref[...]

    def head_norm(x, gain):
        x2 = x * x
        hi = x2.astype(BF16)
        lo = (x2 - hi.astype(F32)).astype(BF16)
        ss = _dot(hi, bd) + _dot(lo, bd)
        return x * lax.rsqrt(ss * (1.0 / ATT_HEAD_DIM) + NORM_EPS) * gain

    qpos = lax.broadcasted_iota(jnp.int32, (ATT_BLOCK, 2 * ATT_BLOCK), 0)
    kpos = lax.broadcasted_iota(jnp.int32, (ATT_BLOCK, 2 * ATT_BLOCK), 1)
    rel = qpos + ATT_BLOCK - kpos
    valid = (rel >= 0) & (rel <= span) & ((kpos >= ATT_BLOCK) | (n > 0))
    dist = (rel * dilation).astype(F32)
    lane = lax.broadcasted_iota(jnp.int32, (ATT_BLOCK, 128), 1)
    col_head = lax.broadcasted_iota(jnp.int32, (ATT_BLOCK, ATT_WIDTH), 1) // ATT_HEAD_DIM
    heads = range(ATT_SLOTS)

    q = head_norm(q_ref[0], qg_ref[...]) * (ATT_HEAD_DIM ** -0.5)
    k = head_norm(jnp.concatenate([kp_ref[0], kc_ref[0]], axis=0), kg_ref[...])
    v = jnp.concatenate([vp_ref[0], vc_ref[0]], axis=0)
    kb, vb = k.astype(BF16), v.astype(BF16)
    s = [_dot_nt(jnp.where(col_head == h, q, 0.0).astype(BF16), kb) for h in heads]
    s = [jnp.where(valid, s[h] - slopes[h] * dist, NEG_BIG) for h in heads]
    m = [jnp.max(s[h], axis=-1, keepdims=True) for h in heads]
    e = [jnp.exp(s[h] - m[h]) for h in heads]
    den = [jnp.sum(e[h], axis=-1, keepdims=True) for h in heads]
    pv = [_dot(e[h].astype(BF16), vb) for h in heads]
    out = jnp.zeros((ATT_BLOCK, ATT_WIDTH), F32)
    lse_all = jnp.zeros((ATT_BLOCK, 128), F32)
    for h in heads:
        out = jnp.where(col_head == h, pv[h] / den[h], out)
        lse_all = jnp.where(lane == h, m[h] + jnp.log(den[h]), lse_all)
    o_ref[0] = out
    lse_ref[0] = lse_all


def _attention_group(proj3, q_gain, k_gain, group):
    b, seq, _ = proj3.shape
    window, dilation = ATT_GROUPS[group]
    span = window // dilation
    sub = seq // dilation
    nb = sub // ATT_BLOCK
    view = proj3.reshape(b, sub, dilation * ATT_COLS)
    cb = ATT_COLS // ATT_WIDTH
    qc, kc, vc = OFF_QA // ATT_WIDTH + group, OFF_KA // ATT_WIDTH, OFF_VA // ATT_WIDTH
    n_heads = len(ATT_GROUPS) * ATT_SLOTS
    slopes = 2.0 ** (-ALIBI_MAX_EXP * np.arange(1, n_heads + 1) / n_heads)
    slopes = tuple(float(s) for s in slopes.reshape(len(ATT_GROUPS), ATT_SLOTS)[group])
    blk = (1, ATT_BLOCK, ATT_WIDTH)
    bd = (np.arange(ATT_WIDTH)[:, None] // ATT_HEAD_DIM == np.arange(ATT_WIDTH)[None, :] // ATT_HEAD_DIM)
    o, lse = pl.pallas_call(
        functools.partial(_attn_kernel, dilation=dilation, span=span, slopes=slopes),
        grid=(b, dilation, nb),
        in_specs=[
            pl.BlockSpec(blk, lambda i, r, n: (i, n, r * cb + qc)),
            pl.BlockSpec(blk, lambda i, r, n: (i, jnp.maximum(n - 1, 0), r * cb + kc)),
            pl.BlockSpec(blk, lambda i, r, n: (i, n, r * cb + kc)),
            pl.BlockSpec(blk, lambda i, r, n: (i, jnp.maximum(n - 1, 0), r * cb + vc)),
            pl.BlockSpec(blk, lambda i, r, n: (i, n, r * cb + vc)),
            _const_spec((1, ATT_WIDTH)),
            _const_spec((1, ATT_WIDTH)),
            _const_spec((ATT_WIDTH, ATT_WIDTH)),
        ],
        out_specs=[
            pl.BlockSpec(blk, lambda i, r, n: (i, n, r)),
            pl.BlockSpec((1, ATT_BLOCK, 128), lambda i, r, n: (i, n, r)),
        ],
        out_shape=[
            jax.ShapeDtypeStruct((b, sub, dilation * ATT_WIDTH), F32),
            jax.ShapeDtypeStruct((b, sub, dilation * 128), F32),
        ],
        compiler_params=_params("parallel", "parallel", "arbitrary"),
        name=f"dilated_attention_g{group}",
    )(view, view, view, view, view,
      jnp.tile(q_gain, ATT_SLOTS).reshape(1, ATT_WIDTH), jnp.tile(k_gain, ATT_SLOTS).reshape(1, ATT_WIDTH),
      jnp.asarray(bd, BF16))
    return o.reshape(b * seq, ATT_WIDTH), lse.reshape(b * seq, 128)


DN_ROWS = 256


def _deltanet_kernel(qkv_ref, z_ref, ba_ref, cw_ref, alog_ref, dtb_ref, og_ref, tril_ref, o_ref,
                     state_ref, tail_ref):
    c = DN_CHUNK
    rows = qkv_ref.shape[1]

    @pl.when(pl.program_id(1) == 0)
    def _():
        state_ref[...] = jnp.zeros_like(state_ref)
        tail_ref[...] = jnp.zeros_like(tail_ref)

    x = qkv_ref[0]
    xc = jnp.concatenate([tail_ref[...], x], axis=0)
    cw = cw_ref[...]
    acc = cw[DN_CONV - 1:DN_CONV] * x
    for j in range(DN_CONV - 1):
        acc = acc + cw[j:j + 1] * xc[5 + j:5 + j + rows]
    tail_ref[...] = x[rows - 8:rows]
    a = acc * _sigmoid(acc)

    ba = ba_ref[0]
    beta_all = _sigmoid(ba)
    sp = ba + dtb_ref[...]
    sp = jnp.maximum(sp, 0.0) + jnp.log(1.0 + jnp.exp(-jnp.abs(sp)))
    g_all = -(jnp.exp(alog_ref[...]) * sp)
    tril = tril_ref[...]
    ii = lax.broadcasted_iota(jnp.int32, (c, c), 0)
    jj = lax.broadcasted_iota(jnp.int32, (c, c), 1)
    incl = ii >= jj
    strict = ii > jj
    eye = (ii == jj).astype(F32)
    z = z_ref[0]
    og = og_ref[...]

    n_chunks = rows // c
    units = [(h, ci) for ci in range(n_chunks) for h in range(DN_HEADS)]

    loc = {}
    for h in range(DN_HEADS):
        hs = slice(h * DN_HEAD_DIM, (h + 1) * DN_HEAD_DIM)
        qh = a[:, hs]
        kh = a[:, DN_WIDTH + h * DN_HEAD_DIM:DN_WIDTH + (h + 1) * DN_HEAD_DIM]
        vh = a[:, 2 * DN_WIDTH + h * DN_HEAD_DIM:2 * DN_WIDTH + (h + 1) * DN_HEAD_DIM]
        qh = qh * lax.rsqrt(jnp.sum(qh * qh, axis=-1, keepdims=True) + NORM_EPS) * (DN_HEAD_DIM ** -0.5)
        kh = kh * lax.rsqrt(jnp.sum(kh * kh, axis=-1, keepdims=True) + NORM_EPS)
        beta = beta_all[:, h:h + 1]
        gh = jnp.broadcast_to(g_all[:, DN_HEADS + h:DN_HEADS + h + 1], (rows, DN_HEAD_DIM))
        for ci in range(n_chunks):
            rs = slice(ci * c, (ci + 1) * c)
            gc = _dot(tril, gh[rs], precision=HI)
            gc64 = gc[:, :c]
            diff = gc64 - gc64.T
            decay = jnp.where(incl, jnp.exp(jnp.where(incl, diff, 0.0)), 0.0)
            eg = jnp.exp(gc)
            g_last = gc[c - 1:c, :]
            kc_, qc_, vc_, bc_ = kh[rs], qh[rs], vh[rs], beta[rs]
            kbeta = kc_ * bc_
            kcb = kc_.astype(BF16)
            loc[h, ci] = dict(
                lower=jnp.where(strict, _dot_nt(kbeta.astype(BF16), kcb) * decay, 0.0),
                vb=(vc_ * bc_).astype(BF16), kbe=(kbeta * eg).astype(BF16),
                attn=jnp.where(incl, _dot_nt(qc_.astype(BF16), kcb) * decay, 0.0).astype(BF16),
                q_dec=(qc_ * eg).astype(BF16), k_dec=(kc_ * jnp.exp(g_last - gc)).astype(BF16),
                g_end=jnp.exp(g_last))

    t_inv = {k: eye - loc[k]["lower"] for k in units}
    pw = {k: _dot3(loc[k]["lower"], loc[k]["lower"]) for k in units}
    for it in range(5):
        t_inv = {k: t_inv[k] + _dot3(t_inv[k], pw[k]) for k in units}
        if it < 4:
            pw = {k: _dot3(pw[k], pw[k]) for k in units}
    uw = {}
    for k in units:
        t_b = t_inv[k].astype(BF16)
        uw[k] = (_dot(t_b, loc[k]["vb"]), _dot(t_b, loc[k]["kbe"]).astype(BF16))

    state = [state_ref[h] for h in range(DN_HEADS)]
    o_chunks = [[] for _ in range(DN_HEADS)]
    for ci in range(n_chunks):
        for h in range(DN_HEADS):
            d = loc[h, ci]
            u, w = uw[h, ci]
            sb = state[h].astype(BF16)
            vnb = (u - _dot(w, sb)).astype(BF16)
            o_chunks[h].append(_dot(d["q_dec"], sb) + _dot(d["attn"], vnb))
            state[h] = state[h] * d["g_end"] + _dot_tn(d["k_dec"], vnb)
    outs = []
    for h in range(DN_HEADS):
        state_ref[h] = state[h]
        zh = z[:, h * DN_HEAD_DIM:(h + 1) * DN_HEAD_DIM]
        outs.append(_rms(jnp.concatenate(o_chunks[h], axis=0), og) * (zh * _sigmoid(zh)))
    o_ref[0] = jnp.concatenate(outs, axis=-1)


def _deltanet(proj3, conv_w, a_log, dt_bias, out_gain):
    b, seq, _ = proj3.shape
    rows = min(DN_ROWS, seq)
    pad_a = jnp.zeros((1, 128), F32).at[0, DN_HEADS:2 * DN_HEADS].set(a_log)
    pad_d = jnp.zeros((1, 128), F32).at[0, DN_HEADS:2 * DN_HEADS].set(dt_bias)
    tril = jnp.asarray(np.tril(np.ones((DN_CHUNK, DN_CHUNK), np.float32)))
    out = pl.pallas_call(
        _deltanet_kernel,
        grid=(b, seq // rows),
        in_specs=[
            pl.BlockSpec((1, rows, 3 * DN_WIDTH), lambda i, t: (i, t, OFF_QKV // (3 * DN_WIDTH))),
            pl.BlockSpec((1, rows, DN_WIDTH), lambda i, t: (i, t, OFF_Z // DN_WIDTH)),
            pl.BlockSpec((1, rows, 128), lambda i, t: (i, t, OFF_BA // 128)),
            _const_spec((DN_CONV, 3 * DN_WIDTH)),
            _const_spec((1, 128)),
            _const_spec((1, 128)),
            _const_spec((1, DN_HEAD_DIM)),
            _const_spec((DN_CHUNK, DN_CHUNK)),
        ],
        out_specs=pl.BlockSpec((1, rows, DN_WIDTH), lambda i, t: (i, t, 0)),
        out_shape=jax.ShapeDtypeStruct((b, seq, DN_WIDTH), F32),
        scratch_shapes=[pltpu.VMEM((DN_HEADS, DN_HEAD_DIM, DN_HEAD_DIM), F32), pltpu.VMEM((8, 3 * DN_WIDTH), F32)],
        compiler_params=_params("parallel", "arbitrary"),
        name="gated_deltanet",
    )(proj3, proj3, proj3, conv_w, pad_a, pad_d, out_gain.reshape(1, DN_HEAD_DIM), tril)
    return out.reshape(b * seq, DN_WIDTH)


S5_ROWS = 128
S5_N = S5_GROUPS * S5_STATE


def _s5_kernel(u_ref, bbd_ref, pinv_r_ref, pinv_i_ref, pow_r_ref, pow_i_ref, pow1_r_ref, pow1_i_ref,
               cr_ref, ci_ref, d_ref, tril_ref, o_ref, sr_ref, si_ref):
    @pl.when(pl.program_id(1) == 0)
    def _():
        sr_ref[...] = jnp.zeros_like(sr_ref)
        si_ref[...] = jnp.zeros_like(si_ref)

    u = u_ref[0]
    bu = _dot(u.astype(BF16), bbd_ref[...])
    br, bi = bu[:, :S5_N], bu[:, S5_N:]
    pr, pi = pinv_r_ref[...], pinv_i_ref[...]
    zr = pr * br - pi * bi
    zi = pr * bi + pi * br
    tril = tril_ref[...]

    def cumsum_rows(zz):
        hi = zz.astype(BF16)
        lo = (zz - hi.astype(F32)).astype(BF16)
        return _dot(tril, hi) + _dot(tril, lo)

    cr = cumsum_rows(zr)
    ci = cumsum_rows(zi)
    wr, wi = pow_r_ref[...], pow_i_ref[...]
    p1r, p1i = pow1_r_ref[...], pow1_i_ref[...]
    sr, si = sr_ref[...], si_ref[...]
    xr = wr * cr - wi * ci + (p1r * sr - p1i * si)
    xi = wr * ci + wi * cr + (p1r * si + p1i * sr)
    rows = xr.shape[0]
    sr_ref[...] = xr[rows - 1:rows]
    si_ref[...] = xi[rows - 1:rows]
    y = _dot(xr.astype(BF16), cr_ref[...]) - _dot(xi.astype(BF16), ci_ref[...]) + d_ref[...] * u
    o_ref[0] = _gelu_tanh(y)


def _s5(proj3, a_re, a_im, log_dt, b_re, b_im, c_re, c_im, d_skip):
    b, seq, _ = proj3.shape
    rows = min(S5_ROWS, seq)
    g, n, p = S5_GROUPS, S5_STATE, S5_GROUP
    dt = jnp.exp(log_dt)[:, None]
    lr, li = a_re, a_im
    mag = jnp.exp(lr * dt)
    ab_r, ab_i = mag * jnp.cos(li * dt), mag * jnp.sin(li * dt)
    den = lr * lr + li * li
    nr, ni = ab_r - 1.0, ab_i
    cr, ci = (nr * lr + ni * li) / den, (ni * lr - nr * li) / den
    bb_r = cr[..., None] * b_re - ci[..., None] * b_im
    bb_i = cr[..., None] * b_im + ci[..., None] * b_re
    eye_g = jnp.eye(g, dtype=F32)
    bbd_r = jnp.einsum('gnp,gh->gphn', bb_r, eye_g).reshape(g * p, g * n)
    bbd_i = jnp.einsum('gnp,gh->gphn', bb_i, eye_g).reshape(g * p, g * n)
    bbd = jnp.concatenate([bbd_r, bbd_i], axis=1).astype(BF16)
    cbd_r = jnp.einsum('gpn,gh->gnhp', c_re, eye_g).reshape(g * n, g * p).astype(BF16)
    cbd_i = jnp.einsum('gpn,gh->gnhp', c_im, eye_g).reshape(g * n, g * p).astype(BF16)
    steps = jnp.arange(rows, dtype=F32)[:, None]
    lrd = (lr * dt).reshape(1, g * n)
    lid = (li * dt).reshape(1, g * n)

    def power(k):
        m = jnp.exp(lrd * k)
        return m * jnp.cos(lid * k), m * jnp.sin(lid * k)

    pinv_r, pinv_i = power(-steps)
    pow_r, pow_i = power(steps)
    pow1_r, pow1_i = power(steps + 1.0)
    tril = jnp.asarray(np.tril(np.ones((rows, rows), np.float32)), BF16)
    tab = _const_spec((rows, g * n))
    out = pl.pallas_call(
        _s5_kernel,
        grid=(b, seq // rows),
        in_specs=[
            pl.BlockSpec((1, rows, S5_WIDTH), lambda i, t: (i, t, OFF_U // S5_WIDTH)),
            _const_spec((S5_WIDTH, 2 * g * n)),
            tab, tab, tab, tab, tab, tab,
            _const_spec((g * n, S5_WIDTH)),
            _const_spec((g * n, S5_WIDTH)),
            _const_spec((1, S5_WIDTH)),
            _const_spec((rows, rows)),
        ],
        out_specs=pl.BlockSpec((1, rows, S5_WIDTH), lambda i, t: (i, t, 0)),
        out_shape=jax.ShapeDtypeStruct((b, seq, S5_WIDTH), F32),
        scratch_shapes=[pltpu.VMEM((1, g * n), F32), pltpu.VMEM((1, g * n), F32)],
        compiler_params=_params("parallel", "arbitrary"),
        name="s5_mixer",
    )(proj3, bbd, pinv_r, pinv_i, pow_r, pow_i, pow1_r, pow1_i, cbd_r, cbd_i, d_skip.reshape(1, S5_WIDTH), tril)
    return out.reshape(b * seq, S5_WIDTH)


def _combine_kernel(x_ref, ga_ref, gb_ref, gc_ref, o1_ref, o2_ref, o3_ref, l1_ref, l2_ref, l3_ref,
                    ob_ref, s_ref, wa_ref, wb_ref, wcv_ref, wcg_ref, wo_ref, ex_ref, nf_ref,
                    x1_ref, hf_ref):
    l1, l2, l3 = l1_ref[...], l2_ref[...], l3_ref[...]
    m = jnp.maximum(jnp.maximum(l1, l2), l3)
    e1, e2, e3 = jnp.exp(l1 - m), jnp.exp(l2 - m), jnp.exp(l3 - m)
    inv = 1.0 / (e1 + e2 + e3)
    ex = ex_ref[...]

    def expand(wt):
        hi = wt.astype(BF16)
        lo = (wt - hi.astype(F32)).astype(BF16)
        return _dot(hi, ex) + _dot(lo, ex)

    mix = expand(e1 * inv) * o1_ref[...] + expand(e2 * inv) * o2_ref[...] + expand(e3 * inv) * o3_ref[...]
    y_a = _dot(mix.astype(BF16), wa_ref[...])
    y_b = _dot(ob_ref[...].astype(BF16), wb_ref[...])
    sb = s_ref[...].astype(BF16)
    y_c = _dot(sb, wcv_ref[...]) * _sigmoid(_dot(sb, wcg_ref[...]))
    mixed = _sigmoid(ga_ref[...]) * y_a + _sigmoid(gb_ref[...]) * y_b + _sigmoid(gc_ref[...]) * y_c
    x1 = x_ref[...] + _dot(mixed.astype(BF16), wo_ref[...])
    x1_ref[...] = x1
    hf_ref[...] = _rms(x1, nf_ref[...])


def _combine(x2d, proj2, att, ob, s, w_a_out, w_b_out, w_c_val, w_c_gate, w_out, norm_ffn, tm=512):
    t, d = x2d.shape
    tm = min(tm, t)
    (o1, l1), (o2, l2), (o3, l3) = att
    ex = np.zeros((128, ATT_WIDTH), np.float32)
    for h in range(ATT_SLOTS):
        ex[h, h * ATT_HEAD_DIM:(h + 1) * ATT_HEAD_DIM] = 1.0
    row = lambda w: pl.BlockSpec((tm, w), lambda i: (i, 0))
    gate = lambda k: pl.BlockSpec((tm, d), lambda i: (i, OFF_GATES // d + k))
    bf = lambda w: w.astype(BF16)
    return pl.pallas_call(
        _combine_kernel,
        grid=(t // tm,),
        in_specs=[
            row(d), gate(0), gate(1), gate(2),
            row(ATT_WIDTH), row(ATT_WIDTH), row(ATT_WIDTH), row(128), row(128), row(128),
            row(DN_WIDTH), row(S5_WIDTH),
            _const_spec((ATT_WIDTH, d)), _const_spec((DN_WIDTH, d)), _const_spec((S5_WIDTH, d)),
            _const_spec((S5_WIDTH, d)), _const_spec((d, d)), _const_spec((128, ATT_WIDTH)), _const_spec((1, d)),
        ],
        out_specs=[row(d), row(d)],
        out_shape=[jax.ShapeDtypeStruct((t, d), F32), jax.ShapeDtypeStruct((t, d), F32)],
        compiler_params=_params("parallel"),
        name="branch_combine",
    )(x2d, proj2, proj2, proj2, o1, o2, o3, l1, l2, l3, ob, s,
      bf(w_a_out), bf(w_b_out), bf(w_c_val), bf(w_c_gate), bf(w_out), jnp.asarray(ex, BF16), norm_ffn.reshape(1, d))


def _ple_kernel(x_ref, f_ref, pe_ref, pn_ref, wg_ref, pw_ref, o_ref):
    x2 = x_ref[...] + f_ref[...]
    gate = _sigmoid(_dot(_rms(x2, pn_ref[...]).astype(BF16), wg_ref[...]))
    o_ref[...] = x2 + gate * _dot(pe_ref[...].astype(BF16), pw_ref[...])


def _ple(x1, ffn, pe2d, ple_norm, ple_w_gate, ple_w, tm=512):
    t, d = x1.shape
    tm = min(tm, t)
    row = lambda w: pl.BlockSpec((tm, w), lambda i: (i, 0))
    return pl.pallas_call(
        _ple_kernel,
        grid=(t // tm,),
        in_specs=[row(d), row(d), row(PLE_DIM), _const_spec((1, d)), _const_spec((d, d)), _const_spec((PLE_DIM, d))],
        out_specs=row(d),
        out_shape=jax.ShapeDtypeStruct((t, d), F32),
        compiler_params=_params("parallel"),
        name="ple_residual",
    )(x1, ffn, pe2d, ple_norm.reshape(1, d), ple_w_gate.astype(BF16), ple_w.astype(BF16))


PEER_PICKS = PEER_HEADS * PEER_TOPK
PEER_EXPERTS = PEER_KEYS * PEER_KEYS
PEER_HALF = D_MODEL // 2
ROW_SUBLANES = PEER_HALF // 128
TABLE_PAD = 8
ROUTE_ROWS = 128
EXPERT_ROWS = 256
UP_BATCH = 32
DOWN_BATCH = 16


STAIR_COUNTS = tuple(PEER_TOPK // (a + 1) for a in range(PEER_TOPK))
STAIR_ROWS = -(-sum(STAIR_COUNTS) // 8) * 8
POS_PAD = 1 << 20


def _top_rows(work, row_id, k, big):
    vals, ids = [], []
    for _ in range(k):
        m = jnp.max(work, axis=0, keepdims=True)
        pos = jnp.min(jnp.where(work == m, row_id, big), axis=0, keepdims=True)
        vals.append(m)
        ids.append(pos)
        work = jnp.where(row_id == pos, -jnp.inf, work)
    return vals, ids


def _peer_route_kernel(hf_ref, wq_ref, k1_ref, k2_ref, pos_ref, idx_ref, gate_ref):
    rows = hf_ref.shape[0]
    qry = _dot(hf_ref[...].astype(BF16), wq_ref[...]).astype(BF16)
    k1, k2 = k1_ref[...], k2_ref[...]
    key_id = lax.broadcasted_iota(jnp.int32, (PEER_KEYS, rows), 0).astype(F32)
    cand_pos = pos_ref[...]
    pad = STAIR_ROWS - sum(STAIR_COUNTS)
    idx_rows, gate_rows = [], []
    for h in range(PEER_HEADS):
        q1 = qry[:, (2 * h) * PEER_KEY_DIM:(2 * h + 1) * PEER_KEY_DIM]
        q2 = qry[:, (2 * h + 1) * PEER_KEY_DIM:(2 * h + 2) * PEER_KEY_DIM]
        v1, i1 = _top_rows(_dot_nt(k1, q1), key_id, PEER_TOPK, float(PEER_KEYS))
        v2, i2 = _top_rows(_dot_nt(k2, q2), key_id, PEER_TOPK, float(PEER_KEYS))
        v2c = jnp.concatenate(v2, axis=0)
        i2c = jnp.concatenate(i2, axis=0)
        cand_s = jnp.concatenate([v1[a] + v2c[0:n] for a, n in enumerate(STAIR_COUNTS)]
                                 + [jnp.full((pad, rows), -jnp.inf, F32)], axis=0)
        cand_i = jnp.concatenate([i1[a] * PEER_KEYS + i2c[0:n] for a, n in enumerate(STAIR_COUNTS)]
                                 + [jnp.zeros((pad, rows), F32)], axis=0)
        top_s, pos = _top_rows(cand_s, cand_pos, PEER_TOPK, float(POS_PAD))
        e = [jnp.exp(s - top_s[0]) for s in top_s]
        den = e[0]
        for x in e[1:]:
            den = den + x
        inv = 1.0 / den
        for s, p in zip(e, pos):
            gate_rows.append(s * inv)
            idx_rows.append(jnp.sum(jnp.where(cand_pos == p, cand_i, 0.0), axis=0, keepdims=True))
    idx_ref[...] = (jnp.concatenate(idx_rows, axis=0).T * ROW_SUBLANES + TABLE_PAD).astype(jnp.int32)
    gate_ref[...] = jnp.concatenate(gate_rows, axis=0).T


def _peer_route(hf, w_query, keys1, keys2):
    t, d = hf.shape
    rows = min(ROUTE_ROWS, t)
    nq = w_query.shape[1]
    flat_pos = [a * PEER_TOPK + b for a, n in enumerate(STAIR_COUNTS) for b in range(n)]
    flat_pos += [POS_PAD] * (STAIR_ROWS - len(flat_pos))
    cand_pos = jnp.asarray(np.tile(np.asarray(flat_pos, np.float32)[:, None], (1, rows)))
    return pl.pallas_call(
        _peer_route_kernel,
        grid=(t // rows,),
        in_specs=[pl.BlockSpec((rows, d), lambda i: (i, 0)), _const_spec((d, nq)),
                  _const_spec((PEER_KEYS, PEER_KEY_DIM)), _const_spec((PEER_KEYS, PEER_KEY_DIM)),
                  _const_spec((STAIR_ROWS, rows))],
        out_specs=[pl.BlockSpec((rows, PEER_PICKS), lambda i: (i, 0)), pl.BlockSpec((rows, PEER_PICKS), lambda i: (i, 0))],
        out_shape=[jax.ShapeDtypeStruct((t, PEER_PICKS), jnp.int32), jax.ShapeDtypeStruct((t, PEER_PICKS), F32)],
        compiler_params=_params("parallel"),
        name="peer_route",
    )(hf, w_query.astype(BF16), keys1.astype(BF16), keys2.astype(BF16), cand_pos)


PACK_ROWS = 512


def _pack_kernel(t_ref, o_ref):
    x = t_ref[...]
    lo = lax.bitcast_convert_type(x[:, :PEER_HALF].astype(BF16).astype(F32), jnp.uint32) >> 16
    hi = lax.bitcast_convert_type(x[:, PEER_HALF:].astype(BF16).astype(F32), jnp.uint32)
    words = hi | lo
    for c in range(ROW_SUBLANES):
        o_ref[pl.ds(c, t_ref.shape[0], stride=ROW_SUBLANES), :] = words[:, c * 128:(c + 1) * 128]


def _pack_table(table):
    e, d = table.shape
    rows = min(PACK_ROWS, e)
    return pl.pallas_call(
        _pack_kernel,
        grid=(e // rows,),
        in_specs=[pl.BlockSpec((rows, d), lambda i: (i, 0))],
        out_specs=pl.BlockSpec((rows * ROW_SUBLANES, 128), lambda i: (i, 0)),
        out_shape=jax.ShapeDtypeStruct((e * ROW_SUBLANES, 128), jnp.uint32),
        compiler_params=_params("parallel"),
        name="pack_expert_table",
    )(table)


def _unpack_row(words):
    lo = lax.bitcast_convert_type(words << 16, F32)
    hi = lax.bitcast_convert_type(words & jnp.uint32(0xFFFF0000), F32)
    return lo, hi


def _load_table_once(tab_hbm, tab_vmem, sem):
    @pl.when(pl.program_id(0) == 0)
    def _():
        n = tab_hbm.shape[0]
        cp = pltpu.make_async_copy(tab_hbm, tab_vmem.at[pl.ds(TABLE_PAD, n), :], sem)
        cp.start()
        tab_vmem[0:TABLE_PAD, :] = jnp.zeros((TABLE_PAD, 128), jnp.uint32)
        tab_vmem[TABLE_PAD + n:2 * TABLE_PAD + n, :] = jnp.zeros((TABLE_PAD, 128), jnp.uint32)
        cp.wait()


def _stage_ids(idx_hbm, idx_smem, sems, count):
    step = pl.program_id(0)
    slot = step % 2

    def copy(s, buf):
        return pltpu.make_async_copy(idx_hbm.at[pl.ds(s * count, count)],
                                     idx_smem.at[pl.ds(buf * count, count)], sems.at[1 + buf])

    @pl.when(step == 0)
    def _():
        copy(0, 0).start()

    @pl.when(step + 1 < pl.num_programs(0))
    def _():
        copy(step + 1, 1 - slot).start()

    copy(step, slot).wait()
    return slot * count


def _peer_up_kernel(idx_hbm, x_ref, gate_ref, sel_ref, tab_hbm, o_ref, tab_ref, idx_smem, part_ref,
                    xx_ref, sems):
    rows = gate_ref.shape[0]
    _load_table_once(tab_hbm, tab_ref, sems.at[0])
    id0 = _stage_ids(idx_hbm, idx_smem, sems, rows * PEER_PICKS)
    half = PEER_HALF // 128
    batch = part_ref.shape[1] // 8
    r_id =lax.broadcasted_iota(jnp.int32, (8, PEER_PICKS), 0)
    c_id = lax.broadcasted_iota(jnp.int32, (8, PEER_PICKS), 1)
    row_pick = 4 * (r_id % 2) + 2 * ((r_id // 2) % 2) + (r_id // 4)
    diag = (c_id % 8 == row_pick)[None]
    sub_id = lax.broadcasted_iota(jnp.int32, (8, 128), 0)
    low2 = (sub_id % 4) < 2
    even = (sub_id % 2) == 0
    low = sub_id < ROW_SUBLANES
    group = 8
    unroll = 4

    def fold_rows(p0, p1, p2, p3):
        def half_fold(a, b, mask, s):
            x = jnp.where(mask, a, pltpu.roll(b, s, 0))
            y = jnp.where(mask, pltpu.roll(a, 8 - s, 0), b)
            return x + y
        return half_fold(half_fold(p0, p1, low2, 2), half_fold(p2, p3, low2, 2), even, 1)

    def token_batch(b, carry):
        base = pl.multiple_of(b * batch, batch)

        def token_group(tg, c):
            tg8 = pl.multiple_of(tg * group, group)
            id_refs = [idx_smem.at[pl.ds(pl.multiple_of(id0 + (base + tg8 + i) * PEER_PICKS, PEER_PICKS), PEER_PICKS)]
                       for i in range(group)]
            for i in range(group):
                x_lo = x_ref[base + tg8 + i, 0:half, :]
                x_hi = x_ref[base + tg8 + i, half:2 * half, :]
                xx_ref[i, 0:8, :] = jnp.concatenate([x_lo, x_lo], axis=0)
                xx_ref[i, 8:16, :] = jnp.concatenate([x_hi, x_hi], axis=0)

            def pick_step(k, cc):
                for u in range(unroll):
                    ku = k * unroll + u
                    k8 = pl.multiple_of(ku * 8, 8)
                    o = [k8 + j for j in range(8)]
                    dst = part_ref.at[ku, pl.ds(pl.multiple_of(tg8 * 8, 8 * group), 8 * group), :]
                    for i in range(group):
                        x_lo2 = xx_ref[i, 0:8, :]
                        x_hi2 = xx_ref[i, 8:16, :]
                        tiles = []
                        for j in range(0, 8, 2):
                            ra, rb = id_refs[i][o[j]], id_refs[i][o[j + 1]]
                            words = jnp.where(low, tab_ref[pl.ds(ra, 8), :], tab_ref[pl.ds(rb - ROW_SUBLANES, 8), :])
                            lo, hi = _unpack_row(words)
                            tiles.append(lo * x_lo2 + hi * x_hi2)
                        dst[i * 8:(i + 1) * 8, :] = fold_rows(*tiles)
                return cc

            lax.fori_loop(0, PEER_PICKS // 8 // unroll, pick_step, 0)
            return c

        lax.fori_loop(0, batch // group, token_group, 0)
        sums = jnp.zeros((batch * 8, PEER_PICKS), F32)
        for k in range(PEER_PICKS // 8):
            part = part_ref[k]
            sums = sums + _dot(part.astype(BF16), sel_ref[k])
        act = jnp.sum(jnp.where(diag, sums.reshape(batch, 8, PEER_PICKS), 0.0), axis=1)
        o_ref[pl.ds(base, batch), :] = gate_ref[pl.ds(base, batch), :] * _gelu_tanh(act)
        return carry

    lax.fori_loop(0, rows // batch, token_batch, 0)


def _peer_down_kernel(idx_hbm, w_ref, ones_ref, tab_hbm, o_ref, tab_ref, idx_smem, splat_ref, sems):
    rows = w_ref.shape[0]
    _load_table_once(tab_hbm, tab_ref, sems.at[0])
    id0 = _stage_ids(idx_hbm, idx_smem, sems, rows * PEER_PICKS)
    half = PEER_HALF // 128
    batch = splat_ref.shape[0]
    group = 4
    unroll = 4
    low = lax.broadcasted_iota(jnp.int32, (8, 128), 0) < ROW_SUBLANES
    eye =(lax.broadcasted_iota(jnp.int32, (PEER_PICKS, PEER_PICKS), 0)
           == lax.broadcasted_iota(jnp.int32, (PEER_PICKS, PEER_PICKS), 1)).astype(F32)
    ones = ones_ref[...]

    def token_batch(b, carry):
        base = pl.multiple_of(b * batch, batch)
        for i in range(batch):
            d = eye * w_ref[pl.ds(base + i, 1), :]
            splat_ref[i] = _dot(d.astype(BF16), ones)

        def token_group(tg, c):
            tg8 = pl.multiple_of(tg * group, group)
            id_refs = [idx_smem.at[pl.ds(pl.multiple_of(id0 + (base + tg8 + i) * PEER_PICKS, PEER_PICKS), PEER_PICKS)]
                       for i in range(group)]

            def pick_step(k, acc):
                acc = list(acc)
                for u in range(unroll):
                    k8 = pl.multiple_of((k * unroll + u) * 8, 8)
                    o = [k8 + j for j in range(8)]
                    sp = splat_ref.at[pl.ds(tg8, group), pl.ds(k8, 8), :]
                    for i in range(group):
                        for j in range(0, 8, 2):
                            ra, rb = id_refs[i][o[j]], id_refs[i][o[j + 1]]
                            words = jnp.where(low, tab_ref[pl.ds(ra, 8), :], tab_ref[pl.ds(rb - ROW_SUBLANES, 8), :])
                            lo, hi = _unpack_row(words)
                            w = jnp.where(low, jnp.broadcast_to(sp[i, j:j + 1, :], (8, 128)),
                                          jnp.broadcast_to(sp[i, j + 1:j + 2, :], (8, 128)))
                            acc[2 * i] = acc[2 * i] + w * lo
                            acc[2 * i + 1] = acc[2 * i + 1] + w * hi
                return tuple(acc)

            zero = jnp.zeros((8, 128), F32)
            acc = lax.fori_loop(0, PEER_PICKS // 8 // unroll, pick_step, (zero,) * (2 * group))
            for i in range(group):
                o_ref[base + tg8 + i, 0:half, :] = acc[2 * i][0:half] + acc[2 * i][half:2 * half]
                o_ref[base + tg8 + i, half:2 * half, :] = acc[2 * i + 1][0:half] + acc[2 * i + 1][half:2 * half]
            return c

        lax.fori_loop(0, batch // group, token_group, 0)
        return carry

    lax.fori_loop(0, rows // batch, token_batch, 0)


def _peer_experts(hf, idx, gate, expert_u, expert_v):
    t, d = hf.shape
    rows = min(EXPERT_ROWS, t)
    sub = d // 128
    x3 = hf.reshape(t, sub, 128)
    sel = np.zeros((PEER_PICKS // 8, 128, PEER_PICKS), np.float32)
    for j in range(PEER_PICKS // 8):
        sel[j, :, 8 * j:8 * j + 8] = 1.0
    sel = jnp.asarray(sel, BF16)
    tab_shape = (PEER_EXPERTS * ROW_SUBLANES + 2 * TABLE_PAD, 128)
    picks = pl.BlockSpec((rows, PEER_PICKS), lambda i: (i, 0))
    hbm = pl.BlockSpec(memory_space=pl.ANY)
    tok3 = pl.BlockSpec((rows, sub, 128), lambda i: (i, 0, 0))
    idx_flat = idx.reshape(t * PEER_PICKS)
    ids_smem = pltpu.SMEM((2 * rows * PEER_PICKS,), jnp.int32)
    w = pl.pallas_call(
        _peer_up_kernel,
        grid=(t // rows,),
        in_specs=[hbm, tok3, picks, _const_spec(sel.shape), hbm],
        out_specs=picks,
        out_shape=jax.ShapeDtypeStruct((t, PEER_PICKS), F32),
        scratch_shapes=[pltpu.VMEM(tab_shape, jnp.uint32), ids_smem,
                        pltpu.VMEM((PEER_PICKS // 8, UP_BATCH * 8, 128), F32),
                        pltpu.VMEM((8, 16, 128), F32), pltpu.SemaphoreType.DMA((3,))],
        compiler_params=_params("arbitrary"),
        name="peer_up",
    )(idx_flat, x3, gate, sel, _pack_table(expert_u))
    out = pl.pallas_call(
        _peer_down_kernel,
        grid=(t // rows,),
        in_specs=[hbm, picks, _const_spec((PEER_PICKS, 128)), hbm],
        out_specs=tok3,
        out_shape=jax.ShapeDtypeStruct((t, sub, 128), F32),
        scratch_shapes=[pltpu.VMEM(tab_shape, jnp.uint32), ids_smem,
                        pltpu.VMEM((DOWN_BATCH, PEER_PICKS, 128), F32), pltpu.SemaphoreType.DMA((3,))],
        compiler_params=_params("arbitrary"),
        name="peer_down",
    )(idx_flat, w, jnp.ones((PEER_PICKS, 128), BF16), _pack_table(expert_v))
    return out.reshape(t, d)


def _peer(hf, w_query, keys1, keys2, expert_u, expert_v):
    idx, gate = _peer_route(hf, w_query, keys1, keys2)
    return _peer_experts(hf, idx, gate, expert_u, expert_v)


def _pad_w_in(w_in):
    d = w_in.shape[0]
    outs = [jnp.zeros((d, PROJ_COLS), BF16), jnp.zeros((d, ATT_COLS), BF16)]
    src = 0
    for size, (arr, dst) in zip(SRC_SIZES, SRC_DST):
        outs[arr] = outs[arr].at[:, dst:dst + size].set(w_in[:, src:src + size].astype(BF16))
        src += size
    return outs


def _layer(x2d, pe2d, b, seq, norm_mix, w_in, att_q_gain, att_k_gain, w_a_out, dn_conv, dn_a_log, dn_dt_bias,
           dn_out_gain, w_b_out, s5_a_re, s5_a_im, s5_log_dt, s5_b_re, s5_b_im, s5_c_re, s5_c_im, s5_d,
           w_c_val, w_c_gate, w_out, norm_ffn, peer_w_query, peer_keys1, peer_keys2, peer_u, peer_v,
           ple_w, ple_norm, ple_w_gate):
    w_main, w_att = _pad_w_in(w_in)
    proj2 = _norm_matmul(x2d, norm_mix, w_main, tn=512)
    proj3 = proj2.reshape(b, seq, PROJ_COLS)
    proj_att = _norm_matmul(x2d, norm_mix, w_att, tn=ATT_COLS).reshape(b, seq, ATT_COLS)
    att = [_attention_group(proj_att, att_q_gain, att_k_gain, g) for g in range(len(ATT_GROUPS))]
    ob = _deltanet(proj3, dn_conv, dn_a_log, dn_dt_bias, dn_out_gain)
    s = _s5(proj3, s5_a_re, s5_a_im, s5_log_dt, s5_b_re, s5_b_im, s5_c_re, s5_c_im, s5_d)
    x1, hf = _combine(x2d, proj2, att, ob, s, w_a_out, w_b_out, w_c_val, w_c_gate, w_out, norm_ffn)
    ffn = _peer(hf, peer_w_query, peer_keys1, peer_keys2, peer_u, peer_v)
    return _ple(x1, ffn, pe2d, ple_norm, ple_w_gate, ple_w)


def kernel(x, p, norm_mix, w_in, att_q_gain, att_k_gain, w_a_out, dn_conv, dn_a_log, dn_dt_bias, dn_out_gain, w_b_out, s5_a_re, s5_a_im, s5_log_dt, s5_b_re, s5_b_im, s5_c_re, s5_c_im, s5_d, w_c_val, w_c_gate, w_out, norm_ffn, peer_w_query, peer_keys1, peer_keys2, peer_u, peer_v, ple_w, ple_norm, ple_w_gate):
    b, seq, d = x.shape
    depth = p.shape[0]
    x2d = x.reshape(b * seq, d)
    for i in range(depth):
        x2d = _layer(x2d, p[i].reshape(b * seq, PLE_DIM), b, seq, norm_mix[i], w_in[i], att_q_gain[i], att_k_gain[i],
                     w_a_out[i], dn_conv[i], dn_a_log[i], dn_dt_bias[i], dn_out_gain[i], w_b_out[i], s5_a_re[i],
                     s5_a_im[i], s5_log_dt[i], s5_b_re[i], s5_b_im[i], s5_c_re[i], s5_c_im[i], s5_d[i], w_c_val[i],
                     w_c_gate[i], w_out[i], norm_ffn[i], peer_w_query[i], peer_keys1[i], peer_keys2[i], peer_u[i],
                     peer_v[i], ple_w[i], ple_norm[i], ple_w_gate[i])
    return x2d.reshape(b, seq, d)
```

```python
import functools

import numpy as np
import jax
import jax.numpy as jnp
from jax import lax
from jax.experimental import pallas as pl
from jax.experimental.pallas import tpu as pltpu

F32 = jnp.float32
BF16 = jnp.bfloat16
HI = lax.Precision.HIGHEST

D_MODEL = 1024
NORM_EPS = 1e-6
ATT_GROUPS = ((128, 1), (512, 4), (2048, 16))
ATT_SLOTS = 4
ATT_HEAD_DIM = 64
ATT_WIDTH = 256
ATT_BLOCK = 128
ALIBI_MAX_EXP = 8.0
DN_HEADS = 4
DN_HEAD_DIM = 128
DN_WIDTH = 512
DN_CONV = 4
DN_CHUNK = 64
S5_GROUP = 16
S5_GROUPS = 16
S5_WIDTH = 256
S5_STATE = 64
PEER_HEADS = 8
PEER_KEYS = 128
PEER_TOPK = 16
PEER_KEY_DIM = 128
PEER_BLOCK = 128
PLE_DIM = 256

OFF_QKV, OFF_Z, OFF_GATES, OFF_U, OFF_BA = 0, 1536, 2048, 5120, 5376
PROJ_COLS = 5632
OFF_QA, OFF_KA, OFF_VA = 0, 768, 1024
ATT_COLS = 1280
SRC_SIZES = (768, 256, 256, 1536, 512, 4, 4, 256, 3072)
SRC_DST = ((1, OFF_QA), (1, OFF_KA), (1, OFF_VA), (0, OFF_QKV), (0, OFF_Z), (0, OFF_BA), (0, OFF_BA + 4),
           (0, OFF_U), (0, OFF_GATES))

VMEM_LIMIT = 56 * 1024 * 1024
NEG_BIG = -1e30


def _params(*sem):
    return pltpu.CompilerParams(dimension_semantics=sem, vmem_limit_bytes=VMEM_LIMIT)


def _const_spec(shape):
    nd = len(shape)
    return pl.BlockSpec(shape, lambda *_: (0,) * nd)


def _rms(x, gain):
    return x * lax.rsqrt(jnp.mean(x * x, axis=-1, keepdims=True) + NORM_EPS) * gain


def _sigmoid(x):
    return 1.0 / (1.0 + jnp.exp(-x))


def _gelu_tanh(x):
    return 0.5 * x * (1.0 + jnp.tanh(0.7978845608028654 * (x + 0.044715 * (x * x * x))))


def _dot(a, b, **kw):
    return jnp.dot(a, b, preferred_element_type=F32, **kw)


def _dot3(a, b):
    a_hi, b_hi = a.astype(BF16), b.astype(BF16)
    a_lo = (a - a_hi.astype(F32)).astype(BF16)
    b_lo = (b - b_hi.astype(F32)).astype(BF16)
    return _dot(a_hi, b_hi) + (_dot(a_hi, b_lo) + _dot(a_lo, b_hi))


def _dot_nt(a, b):
    return lax.dot_general(a, b, (((1,), (1,)), ((), ())), preferred_element_type=F32)


def _dot_tn(a, b):
    return lax.dot_general(a, b, (((0,), (0,)), ((), ())), preferred_element_type=F32)


def _norm_matmul_kernel(x_ref, g_ref, w_ref, o_ref, h_ref):
    @pl.when(pl.program_id(1) == 0)
    def _():
        h_ref[...] = _rms(x_ref[...], g_ref[...]).astype(BF16)

    o_ref[...] = _dot(h_ref[...], w_ref[...])


def _norm_matmul(x2d, gain, w_bf16, tm=1024, tn=1024):
    t, d = x2d.shape
    n = w_bf16.shape[1]
    tm = min(tm, t)
    tn = min(tn, n)
    return pl.pallas_call(
        _norm_matmul_kernel,
        grid=(t // tm, n // tn),
        in_specs=[
            pl.BlockSpec((tm, d), lambda i, j: (i, 0)),
            pl.BlockSpec((1, d), lambda i, j: (0, 0)),
            pl.BlockSpec((d, tn), lambda i, j: (0, j)),
        ],
        out_specs=pl.BlockSpec((tm, tn), lambda i, j: (i, j)),
        out_shape=jax.ShapeDtypeStruct((t, n), F32),
        scratch_shapes=[pltpu.VMEM((tm, d), BF16)],
        compiler_params=_params("parallel", "arbitrary"),
        name="norm_matmul",
    )(x2d, gain.reshape(1, d), w_bf16)


def _attn_kernel(q_ref, kp_ref, kc_ref, vp_ref, vc_ref, qg_ref, kg_ref, bd_ref, o_ref, lse_ref, *,
                 dilation, span, slopes):
    n = pl.program_id(2)
    bd = bd_ref[...]

    def head_norm(x, gain):
        x2 = x * x
        hi = x2.astype(BF16)
        lo = (x2 - hi.astype(F32)).astype(BF16)
        ss = _dot(hi, bd) + _dot(lo, bd)
        return x * lax.rsqrt(ss * (1.0 / ATT_HEAD_DIM) + NORM_EPS) * gain

    qpos = lax.broadcasted_iota(jnp.int32, (ATT_BLOCK, 2 * ATT_BLOCK), 0)
    kpos = lax.broadcasted_iota(jnp.int32, (ATT_BLOCK, 2 * ATT_BLOCK), 1)
    rel = qpos + ATT_BLOCK - kpos
    valid = (rel >= 0) & (rel <= span) & ((kpos >= ATT_BLOCK) | (n > 0))
    dist = (rel * dilation).astype(F32)
    lane = lax.broadcasted_iota(jnp.int32, (ATT_BLOCK, 128), 1)
    col_head = lax.broadcasted_iota(jnp.int32, (ATT_BLOCK, ATT_WIDTH), 1) // ATT_HEAD_DIM
    heads = range(ATT_SLOTS)

    q = head_norm(q_ref[0], qg_ref[...]) * (ATT_HEAD_DIM ** -0.5)
    k = head_norm(jnp.concatenate([kp_ref[0], kc_ref[0]], axis=0), kg_ref[...])
    v = jnp.concatenate([vp_ref[0], vc_ref[0]], axis=0)
    kb, vb = k.astype(BF16), v.astype(BF16)
    s = [_dot_nt(jnp.where(col_head == h, q, 0.0).astype(BF16), kb) for h in heads]
    s = [jnp.where(valid, s[h] - slopes[h] * dist, NEG_BIG) for h in heads]
    m = [jnp.max(s[h], axis=-1, keepdims=True) for h in heads]
    e = [jnp.exp(s[h] - m[h]) for h in heads]
    den = [jnp.sum(e[h], axis=-1, keepdims=True) for h in heads]
    pv = [_dot(e[h].astype(BF16), vb) for h in heads]
    out = jnp.zeros((ATT_BLOCK, ATT_WIDTH), F32)
    lse_all = jnp.zeros((ATT_BLOCK, 128), F32)
    for h in heads:
        out = jnp.where(col_head == h, pv[h] / den[h], out)
        lse_all = jnp.where(lane == h, m[h] + jnp.log(den[h]), lse_all)
    o_ref[0] = out
    lse_ref[0] = lse_all


def _attention_group(proj3, q_gain, k_gain, group):
    b, seq, _ = proj3.shape
    window, dilation = ATT_GROUPS[group]
    span = window // dilation
    sub = seq // dilation
    nb = sub // ATT_BLOCK
    view = proj3.reshape(b, sub, dilation * ATT_COLS)
    cb = ATT_COLS // ATT_WIDTH
    qc, kc, vc = OFF_QA // ATT_WIDTH + group, OFF_KA // ATT_WIDTH, OFF_VA // ATT_WIDTH
    n_heads = len(ATT_GROUPS) * ATT_SLOTS
    slopes = 2.0 ** (-ALIBI_MAX_EXP * np.arange(1, n_heads + 1) / n_heads)
    slopes = tuple(float(s) for s in slopes.reshape(len(ATT_GROUPS), ATT_SLOTS)[group])
    blk = (1, ATT_BLOCK, ATT_WIDTH)
    bd = (np.arange(ATT_WIDTH)[:, None] // ATT_HEAD_DIM == np.arange(ATT_WIDTH)[None, :] // ATT_HEAD_DIM)
    o, lse = pl.pallas_call(
        functools.partial(_attn_kernel, dilation=dilation, span=span, slopes=slopes),
        grid=(b, dilation, nb),
        in_specs=[
            pl.BlockSpec(blk, lambda i, r, n: (i, n, r * cb + qc)),
            pl.BlockSpec(blk, lambda i, r, n: (i, jnp.maximum(n - 1, 0), r * cb + kc)),
            pl.BlockSpec(blk, lambda i, r, n: (i, n, r * cb + kc)),
            pl.BlockSpec(blk, lambda i, r, n: (i, jnp.maximum(n - 1, 0), r * cb + vc)),
            pl.BlockSpec(blk, lambda i, r, n: (i, n, r * cb + vc)),
            _const_spec((1, ATT_WIDTH)),
            _const_spec((1, ATT_WIDTH)),
            _const_spec((ATT_WIDTH, ATT_WIDTH)),
        ],
        out_specs=[
            pl.BlockSpec(blk, lambda i, r, n: (i, n, r)),
            pl.BlockSpec((1, ATT_BLOCK, 128), lambda i, r, n: (i, n, r)),
        ],
        out_shape=[
            jax.ShapeDtypeStruct((b, sub, dilation * ATT_WIDTH), F32),
            jax.ShapeDtypeStruct((b, sub, dilation * 128), F32),
        ],
        compiler_params=_params("parallel", "parallel", "arbitrary"),
        name=f"dilated_attention_g{group}",
    )(view, view, view, view, view,
      jnp.tile(q_gain, ATT_SLOTS).reshape(1, ATT_WIDTH), jnp.tile(k_gain, ATT_SLOTS).reshape(1, ATT_WIDTH),
      jnp.asarray(bd, BF16))
    return o.reshape(b * seq, ATT_WIDTH), lse.reshape(b * seq, 128)


DN_ROWS = 256


def _deltanet_kernel(qkv_ref, z_ref, ba_ref, cw_ref, alog_ref, dtb_ref, og_ref, tril_ref, o_ref,
                     state_ref, tail_ref):
    c = DN_CHUNK
    rows = qkv_ref.shape[1]

    @pl.when(pl.program_id(1) == 0)
    def _():
        state_ref[...] = jnp.zeros_like(state_ref)
        tail_ref[...] = jnp.zeros_like(tail_ref)

    x = qkv_ref[0]
    xc = jnp.concatenate([tail_ref[...], x], axis=0)
    cw = cw_ref[...]
    acc = cw[DN_CONV - 1:DN_CONV] * x
    for j in range(DN_CONV - 1):
        acc = acc + cw[j:j + 1] * xc[5 + j:5 + j + rows]
    tail_ref[...] = x[rows - 8:rows]
    a = acc * _sigmoid(acc)

    ba = ba_ref[0]
    beta_all = _sigmoid(ba)
    sp = ba + dtb_ref[...]
    sp = jnp.maximum(sp, 0.0) + jnp.log(1.0 + jnp.exp(-jnp.abs(sp)))
    g_all = -(jnp.exp(alog_ref[...]) * sp)
    tril = tril_ref[...]
    ii = lax.broadcasted_iota(jnp.int32, (c, c), 0)
    jj = lax.broadcasted_iota(jnp.int32, (c, c), 1)
    incl = ii >= jj
    strict = ii > jj
    eye = (ii == jj).astype(F32)
    z = z_ref[0]
    og = og_ref[...]

    n_chunks = rows // c
    units = [(h, ci) for ci in range(n_chunks) for h in range(DN_HEADS)]

    loc = {}
    for h in range(DN_HEADS):
        hs = slice(h * DN_HEAD_DIM, (h + 1) * DN_HEAD_DIM)
        qh = a[:, hs]
        kh = a[:, DN_WIDTH + h * DN_HEAD_DIM:DN_WIDTH + (h + 1) * DN_HEAD_DIM]
        vh = a[:, 2 * DN_WIDTH + h * DN_HEAD_DIM:2 * DN_WIDTH + (h + 1) * DN_HEAD_DIM]
        qh = qh * lax.rsqrt(jnp.sum(qh * qh, axis=-1, keepdims=True) + NORM_EPS) * (DN_HEAD_DIM ** -0.5)
        kh = kh * lax.rsqrt(jnp.sum(kh * kh, axis=-1, keepdims=True) + NORM_EPS)
        beta = beta_all[:, h:h + 1]
        gh = jnp.broadcast_to(g_all[:, DN_HEADS + h:DN_HEADS + h + 1], (rows, DN_HEAD_DIM))
        for ci in range(n_chunks):
            rs = slice(ci * c, (ci + 1) * c)
            gc = _dot(tril, gh[rs], precision=HI)
            gc64 = gc[:, :c]
            diff = gc64 - gc64.T
            decay = jnp.where(incl, jnp.exp(jnp.where(incl, diff, 0.0)), 0.0)
            eg = jnp.exp(gc)
            g_last = gc[c - 1:c, :]
            kc_, qc_, vc_, bc_ = kh[rs], qh[rs], vh[rs], beta[rs]
            kbeta = kc_ * bc_
            kcb = kc_.astype(BF16)
            loc[h, ci] = dict(
                lower=jnp.where(strict, _dot_nt(kbeta.astype(BF16), kcb) * decay, 0.0),
                vb=(vc_ * bc_).astype(BF16), kbe=(kbeta * eg).astype(BF16),
                attn=jnp.where(incl, _dot_nt(qc_.astype(BF16), kcb) * decay, 0.0).astype(BF16),
                q_dec=(qc_ * eg).astype(BF16), k_dec=(kc_ * jnp.exp(g_last - gc)).astype(BF16),
                g_end=jnp.exp(g_last))

    t_inv = {k: eye - loc[k]["lower"] for k in units}
    pw = {k: _dot3(loc[k]["lower"], loc[k]["lower"]) for k in units}
    for it in range(5):
        t_inv = {k: t_inv[k] + _dot3(t_inv[k], pw[k]) for k in units}
        if it < 4:
            pw = {k: _dot3(pw[k], pw[k]) for k in units}
    uw = {}
    for k in units:
        t_b = t_inv[k].astype(BF16)
        uw[k] = (_dot(t_b, loc[k]["vb"]), _dot(t_b, loc[k]["kbe"]).astype(BF16))

    state = [state_ref[h] for h in range(DN_HEADS)]
    o_chunks = [[] for _ in range(DN_HEADS)]
    for ci in range(n_chunks):
        for h in range(DN_HEADS):
            d = loc[h, ci]
            u, w = uw[h, ci]
            sb = state[h].astype(BF16)
            vnb = (u - _dot(w, sb)).astype(BF16)
            o_chunks[h].append(_dot(d["q_dec"], sb) + _dot(d["attn"], vnb))
            state[h] = state[h] * d["g_end"] + _dot_tn(d["k_dec"], vnb)
    outs = []
    for h in range(DN_HEADS):
        state_ref[h] = state[h]
        zh = z[:, h * DN_HEAD_DIM:(h + 1) * DN_HEAD_DIM]
        outs.append(_rms(jnp.concatenate(o_chunks[h], axis=0), og) * (zh * _sigmoid(zh)))
    o_ref[0] = jnp.concatenate(outs, axis=-1)


def _deltanet(proj3, conv_w, a_log, dt_bias, out_gain):
    b, seq, _ = proj3.shape
    rows = min(DN_ROWS, seq)
    pad_a = jnp.zeros((1, 128), F32).at[0, DN_HEADS:2 * DN_HEADS].set(a_log)
    pad_d = jnp.zeros((1, 128), F32).at[0, DN_HEADS:2 * DN_HEADS].set(dt_bias)
    tril = jnp.asarray(np.tril(np.ones((DN_CHUNK, DN_CHUNK), np.float32)))
    out = pl.pallas_call(
        _deltanet_kernel,
        grid=(b, seq // rows),
        in_specs=[
            pl.BlockSpec((1, rows, 3 * DN_WIDTH), lambda i, t: (i, t, OFF_QKV // (3 * DN_WIDTH))),
            pl.BlockSpec((1, rows, DN_WIDTH), lambda i, t: (i, t, OFF_Z // DN_WIDTH)),
            pl.BlockSpec((1, rows, 128), lambda i, t: (i, t, OFF_BA // 128)),
            _const_spec((DN_CONV, 3 * DN_WIDTH)),
            _const_spec((1, 128)),
            _const_spec((1, 128)),
            _const_spec((1, DN_HEAD_DIM)),
            _const_spec((DN_CHUNK, DN_CHUNK)),
        ],
        out_specs=pl.BlockSpec((1, rows, DN_WIDTH), lambda i, t: (i, t, 0)),
        out_shape=jax.ShapeDtypeStruct((b, seq, DN_WIDTH), F32),
        scratch_shapes=[pltpu.VMEM((DN_HEADS, DN_HEAD_DIM, DN_HEAD_DIM), F32), pltpu.VMEM((8, 3 * DN_WIDTH), F32)],
        compiler_params=_params("parallel", "arbitrary"),
        name="gated_deltanet",
    )(proj3, proj3, proj3, conv_w, pad_a, pad_d, out_gain.reshape(1, DN_HEAD_DIM), tril)
    return out.reshape(b * seq, DN_WIDTH)


S5_ROWS = 128
S5_N = S5_GROUPS * S5_STATE


def _s5_kernel(u_ref, bbd_ref, pinv_r_ref, pinv_i_ref, pow_r_ref, pow_i_ref, pow1_r_ref, pow1_i_ref,
               cr_ref, ci_ref, d_ref, tril_ref, o_ref, sr_ref, si_ref):
    @pl.when(pl.program_id(1) == 0)
    def _():
        sr_ref[...] = jnp.zeros_like(sr_ref)
        si_ref[...] = jnp.zeros_like(si_ref)

    u = u_ref[0]
    bu = _dot(u.astype(BF16), bbd_ref[...])
    br, bi = bu[:, :S5_N], bu[:, S5_N:]
    pr, pi = pinv_r_ref[...], pinv_i_ref[...]
    zr = pr * br - pi * bi
    zi = pr * bi + pi * br
    tril = tril_ref[...]

    def cumsum_rows(zz):
        hi = zz.astype(BF16)
        lo = (zz - hi.astype(F32)).astype(BF16)
        return _dot(tril, hi) + _dot(tril, lo)

    cr = cumsum_rows(zr)
    ci = cumsum_rows(zi)
    wr, wi = pow_r_ref[...], pow_i_ref[...]
    p1r, p1i = pow1_r_ref[...], pow1_i_ref[...]
    sr, si = sr_ref[...], si_ref[...]
    xr = wr * cr - wi * ci + (p1r * sr - p1i * si)
    xi = wr * ci + wi * cr + (p1r * si + p1i * sr)
    rows = xr.shape[0]
    sr_ref[...] = xr[rows - 1:rows]
    si_ref[...] = xi[rows - 1:rows]
    y = _dot(xr.astype(BF16), cr_ref[...]) - _dot(xi.astype(BF16), ci_ref[...]) + d_ref[...] * u
    o_ref[0] = _gelu_tanh(y)


def _s5(proj3, a_re, a_im, log_dt, b_re, b_im, c_re, c_im, d_skip):
    b, seq, _ = proj3.shape
    rows = min(S5_ROWS, seq)
    g, n, p = S5_GROUPS, S5_STATE, S5_GROUP
    dt = jnp.exp(log_dt)[:, None]
    lr, li = a_re, a_im
    mag = jnp.exp(lr * dt)
    ab_r, ab_i = mag * jnp.cos(li * dt), mag * jnp.sin(li * dt)
    den = lr * lr + li * li
    nr, ni = ab_r - 1.0, ab_i
    cr, ci = (nr * lr + ni * li) / den, (ni * lr - nr * li) / den
    bb_r = cr[..., None] * b_re - ci[..., None] * b_im
    bb_i = cr[..., None] * b_im + ci[..., None] * b_re
    eye_g = jnp.eye(g, dtype=F32)
    bbd_r = jnp.einsum('gnp,gh->gphn', bb_r, eye_g).reshape(g * p, g * n)
    bbd_i = jnp.einsum('gnp,gh->gphn', bb_i, eye_g).reshape(g * p, g * n)
    bbd = jnp.concatenate([bbd_r, bbd_i], axis=1).astype(BF16)
    cbd_r = jnp.einsum('gpn,gh->gnhp', c_re, eye_g).reshape(g * n, g * p).astype(BF16)
    cbd_i = jnp.einsum('gpn,gh->gnhp', c_im, eye_g).reshape(g * n, g * p).astype(BF16)
    steps = jnp.arange(rows, dtype=F32)[:, None]
    lrd = (lr * dt).reshape(1, g * n)
    lid = (li * dt).reshape(1, g * n)

    def power(k):
        m = jnp.exp(lrd * k)
        return m * jnp.cos(lid * k), m * jnp.sin(lid * k)

    pinv_r, pinv_i = power(-steps)
    pow_r, pow_i = power(steps)
    pow1_r, pow1_i = power(steps + 1.0)
    tril = jnp.asarray(np.tril(np.ones((rows, rows), np.float32)), BF16)
    tab = _const_spec((rows, g * n))
    out = pl.pallas_call(
        _s5_kernel,
        grid=(b, seq // rows),
        in_specs=[
            pl.BlockSpec((1, rows, S5_WIDTH), lambda i, t: (i, t, OFF_U // S5_WIDTH)),
            _const_spec((S5_WIDTH, 2 * g * n)),
            tab, tab, tab, tab, tab, tab,
            _const_spec((g * n, S5_WIDTH)),
            _const_spec((g * n, S5_WIDTH)),
            _const_spec((1, S5_WIDTH)),
            _const_spec((rows, rows)),
        ],
        out_specs=pl.BlockSpec((1, rows, S5_WIDTH), lambda i, t: (i, t, 0)),
        out_shape=jax.ShapeDtypeStruct((b, seq, S5_WIDTH), F32),
        scratch_shapes=[pltpu.VMEM((1, g * n), F32), pltpu.VMEM((1, g * n), F32)],
        compiler_params=_params("parallel", "arbitrary"),
        name="s5_mixer",
    )(proj3, bbd, pinv_r, pinv_i, pow_r, pow_i, pow1_r, pow1_i, cbd_r, cbd_i, d_skip.reshape(1, S5_WIDTH), tril)
    return out.reshape(b * seq, S5_WIDTH)


def _combine_kernel(x_ref, ga_ref, gb_ref, gc_ref, o1_ref, o2_ref, o3_ref, l1_ref, l2_ref, l3_ref,
                    ob_ref, s_ref, wa_ref, wb_ref, wcv_ref, wcg_ref, wo_ref, ex_ref, nf_ref,
                    x1_ref, hf_ref):
    l1, l2, l3 = l1_ref[...], l2_ref[...], l3_ref[...]
    m = jnp.maximum(jnp.maximum(l1, l2), l3)
    e1, e2, e3 = jnp.exp(l1 - m), jnp.exp(l2 - m), jnp.exp(l3 - m)
    inv = 1.0 / (e1 + e2 + e3)
    ex = ex_ref[...]

    def expand(wt):
        hi = wt.astype(BF16)
        lo = (wt - hi.astype(F32)).astype(BF16)
        return _dot(hi, ex) + _dot(lo, ex)

    mix = expand(e1 * inv) * o1_ref[...] + expand(e2 * inv) * o2_ref[...] + expand(e3 * inv) * o3_ref[...]
    y_a = _dot(mix.astype(BF16), wa_ref[...])
    y_b = _dot(ob_ref[...].astype(BF16), wb_ref[...])
    sb = s_ref[...].astype(BF16)
    y_c = _dot(sb, wcv_ref[...]) * _sigmoid(_dot(sb, wcg_ref[...]))
    mixed = _sigmoid(ga_ref[...]) * y_a + _sigmoid(gb_ref[...]) * y_b + _sigmoid(gc_ref[...]) * y_c
    x1 = x_ref[...] + _dot(mixed.astype(BF16), wo_ref[...])
    x1_ref[...] = x1
    hf_ref[...] = _rms(x1, nf_ref[...])


def _combine(x2d, proj2, att, ob, s, w_a_out, w_b_out, w_c_val, w_c_gate, w_out, norm_ffn, tm=512):
    t, d = x2d.shape
    tm = min(tm, t)
    (o1, l1), (o2, l2), (o3, l3) = att
    ex = np.zeros((128, ATT_WIDTH), np.float32)
    for h in range(ATT_SLOTS):
        ex[h, h * ATT_HEAD_DIM:(h + 1) * ATT_HEAD_DIM] = 1.0
    row = lambda w: pl.BlockSpec((tm, w), lambda i: (i, 0))
    gate = lambda k: pl.BlockSpec((tm, d), lambda i: (i, OFF_GATES // d + k))
    bf = lambda w: w.astype(BF16)
    return pl.pallas_call(
        _combine_kernel,
        grid=(t // tm,),
        in_specs=[
            row(d), gate(0), gate(1), gate(2),
            row(ATT_WIDTH), row(ATT_WIDTH), row(ATT_WIDTH), row(128), row(128), row(128),
            row(DN_WIDTH), row(S5_WIDTH),
            _const_spec((ATT_WIDTH, d)), _const_spec((DN_WIDTH, d)), _const_spec((S5_WIDTH, d)),
            _const_spec((S5_WIDTH, d)), _const_spec((d, d)), _const_spec((128, ATT_WIDTH)), _const_spec((1, d)),
        ],
        out_specs=[row(d), row(d)],
        out_shape=[jax.ShapeDtypeStruct((t, d), F32), jax.ShapeDtypeStruct((t, d), F32)],
        compiler_params=_params("parallel"),
        name="branch_combine",
    )(x2d, proj2, proj2, proj2, o1, o2, o3, l1, l2, l3, ob, s,
      bf(w_a_out), bf(w_b_out), bf(w_c_val), bf(w_c_gate), bf(w_out), jnp.asarray(ex, BF16), norm_ffn.reshape(1, d))


def _ple_kernel(x_ref, f_ref, pe_ref, pn_ref, wg_ref, pw_ref, o_ref):
    x2 = x_ref[...] + f_ref[...]
    gate = _sigmoid(_dot(_rms(x2, pn_ref[...]).astype(BF16), wg_ref[...]))
    o_ref[...] = x2 + gate * _dot(pe_ref[...].astype(BF16), pw_ref[...])


def _ple(x1, ffn, pe2d, ple_norm, ple_w_gate, ple_w, tm=512):
    t, d = x1.shape
    tm = min(tm, t)
    row = lambda w: pl.BlockSpec((tm, w), lambda i: (i, 0))
    return pl.pallas_call(
        _ple_kernel,
        grid=(t // tm,),
        in_specs=[row(d), row(d), row(PLE_DIM), _const_spec((1, d)), _const_spec((d, d)), _const_spec((PLE_DIM, d))],
        out_specs=row(d),
        out_shape=jax.ShapeDtypeStruct((t, d), F32),
        compiler_params=_params("parallel"),
        name="ple_residual",
    )(x1, ffn, pe2d, ple_norm.reshape(1, d), ple_w_gate.astype(BF16), ple_w.astype(BF16))


PEER_PICKS = PEER_HEADS * PEER_TOPK
PEER_EXPERTS = PEER_KEYS * PEER_KEYS
PEER_HALF = D_MODEL // 2
ROW_SUBLANES = PEER_HALF // 128
TABLE_PAD = 8
ROUTE_ROWS = 128
EXPERT_ROWS = 256
UP_BATCH = 32
DOWN_BATCH = 16


STAIR_COUNTS = tuple(PEER_TOPK // (a + 1) for a in range(PEER_TOPK))
STAIR_ROWS = -(-sum(STAIR_COUNTS) // 8) * 8
POS_PAD = 1 << 20


def _top_rows(work, row_id, k, big):
    vals, ids = [], []
    for _ in range(k):
        m = jnp.max(work, axis=0, keepdims=True)
        pos = jnp.min(jnp.where(work == m, row_id, big), axis=0, keepdims=True)
        vals.append(m)
        ids.append(pos)
        work = jnp.where(row_id == pos, -jnp.inf, work)
    return vals, ids


def _peer_route_kernel(hf_ref, wq_ref, k1_ref, k2_ref, pos_ref, idx_ref, gate_ref):
    rows = hf_ref.shape[0]
    qry = _dot(hf_ref[...].astype(BF16), wq_ref[...]).astype(BF16)
    k1, k2 = k1_ref[...], k2_ref[...]
    key_id = lax.broadcasted_iota(jnp.int32, (PEER_KEYS, rows), 0).astype(F32)
    cand_pos = pos_ref[...]
    pad = STAIR_ROWS - sum(STAIR_COUNTS)
    idx_rows, gate_rows = [], []
    for h in range(PEER_HEADS):
        q1 = qry[:, (2 * h) * PEER_KEY_DIM:(2 * h + 1) * PEER_KEY_DIM]
        q2 = qry[:, (2 * h + 1) * PEER_KEY_DIM:(2 * h + 2) * PEER_KEY_DIM]
        v1, i1 = _top_rows(_dot_nt(k1, q1), key_id, PEER_TOPK, float(PEER_KEYS))
        v2, i2 = _top_rows(_dot_nt(k2, q2), key_id, PEER_TOPK, float(PEER_KEYS))
        v2c = jnp.concatenate(v2, axis=0)
        i2c = jnp.concatenate(i2, axis=0)
        cand_s = jnp.concatenate([v1[a] + v2c[0:n] for a, n in enumerate(STAIR_COUNTS)]
                                 + [jnp.full((pad, rows), -jnp.inf, F32)], axis=0)
        cand_i = jnp.concatenate([i1[a] * PEER_KEYS + i2c[0:n] for a, n in enumerate(STAIR_COUNTS)]
                                 + [jnp.zeros((pad, rows), F32)], axis=0)
        top_s, pos = _top_rows(cand_s, cand_pos, PEER_TOPK, float(POS_PAD))
        e = [jnp.exp(s - top_s[0]) for s in top_s]
        den = e[0]
        for x in e[1:]:
            den = den + x
        inv = 1.0 / den
        for s, p in zip(e, pos):
            gate_rows.append(s * inv)
            idx_rows.append(jnp.sum(jnp.where(cand_pos == p, cand_i, 0.0), axis=0, keepdims=True))
    idx_ref[...] = (jnp.concatenate(idx_rows, axis=0).T * ROW_SUBLANES + TABLE_PAD).astype(jnp.int32)
    gate_ref[...] = jnp.concatenate(gate_rows, axis=0).T


def _peer_route(hf, w_query, keys1, keys2):
    t, d = hf.shape
    rows = min(ROUTE_ROWS, t)
    nq = w_query.shape[1]
    flat_pos = [a * PEER_TOPK + b for a, n in enumerate(STAIR_COUNTS) for b in range(n)]
    flat_pos += [POS_PAD] * (STAIR_ROWS - len(flat_pos))
    cand_pos = jnp.asarray(np.tile(np.asarray(flat_pos, np.float32)[:, None], (1, rows)))
    return pl.pallas_call(
        _peer_route_kernel,
        grid=(t // rows,),
        in_specs=[pl.BlockSpec((rows, d), lambda i: (i, 0)), _const_spec((d, nq)),
                  _const_spec((PEER_KEYS, PEER_KEY_DIM)), _const_spec((PEER_KEYS, PEER_KEY_DIM)),
                  _const_spec((STAIR_ROWS, rows))],
        out_specs=[pl.BlockSpec((rows, PEER_PICKS), lambda i: (i, 0)), pl.BlockSpec((rows, PEER_PICKS), lambda i: (i, 0))],
        out_shape=[jax.ShapeDtypeStruct((t, PEER_PICKS), jnp.int32), jax.ShapeDtypeStruct((t, PEER_PICKS), F32)],
        compiler_params=_params("parallel"),
        name="peer_route",
    )(hf, w_query.astype(BF16), keys1.astype(BF16), keys2.astype(BF16), cand_pos)


PACK_ROWS = 512


def _pack_kernel(t_ref, o_ref):
    x = t_ref[0]
    lo = lax.bitcast_convert_type(x[:, :PEER_HALF].astype(BF16).astype(F32), jnp.uint32) >> 16
    hi = lax.bitcast_convert_type(x[:, PEER_HALF:].astype(BF16).astype(F32), jnp.uint32)
    words = hi | lo
    for c in range(ROW_SUBLANES):
        o_ref[pl.ds(c, x.shape[0], stride=ROW_SUBLANES), :] = words[:, c * 128:(c + 1) * 128]


def _pack_table(tables, layer):
    _, e, d = tables.shape
    rows = min(PACK_ROWS, e)
    return pl.pallas_call(
        _pack_kernel,
        grid=(e // rows,),
        in_specs=[pl.BlockSpec((1, rows, d), lambda i: (layer, i, 0))],
        out_specs=pl.BlockSpec((rows * ROW_SUBLANES, 128), lambda i: (i, 0)),
        out_shape=jax.ShapeDtypeStruct((e * ROW_SUBLANES, 128), jnp.uint32),
        compiler_params=_params("parallel"),
        name="pack_expert_table",
    )(tables)


def _unpack_row(words):
    lo = lax.bitcast_convert_type(words << 16, F32)
    hi = lax.bitcast_convert_type(words & jnp.uint32(0xFFFF0000), F32)
    return lo, hi


def _load_table_once(tab_hbm, tab_vmem, sem):
    @pl.when(pl.program_id(0) == 0)
    def _():
        n = tab_hbm.shape[0]
        cp = pltpu.make_async_copy(tab_hbm, tab_vmem.at[pl.ds(TABLE_PAD, n), :], sem)
        cp.start()
        tab_vmem[0:TABLE_PAD, :] = jnp.zeros((TABLE_PAD, 128), jnp.uint32)
        tab_vmem[TABLE_PAD + n:2 * TABLE_PAD + n, :] = jnp.zeros((TABLE_PAD, 128), jnp.uint32)
        cp.wait()


def _stage_ids(idx_hbm, idx_smem, sems, count):
    step = pl.program_id(0)
    slot = step % 2

    def copy(s, buf):
        return pltpu.make_async_copy(idx_hbm.at[pl.ds(s * count, count)],
                                     idx_smem.at[pl.ds(buf * count, count)], sems.at[1 + buf])

    @pl.when(step == 0)
    def _():
        copy(0, 0).start()

    @pl.when(step + 1 < pl.num_programs(0))
    def _():
        copy(step + 1, 1 - slot).start()

    copy(step, slot).wait()
    return slot * count


def _peer_up_kernel(idx_hbm, x_ref, gate_ref, sel_ref, tab_hbm, o_ref, tab_ref, idx_smem, part_ref,
                    xx_ref, sems):
    rows = gate_ref.shape[0]
    _load_table_once(tab_hbm, tab_ref, sems.at[0])
    id0 = _stage_ids(idx_hbm, idx_smem, sems, rows * PEER_PICKS)
    half = PEER_HALF // 128
    batch = part_ref.shape[1] // 8
    r_id =lax.broadcasted_iota(jnp.int32, (8, PEER_PICKS), 0)
    c_id = lax.broadcasted_iota(jnp.int32, (8, PEER_PICKS), 1)
    row_pick = 4 * (r_id % 2) + 2 * ((r_id // 2) % 2) + (r_id // 4)
    diag = (c_id % 8 == row_pick)[None]
    sub_id = lax.broadcasted_iota(jnp.int32, (8, 128), 0)
    low2 = (sub_id % 4) < 2
    even = (sub_id % 2) == 0
    low = sub_id < ROW_SUBLANES
    group = 8
    unroll = 4

    def fold_rows(p0, p1, p2, p3):
        def half_fold(a, b, mask, s):
            x = jnp.where(mask, a, pltpu.roll(b, s, 0))
            y = jnp.where(mask, pltpu.roll(a, 8 - s, 0), b)
            return x + y
        return half_fold(half_fold(p0, p1, low2, 2), half_fold(p2, p3, low2, 2), even, 1)

    def token_batch(b, carry):
        base = pl.multiple_of(b * batch, batch)

        def token_group(tg, c):
            tg8 = pl.multiple_of(tg * group, group)
            id_refs = [idx_smem.at[pl.ds(pl.multiple_of(id0 + (base + tg8 + i) * PEER_PICKS, PEER_PICKS), PEER_PICKS)]
                       for i in range(group)]
            for i in range(group):
                x_lo = x_ref[base + tg8 + i, 0:half, :]
                x_hi = x_ref[base + tg8 + i, half:2 * half, :]
                xx_ref[i, 0:8, :] = jnp.concatenate([x_lo, x_lo], axis=0)
                xx_ref[i, 8:16, :] = jnp.concatenate([x_hi, x_hi], axis=0)

            def pick_step(k, cc):
                for u in range(unroll):
                    ku = k * unroll + u
                    k8 = pl.multiple_of(ku * 8, 8)
                    o = [k8 + j for j in range(8)]
                    dst = part_ref.at[ku, pl.ds(pl.multiple_of(tg8 * 8, 8 * group), 8 * group), :]
                    for i in range(group):
                        x_lo2 = xx_ref[i, 0:8, :]
                        x_hi2 = xx_ref[i, 8:16, :]
                        tiles = []
                        for j in range(0, 8, 2):
                            ra, rb = id_refs[i][o[j]], id_refs[i][o[j + 1]]
                            words = jnp.where(low, tab_ref[pl.ds(ra, 8), :], tab_ref[pl.ds(rb - ROW_SUBLANES, 8), :])
                            lo, hi = _unpack_row(words)
                            tiles.append(lo * x_lo2 + hi * x_hi2)
                        dst[i * 8:(i + 1) * 8, :] = fold_rows(*tiles)
                return cc

            lax.fori_loop(0, PEER_PICKS // 8 // unroll, pick_step, 0)
            return c

        lax.fori_loop(0, batch // group, token_group, 0)
        sums = jnp.zeros((batch * 8, PEER_PICKS), F32)
        for k in range(PEER_PICKS // 8):
            part = part_ref[k]
            sums = sums + _dot(part.astype(BF16), sel_ref[k])
        act = jnp.sum(jnp.where(diag, sums.reshape(batch, 8, PEER_PICKS), 0.0), axis=1)
        o_ref[pl.ds(base, batch), :] = gate_ref[pl.ds(base, batch), :] * _gelu_tanh(act)
        return carry

    lax.fori_loop(0, rows // batch, token_batch, 0)


def _peer_down_kernel(idx_hbm, w_ref, ones_ref, tab_hbm, o_ref, tab_ref, idx_smem, splat_ref, sems):
    rows = w_ref.shape[0]
    _load_table_once(tab_hbm, tab_ref, sems.at[0])
    id0 = _stage_ids(idx_hbm, idx_smem, sems, rows * PEER_PICKS)
    half = PEER_HALF // 128
    batch = splat_ref.shape[0]
    group = 4
    unroll = 4
    low = lax.broadcasted_iota(jnp.int32, (8, 128), 0) < ROW_SUBLANES
    eye =(lax.broadcasted_iota(jnp.int32, (PEER_PICKS, PEER_PICKS), 0)
           == lax.broadcasted_iota(jnp.int32, (PEER_PICKS, PEER_PICKS), 1)).astype(F32)
    ones = ones_ref[...]

    def token_batch(b, carry):
        base = pl.multiple_of(b * batch, batch)
        for i in range(batch):
            d = eye * w_ref[pl.ds(base + i, 1), :]
            splat_ref[i] = _dot(d.astype(BF16), ones)

        def token_group(tg, c):
            tg8 = pl.multiple_of(tg * group, group)
            id_refs = [idx_smem.at[pl.ds(pl.multiple_of(id0 + (base + tg8 + i) * PEER_PICKS, PEER_PICKS), PEER_PICKS)]
                       for i in range(group)]

            def pick_step(k, acc):
                acc = list(acc)
                for u in range(unroll):
                    k8 = pl.multiple_of((k * unroll + u) * 8, 8)
                    o = [k8 + j for j in range(8)]
                    sp = splat_ref.at[pl.ds(tg8, group), pl.ds(k8, 8), :]
                    for i in range(group):
                        for j in range(0, 8, 2):
                            ra, rb = id_refs[i][o[j]], id_refs[i][o[j + 1]]
                            words = jnp.where(low, tab_ref[pl.ds(ra, 8), :], tab_ref[pl.ds(rb - ROW_SUBLANES, 8), :])
                            lo, hi = _unpack_row(words)
                            w = jnp.where(low, jnp.broadcast_to(sp[i, j:j + 1, :], (8, 128)),
                                          jnp.broadcast_to(sp[i, j + 1:j + 2, :], (8, 128)))
                            acc[2 * i] = acc[2 * i] + w * lo
                            acc[2 * i + 1] = acc[2 * i + 1] + w * hi
                return tuple(acc)

            zero = jnp.zeros((8, 128), F32)
            acc = lax.fori_loop(0, PEER_PICKS // 8 // unroll, pick_step, (zero,) * (2 * group))
            for i in range(group):
                o_ref[base + tg8 + i, 0:half, :] = acc[2 * i][0:half] + acc[2 * i][half:2 * half]
                o_ref[base + tg8 + i, half:2 * half, :] = acc[2 * i + 1][0:half] + acc[2 * i + 1][half:2 * half]
            return c

        lax.fori_loop(0, batch // group, token_group, 0)
        return carry

    lax.fori_loop(0, rows // batch, token_batch, 0)


def _peer_experts(hf, idx, gate, expert_u, expert_v, layer):
    t, d = hf.shape
    rows = min(EXPERT_ROWS, t)
    sel = np.zeros((PEER_PICKS // 8, 128, PEER_PICKS), np.float32)
    for j in range(PEER_PICKS // 8):
        sel[j, :, 8 * j:8 * j + 8] = 1.0
    sel = jnp.asarray(sel, BF16)
    tab_shape = (PEER_EXPERTS * ROW_SUBLANES + 2 * TABLE_PAD, 128)
    picks = pl.BlockSpec((rows, PEER_PICKS), lambda i: (i, 0))
    hbm = pl.BlockSpec(memory_space=pl.ANY)
    tok3 = pl.BlockSpec((rows, d // 128, 128), lambda i: (i, 0, 0))
    idx_flat = idx.reshape(t * PEER_PICKS)
    ids_smem = pltpu.SMEM((2 * rows * PEER_PICKS,), jnp.int32)
    w = pl.pallas_call(
        _peer_up_kernel,
        grid=(t // rows,),
        in_specs=[hbm, tok3, picks, _const_spec(sel.shape), hbm],
        out_specs=picks,
        out_shape=jax.ShapeDtypeStruct((t, PEER_PICKS), F32),
        scratch_shapes=[pltpu.VMEM(tab_shape, jnp.uint32), ids_smem,
                        pltpu.VMEM((PEER_PICKS // 8, UP_BATCH * 8, 128), F32),
                        pltpu.VMEM((8, 16, 128), F32), pltpu.SemaphoreType.DMA((3,))],
        compiler_params=_params("arbitrary"),
        name="peer_up",
    )(idx_flat, hf.reshape(t, d // 128, 128), gate, sel, _pack_table(expert_u, layer))
    out = pl.pallas_call(
        _peer_down_kernel,
        grid=(t // rows,),
        in_specs=[hbm, picks, _const_spec((PEER_PICKS, 128)), hbm],
        out_specs=tok3,
        out_shape=jax.ShapeDtypeStruct((t, d // 128, 128), F32),
        scratch_shapes=[pltpu.VMEM(tab_shape, jnp.uint32), ids_smem,
                        pltpu.VMEM((DOWN_BATCH, PEER_PICKS, 128), F32), pltpu.SemaphoreType.DMA((3,))],
        compiler_params=_params("arbitrary"),
        name="peer_down",
    )(idx_flat, w, jnp.ones((PEER_PICKS, 128), BF16), _pack_table(expert_v, layer))
    return out.reshape(t, d)


def _peer(hf, w_query, keys1, keys2, expert_u, expert_v, layer):
    idx, gate = _peer_route(hf, w_query, keys1, keys2)
    return _peer_experts(hf, idx, gate, expert_u, expert_v, layer)


def _pad_w_in(w_in):
    d = w_in.shape[0]
    outs = [jnp.zeros((d, PROJ_COLS), BF16), jnp.zeros((d, ATT_COLS), BF16)]
    src = 0
    for size, (arr, dst) in zip(SRC_SIZES, SRC_DST):
        outs[arr] = outs[arr].at[:, dst:dst + size].set(w_in[:, src:src + size].astype(BF16))
        src += size
    return outs


def _layer(x2d, pe2d, b, seq, norm_mix, w_in, att_q_gain, att_k_gain, w_a_out, dn_conv, dn_a_log, dn_dt_bias,
           dn_out_gain, w_b_out, s5_a_re, s5_a_im, s5_log_dt, s5_b_re, s5_b_im, s5_c_re, s5_c_im, s5_d,
           w_c_val, w_c_gate, w_out, norm_ffn, peer_w_query, peer_keys1, peer_keys2, peer_u, peer_v,
           ple_w, ple_norm, ple_w_gate, layer):
    w_main, w_att = _pad_w_in(w_in)
    proj2 = _norm_matmul(x2d, norm_mix, w_main, tn=512)
    proj3 = proj2.reshape(b, seq, PROJ_COLS)
    proj_att = _norm_matmul(x2d, norm_mix, w_att, tn=ATT_COLS).reshape(b, seq, ATT_COLS)
    att = [_attention_group(proj_att, att_q_gain, att_k_gain, g) for g in range(len(ATT_GROUPS))]
    ob = _deltanet(proj3, dn_conv, dn_a_log, dn_dt_bias, dn_out_gain)
    s = _s5(proj3, s5_a_re, s5_a_im, s5_log_dt, s5_b_re, s5_b_im, s5_c_re, s5_c_im, s5_d)
    x1, hf = _combine(x2d, proj2, att, ob, s, w_a_out, w_b_out, w_c_val, w_c_gate, w_out, norm_ffn)
    ffn = _peer(hf, peer_w_query, peer_keys1, peer_keys2, peer_u, peer_v, layer)
    return _ple(x1, ffn, pe2d, ple_norm, ple_w_gate, ple_w)


def kernel(x, p, norm_mix, w_in, att_q_gain, att_k_gain, w_a_out, dn_conv, dn_a_log, dn_dt_bias, dn_out_gain, w_b_out, s5_a_re, s5_a_im, s5_log_dt, s5_b_re, s5_b_im, s5_c_re, s5_c_im, s5_d, w_c_val, w_c_gate, w_out, norm_ffn, peer_w_query, peer_keys1, peer_keys2, peer_u, peer_v, ple_w, ple_norm, ple_w_gate):
    b, seq, d = x.shape
    depth = p.shape[0]
    x2d = x.reshape(b * seq, d)
    for i in range(depth):
        x2d = _layer(x2d, p[i].reshape(b * seq, PLE_DIM), b, seq, norm_mix[i], w_in[i], att_q_gain[i], att_k_gain[i],
                     w_a_out[i], dn_conv[i], dn_a_log[i], dn_dt_bias[i], dn_out_gain[i], w_b_out[i], s5_a_re[i],
                     s5_a_im[i], s5_log_dt[i], s5_b_re[i], s5_b_im[i], s5_c_re[i], s5_c_im[i], s5_d[i], w_c_val[i],
                     w_c_gate[i], w_out[i], norm_ffn[i], peer_w_query[i], peer_keys1[i], peer_keys2[i], peer_u,
                     peer_v, ple_w[i], ple_norm[i], ple_w_gate[i], layer=i)
    return x2d.reshape(b, seq, d)
```

```python
import functools

import numpy as np
import jax
import jax.numpy as jnp
from jax import lax
from jax.experimental import pallas as pl
from jax.experimental.pallas import tpu as pltpu

F32 = jnp.float32
BF16 = jnp.bfloat16

D_MODEL = 1024
NORM_EPS = 1e-6
ATT_GROUPS = ((128, 1), (512, 4), (2048, 16))
ATT_SLOTS = 4
ATT_HEAD_DIM = 64
ATT_WIDTH = 256
ATT_BLOCK = 128
ALIBI_MAX_EXP = 8.0
DN_HEADS = 4
DN_HEAD_DIM = 128
DN_WIDTH = 512
DN_CONV = 4
DN_CHUNK = 64
S5_GROUP = 16
S5_GROUPS = 16
S5_WIDTH = 256
S5_STATE = 64
PEER_HEADS = 8
PEER_KEYS = 128
PEER_TOPK = 16
PEER_KEY_DIM = 128
PLE_DIM = 256

OFF_QKV, OFF_Z, OFF_GATES, OFF_U, OFF_BA = 0, 1536, 2048, 5120, 5376
PROJ_COLS = 5632
OFF_QA, OFF_KA, OFF_VA = 0, 768, 1024
ATT_COLS = 1280
SRC_SIZES = (768, 256, 256, 1536, 512, 4, 4, 256, 3072)
SRC_DST = ((1, OFF_QA), (1, OFF_KA), (1, OFF_VA), (0, OFF_QKV), (0, OFF_Z), (0, OFF_BA), (0, OFF_BA + 4),
           (0, OFF_U), (0, OFF_GATES))

VMEM_LIMIT = 56 * 1024 * 1024
NEG_BIG = -1e30


def _params(*sem):
    return pltpu.CompilerParams(dimension_semantics=sem, vmem_limit_bytes=VMEM_LIMIT)


def _const_spec(shape):
    nd = len(shape)
    return pl.BlockSpec(shape, lambda *_: (0,) * nd)


def _rms(x, gain):
    return x * lax.rsqrt(jnp.mean(x * x, axis=-1, keepdims=True) + NORM_EPS) * gain


def _sigmoid(x):
    return 1.0 / (1.0 + jnp.exp(-x))


def _gelu_tanh(x):
    return 0.5 * x * (1.0 + jnp.tanh(0.7978845608028654 * (x + 0.044715 * (x * x * x))))


def _dot(a, b, **kw):
    return jnp.dot(a, b, preferred_element_type=F32, **kw)


def _dot3(a, b):
    a_hi, b_hi = a.astype(BF16), b.astype(BF16)
    a_lo = (a - a_hi.astype(F32)).astype(BF16)
    b_lo = (b - b_hi.astype(F32)).astype(BF16)
    return _dot(a_hi, b_hi) + (_dot(a_hi, b_lo) + _dot(a_lo, b_hi))


def _dot_nt(a, b):
    return lax.dot_general(a, b, (((1,), (1,)), ((), ())), preferred_element_type=F32)


def _dot_tn(a, b):
    return lax.dot_general(a, b, (((0,), (0,)), ((), ())), preferred_element_type=F32)


def _norm_matmul_kernel(x_ref, g_ref, w_ref, o_ref, h_ref):
    @pl.when(pl.program_id(1) == 0)
    def _():
        h_ref[...] = _rms(x_ref[...], g_ref[...]).astype(BF16)

    o_ref[...] = _dot(h_ref[...], w_ref[...])


def _norm_matmul(x2d, gain, w_bf16, tm=1024, tn=1024):
    t, d = x2d.shape
    n = w_bf16.shape[1]
    tm = min(tm, t)
    tn = min(tn, n)
    return pl.pallas_call(
        _norm_matmul_kernel,
        grid=(t // tm, n // tn),
        in_specs=[
            pl.BlockSpec((tm, d), lambda i, j: (i, 0)),
            pl.BlockSpec((1, d), lambda i, j: (0, 0)),
            pl.BlockSpec((d, tn), lambda i, j: (0, j)),
        ],
        out_specs=pl.BlockSpec((tm, tn), lambda i, j: (i, j)),
        out_shape=jax.ShapeDtypeStruct((t, n), F32),
        scratch_shapes=[pltpu.VMEM((tm, d), BF16)],
        compiler_params=_params("parallel", "arbitrary"),
        name="norm_matmul",
    )(x2d, gain.reshape(1, d), w_bf16)


def _attn_kernel(q_ref, kp_ref, kc_ref, vp_ref, vc_ref, qg_ref, kg_ref, bd_ref, o_ref, lse_ref, *,
                 dilation, span, slopes):
    n = pl.program_id(2)
    bd = bd_ref[...]

    def head_norm(x, gain):
        x2 = x * x
        hi = x2.astype(BF16)
        lo = (x2 - hi.astype(F32)).astype(BF16)
        ss = _dot(hi, bd) + _dot(lo, bd)
        return x * lax.rsqrt(ss * (1.0 / ATT_HEAD_DIM) + NORM_EPS) * gain

    qpos = lax.broadcasted_iota(jnp.int32, (ATT_BLOCK, 2 * ATT_BLOCK), 0)
    kpos = lax.broadcasted_iota(jnp.int32, (ATT_BLOCK, 2 * ATT_BLOCK), 1)
    rel = qpos + ATT_BLOCK - kpos
    valid = (rel >= 0) & (rel <= span) & ((kpos >= ATT_BLOCK) | (n > 0))
    dist = (rel * dilation).astype(F32)
    lane = lax.broadcasted_iota(jnp.int32, (ATT_BLOCK, 128), 1)
    col_head = lax.broadcasted_iota(jnp.int32, (ATT_BLOCK, ATT_WIDTH), 1) // ATT_HEAD_DIM
    heads = range(ATT_SLOTS)

    q = head_norm(q_ref[0], qg_ref[...]) * (ATT_HEAD_DIM ** -0.5)
    k = head_norm(jnp.concatenate([kp_ref[0], kc_ref[0]], axis=0), kg_ref[...])
    v = jnp.concatenate([vp_ref[0], vc_ref[0]], axis=0)
    kb, vb = k.astype(BF16), v.astype(BF16)
    s = [_dot_nt(jnp.where(col_head == h, q, 0.0).astype(BF16), kb) for h in heads]
    s = [jnp.where(valid, s[h] - slopes[h] * dist, NEG_BIG) for h in heads]
    m = [jnp.max(s[h], axis=-1, keepdims=True) for h in heads]
    e = [jnp.exp(s[h] - m[h]) for h in heads]
    den = [jnp.sum(e[h], axis=-1, keepdims=True) for h in heads]
    pv = [_dot(e[h].astype(BF16), vb) for h in heads]
    out = jnp.zeros((ATT_BLOCK, ATT_WIDTH), F32)
    lse_all = jnp.zeros((ATT_BLOCK, 128), F32)
    for h in heads:
        out = jnp.where(col_head == h, pv[h] / den[h], out)
        lse_all = jnp.where(lane == h, m[h] + jnp.log(den[h]), lse_all)
    o_ref[0] = out
    lse_ref[0] = lse_all


def _attention_group(proj3, q_gain, k_gain, group):
    b, seq, _ = proj3.shape
    window, dilation = ATT_GROUPS[group]
    span = window // dilation
    sub = seq // dilation
    nb = sub // ATT_BLOCK
    view = proj3.reshape(b, sub, dilation * ATT_COLS)
    cb = ATT_COLS // ATT_WIDTH
    qc, kc, vc = OFF_QA // ATT_WIDTH + group, OFF_KA // ATT_WIDTH, OFF_VA // ATT_WIDTH
    n_heads = len(ATT_GROUPS) * ATT_SLOTS
    slopes = 2.0 ** (-ALIBI_MAX_EXP * np.arange(1, n_heads + 1) / n_heads)
    slopes = tuple(float(s) for s in slopes.reshape(len(ATT_GROUPS), ATT_SLOTS)[group])
    blk = (1, ATT_BLOCK, ATT_WIDTH)
    bd = (np.arange(ATT_WIDTH)[:, None] // ATT_HEAD_DIM == np.arange(ATT_WIDTH)[None, :] // ATT_HEAD_DIM)
    o, lse = pl.pallas_call(
        functools.partial(_attn_kernel, dilation=dilation, span=span, slopes=slopes),
        grid=(b, dilation, nb),
        in_specs=[
            pl.BlockSpec(blk, lambda i, r, n: (i, n, r * cb + qc)),
            pl.BlockSpec(blk, lambda i, r, n: (i, jnp.maximum(n - 1, 0), r * cb + kc)),
            pl.BlockSpec(blk, lambda i, r, n: (i, n, r * cb + kc)),
            pl.BlockSpec(blk, lambda i, r, n: (i, jnp.maximum(n - 1, 0), r * cb + vc)),
            pl.BlockSpec(blk, lambda i, r, n: (i, n, r * cb + vc)),
            _const_spec((1, ATT_WIDTH)),
            _const_spec((1, ATT_WIDTH)),
            _const_spec((ATT_WIDTH, ATT_WIDTH)),
        ],
        out_specs=[
            pl.BlockSpec(blk, lambda i, r, n: (i, n, r)),
            pl.BlockSpec((1, ATT_BLOCK, 128), lambda i, r, n: (i, n, r)),
        ],
        out_shape=[
            jax.ShapeDtypeStruct((b, sub, dilation * ATT_WIDTH), F32),
            jax.ShapeDtypeStruct((b, sub, dilation * 128), F32),
        ],
        compiler_params=_params("parallel", "parallel", "arbitrary"),
        name=f"dilated_attention_g{group}",
    )(view, view, view, view, view,
      jnp.tile(q_gain, ATT_SLOTS).reshape(1, ATT_WIDTH), jnp.tile(k_gain, ATT_SLOTS).reshape(1, ATT_WIDTH),
      jnp.asarray(bd, BF16))
    return o.reshape(b * seq, ATT_WIDTH), lse.reshape(b * seq, 128)


DN_ROWS = 256


def _deltanet_kernel(qkv_ref, z_ref, ba_ref, cw_ref, alog_ref, dtb_ref, og_ref, tril_ref, o_ref,
                     state_ref, tail_ref):
    c = DN_CHUNK
    rows = qkv_ref.shape[1]

    @pl.when(pl.program_id(1) == 0)
    def _():
        state_ref[...] = jnp.zeros_like(state_ref)
        tail_ref[...] = jnp.zeros_like(tail_ref)

    x = qkv_ref[0]
    xc = jnp.concatenate([tail_ref[...], x], axis=0)
    cw = cw_ref[...]
    acc = cw[DN_CONV - 1:DN_CONV] * x
    for j in range(DN_CONV - 1):
        acc = acc + cw[j:j + 1] * xc[5 + j:5 + j + rows]
    tail_ref[...] = x[rows - 8:rows]
    a = acc * _sigmoid(acc)

    ba = ba_ref[0]
    beta_all = _sigmoid(ba)
    sp = ba + dtb_ref[...]
    sp = jnp.maximum(sp, 0.0) + jnp.log(1.0 + jnp.exp(-jnp.abs(sp)))
    g_all = -(jnp.exp(alog_ref[...]) * sp)
    tril = tril_ref[...]
    ii = lax.broadcasted_iota(jnp.int32, (c, c), 0)
    jj = lax.broadcasted_iota(jnp.int32, (c, c), 1)
    incl = ii >= jj
    strict = ii > jj
    eye = (ii == jj).astype(F32)
    z = z_ref[0]
    og = og_ref[...]

    n_chunks = rows // c
    units = [(h, ci) for ci in range(n_chunks) for h in range(DN_HEADS)]

    loc = {}
    for h in range(DN_HEADS):
        hs = slice(h * DN_HEAD_DIM, (h + 1) * DN_HEAD_DIM)
        qh = a[:, hs]
        kh = a[:, DN_WIDTH + h * DN_HEAD_DIM:DN_WIDTH + (h + 1) * DN_HEAD_DIM]
        vh = a[:, 2 * DN_WIDTH + h * DN_HEAD_DIM:2 * DN_WIDTH + (h + 1) * DN_HEAD_DIM]
        qh = qh * lax.rsqrt(jnp.sum(qh * qh, axis=-1, keepdims=True) + NORM_EPS) * (DN_HEAD_DIM ** -0.5)
        kh = kh * lax.rsqrt(jnp.sum(kh * kh, axis=-1, keepdims=True) + NORM_EPS)
        beta = beta_all[:, h:h + 1]
        gh = jnp.broadcast_to(g_all[:, DN_HEADS + h:DN_HEADS + h + 1], (rows, DN_HEAD_DIM))
        for ci in range(n_chunks):
            rs = slice(ci * c, (ci + 1) * c)
            g_hi = gh[rs].astype(BF16)
            g_lo = (gh[rs] - g_hi.astype(F32)).astype(BF16)
            gc = _dot(tril, g_hi) + _dot(tril, g_lo)
            gc64 = gc[:, :c]
            diff = gc64 - gc64.T
            decay = jnp.where(incl, jnp.exp(jnp.where(incl, diff, 0.0)), 0.0)
            eg = jnp.exp(gc)
            g_last = gc[c - 1:c, :]
            kc_, qc_, vc_, bc_ = kh[rs], qh[rs], vh[rs], beta[rs]
            kbeta = kc_ * bc_
            kcb = kc_.astype(BF16)
            loc[h, ci] = dict(
                lower=jnp.where(strict, _dot_nt(kbeta.astype(BF16), kcb) * decay, 0.0),
                vb=(vc_ * bc_).astype(BF16), kbe=(kbeta * eg).astype(BF16),
                attn=jnp.where(incl, _dot_nt(qc_.astype(BF16), kcb) * decay, 0.0).astype(BF16),
                q_dec=(qc_ * eg).astype(BF16), k_dec=(kc_ * jnp.exp(g_last - gc)).astype(BF16),
                g_end=jnp.exp(g_last))

    t_inv = {k: eye - loc[k]["lower"] for k in units}
    pw = {k: _dot3(loc[k]["lower"], loc[k]["lower"]) for k in units}
    for it in range(5):
        t_inv = {k: t_inv[k] + _dot3(t_inv[k], pw[k]) for k in units}
        if it < 4:
            pw = {k: _dot3(pw[k], pw[k]) for k in units}
    uw = {}
    for k in units:
        t_b = t_inv[k].astype(BF16)
        uw[k] = (_dot(t_b, loc[k]["vb"]), _dot(t_b, loc[k]["kbe"]).astype(BF16))

    state = [state_ref[h] for h in range(DN_HEADS)]
    o_chunks = [[] for _ in range(DN_HEADS)]
    for ci in range(n_chunks):
        for h in range(DN_HEADS):
            d = loc[h, ci]
            u, w = uw[h, ci]
            sb = state[h].astype(BF16)
            vnb = (u - _dot(w, sb)).astype(BF16)
            o_chunks[h].append(_dot(d["q_dec"], sb) + _dot(d["attn"], vnb))
            state[h] = state[h] * d["g_end"] + _dot_tn(d["k_dec"], vnb)
    outs = []
    for h in range(DN_HEADS):
        state_ref[h] = state[h]
        zh = z[:, h * DN_HEAD_DIM:(h + 1) * DN_HEAD_DIM]
        outs.append(_rms(jnp.concatenate(o_chunks[h], axis=0), og) * (zh * _sigmoid(zh)))
    o_ref[0] = jnp.concatenate(outs, axis=-1)


def _deltanet(proj3, conv_w, a_log, dt_bias, out_gain):
    b, seq, _ = proj3.shape
    rows = min(DN_ROWS, seq)
    pad_a = jnp.zeros((1, 128), F32).at[0, DN_HEADS:2 * DN_HEADS].set(a_log)
    pad_d = jnp.zeros((1, 128), F32).at[0, DN_HEADS:2 * DN_HEADS].set(dt_bias)
    tril = jnp.asarray(np.tril(np.ones((DN_CHUNK, DN_CHUNK), np.float32)), BF16)
    out = pl.pallas_call(
        _deltanet_kernel,
        grid=(b, seq // rows),
        in_specs=[
            pl.BlockSpec((1, rows, 3 * DN_WIDTH), lambda i, t: (i, t, OFF_QKV // (3 * DN_WIDTH))),
            pl.BlockSpec((1, rows, DN_WIDTH), lambda i, t: (i, t, OFF_Z // DN_WIDTH)),
            pl.BlockSpec((1, rows, 128), lambda i, t: (i, t, OFF_BA // 128)),
            _const_spec((DN_CONV, 3 * DN_WIDTH)),
            _const_spec((1, 128)),
            _const_spec((1, 128)),
            _const_spec((1, DN_HEAD_DIM)),
            _const_spec((DN_CHUNK, DN_CHUNK)),
        ],
        out_specs=pl.BlockSpec((1, rows, DN_WIDTH), lambda i, t: (i, t, 0)),
        out_shape=jax.ShapeDtypeStruct((b, seq, DN_WIDTH), F32),
        scratch_shapes=[pltpu.VMEM((DN_HEADS, DN_HEAD_DIM, DN_HEAD_DIM), F32), pltpu.VMEM((8, 3 * DN_WIDTH), F32)],
        compiler_params=_params("parallel", "arbitrary"),
        name="gated_deltanet",
    )(proj3, proj3, proj3, conv_w, pad_a, pad_d, out_gain.reshape(1, DN_HEAD_DIM), tril)
    return out.reshape(b * seq, DN_WIDTH)


S5_ROWS = 128
S5_N = S5_GROUPS * S5_STATE


def _s5_kernel(u_ref, bbd_ref, pinv_r_ref, pinv_i_ref, pow_r_ref, pow_i_ref, a1_r_ref, a1_i_ref,
               cr_ref, ci_ref, d_ref, tril_ref, o_ref, sr_ref, si_ref):
    @pl.when(pl.program_id(1) == 0)
    def _():
        sr_ref[...] = jnp.zeros_like(sr_ref)
        si_ref[...] = jnp.zeros_like(si_ref)

    u = u_ref[0]
    bu = _dot(u.astype(BF16), bbd_ref[...])
    br, bi = bu[:, :S5_N], bu[:, S5_N:]
    pr, pi = pinv_r_ref[...], pinv_i_ref[...]
    zr = pr * br - pi * bi
    zi = pr * bi + pi * br
    tril = tril_ref[...]

    def cumsum_rows(zz):
        hi = zz.astype(BF16)
        lo = (zz - hi.astype(F32)).astype(BF16)
        return _dot(tril, hi) + _dot(tril, lo)

    cr = cumsum_rows(zr)
    ci = cumsum_rows(zi)
    wr, wi = pow_r_ref[...], pow_i_ref[...]
    a_r, a_i = a1_r_ref[...], a1_i_ref[...]
    sr, si = sr_ref[...], si_ref[...]
    cr = cr + (a_r * sr - a_i * si)
    ci = ci + (a_r * si + a_i * sr)
    xr = wr * cr - wi * ci
    xi = wr * ci + wi * cr
    rows = xr.shape[0]
    sr_ref[...] = xr[rows - 1:rows]
    si_ref[...] = xi[rows - 1:rows]
    y = _dot(xr.astype(BF16), cr_ref[...]) - _dot(xi.astype(BF16), ci_ref[...]) + d_ref[...] * u
    o_ref[0] = _gelu_tanh(y)


def _s5(proj3, a_re, a_im, log_dt, b_re, b_im, c_re, c_im, d_skip):
    b, seq, _ = proj3.shape
    rows = min(S5_ROWS, seq)
    g, n, p = S5_GROUPS, S5_STATE, S5_GROUP
    dt = jnp.exp(log_dt)[:, None]
    lr, li = a_re, a_im
    mag = jnp.exp(lr * dt)
    ab_r, ab_i = mag * jnp.cos(li * dt), mag * jnp.sin(li * dt)
    den = lr * lr + li * li
    nr, ni = ab_r - 1.0, ab_i
    cr, ci = (nr * lr + ni * li) / den, (ni * lr - nr * li) / den
    bb_r = cr[..., None] * b_re - ci[..., None] * b_im
    bb_i = cr[..., None] * b_im + ci[..., None] * b_re
    eye_g = jnp.eye(g, dtype=F32)
    bbd_r = jnp.einsum('gnp,gh->gphn', bb_r, eye_g).reshape(g * p, g * n)
    bbd_i = jnp.einsum('gnp,gh->gphn', bb_i, eye_g).reshape(g * p, g * n)
    bbd = jnp.concatenate([bbd_r, bbd_i], axis=1).astype(BF16)
    cbd_r = jnp.einsum('gpn,gh->gnhp', c_re, eye_g).reshape(g * n, g * p).astype(BF16)
    cbd_i = jnp.einsum('gpn,gh->gnhp', c_im, eye_g).reshape(g * n, g * p).astype(BF16)
    steps = jnp.arange(rows, dtype=F32)[:, None]
    lrd = (lr * dt).reshape(1, g * n)
    lid = (li * dt).reshape(1, g * n)

    def power(k):
        m = jnp.exp(lrd * k)
        return m * jnp.cos(lid * k), m * jnp.sin(lid * k)

    pinv_r, pinv_i = power(-steps)
    pow_r, pow_i = power(steps)
    a1_r, a1_i = power(1.0)
    tril = jnp.asarray(np.tril(np.ones((rows, rows), np.float32)), BF16)
    tab = _const_spec((rows, g * n))
    out = pl.pallas_call(
        _s5_kernel,
        grid=(b, seq // rows),
        in_specs=[
            pl.BlockSpec((1, rows, S5_WIDTH), lambda i, t: (i, t, OFF_U // S5_WIDTH)),
            _const_spec((S5_WIDTH, 2 * g * n)),
            tab, tab, tab, tab, _const_spec((1, g * n)), _const_spec((1, g * n)),
            _const_spec((g * n, S5_WIDTH)),
            _const_spec((g * n, S5_WIDTH)),
            _const_spec((1, S5_WIDTH)),
            _const_spec((rows, rows)),
        ],
        out_specs=pl.BlockSpec((1, rows, S5_WIDTH), lambda i, t: (i, t, 0)),
        out_shape=jax.ShapeDtypeStruct((b, seq, S5_WIDTH), F32),
        scratch_shapes=[pltpu.VMEM((1, g * n), F32), pltpu.VMEM((1, g * n), F32)],
        compiler_params=_params("parallel", "arbitrary"),
        name="s5_mixer",
    )(proj3, bbd, pinv_r, pinv_i, pow_r, pow_i, a1_r, a1_i, cbd_r, cbd_i, d_skip.reshape(1, S5_WIDTH), tril)
    return out.reshape(b * seq, S5_WIDTH)


def _combine_kernel(x_ref, ga_ref, gb_ref, gc_ref, o1_ref, o2_ref, o3_ref, l1_ref, l2_ref, l3_ref,
                    ob_ref, s_ref, wa_ref, wb_ref, wcv_ref, wcg_ref, wo_ref, ex_ref, nf_ref,
                    x1_ref, hf_ref):
    l1, l2, l3 = l1_ref[...], l2_ref[...], l3_ref[...]
    m = jnp.maximum(jnp.maximum(l1, l2), l3)
    e1, e2, e3 = jnp.exp(l1 - m), jnp.exp(l2 - m), jnp.exp(l3 - m)
    inv = 1.0 / (e1 + e2 + e3)
    ex = ex_ref[...]

    def expand(wt):
        hi = wt.astype(BF16)
        lo = (wt - hi.astype(F32)).astype(BF16)
        return _dot(hi, ex) + _dot(lo, ex)

    mix = expand(e1 * inv) * o1_ref[...] + expand(e2 * inv) * o2_ref[...] + expand(e3 * inv) * o3_ref[...]
    y_a = _dot(mix.astype(BF16), wa_ref[...])
    y_b = _dot(ob_ref[...].astype(BF16), wb_ref[...])
    sb = s_ref[...].astype(BF16)
    y_c = _dot(sb, wcv_ref[...]) * _sigmoid(_dot(sb, wcg_ref[...]))
    mixed = _sigmoid(ga_ref[...]) * y_a + _sigmoid(gb_ref[...]) * y_b + _sigmoid(gc_ref[...]) * y_c
    x1 = x_ref[...] + _dot(mixed.astype(BF16), wo_ref[...])
    x1_ref[...] = x1
    hf_ref[...] = _rms(x1, nf_ref[...])


def _combine(x2d, proj2, att, ob, s, w_a_out, w_b_out, w_c_val, w_c_gate, w_out, norm_ffn, tm=512):
    t, d = x2d.shape
    tm = min(tm, t)
    (o1, l1), (o2, l2), (o3, l3) = att
    ex = np.zeros((128, ATT_WIDTH), np.float32)
    for h in range(ATT_SLOTS):
        ex[h, h * ATT_HEAD_DIM:(h + 1) * ATT_HEAD_DIM] = 1.0
    row = lambda w: pl.BlockSpec((tm, w), lambda i: (i, 0))
    gate = lambda k: pl.BlockSpec((tm, d), lambda i: (i, OFF_GATES // d + k))
    bf = lambda w: w.astype(BF16)
    return pl.pallas_call(
        _combine_kernel,
        grid=(t // tm,),
        in_specs=[
            row(d), gate(0), gate(1), gate(2),
            row(ATT_WIDTH), row(ATT_WIDTH), row(ATT_WIDTH), row(128), row(128), row(128),
            row(DN_WIDTH), row(S5_WIDTH),
            _const_spec((ATT_WIDTH, d)), _const_spec((DN_WIDTH, d)), _const_spec((S5_WIDTH, d)),
            _const_spec((S5_WIDTH, d)), _const_spec((d, d)), _const_spec((128, ATT_WIDTH)), _const_spec((1, d)),
        ],
        out_specs=[row(d), row(d)],
        out_shape=[jax.ShapeDtypeStruct((t, d), F32), jax.ShapeDtypeStruct((t, d), F32)],
        compiler_params=_params("parallel"),
        name="branch_combine",
    )(x2d, proj2, proj2, proj2, o1, o2, o3, l1, l2, l3, ob, s,
      bf(w_a_out), bf(w_b_out), bf(w_c_val), bf(w_c_gate), bf(w_out), jnp.asarray(ex, BF16), norm_ffn.reshape(1, d))


def _ple_kernel(x_ref, f_ref, pe_ref, pn_ref, wg_ref, pw_ref, o_ref):
    x2 = x_ref[...] + f_ref[...]
    gate = _sigmoid(_dot(_rms(x2, pn_ref[...]).astype(BF16), wg_ref[...]))
    o_ref[...] = x2 + gate * _dot(pe_ref[...].astype(BF16), pw_ref[...])


def _ple(x1, ffn, pe2d, ple_norm, ple_w_gate, ple_w, tm=512):
    t, d = x1.shape
    tm = min(tm, t)
    row = lambda w: pl.BlockSpec((tm, w), lambda i: (i, 0))
    return pl.pallas_call(
        _ple_kernel,
        grid=(t // tm,),
        in_specs=[row(d), row(d), row(PLE_DIM), _const_spec((1, d)), _const_spec((d, d)), _const_spec((PLE_DIM, d))],
        out_specs=row(d),
        out_shape=jax.ShapeDtypeStruct((t, d), F32),
        compiler_params=_params("parallel"),
        name="ple_residual",
    )(x1, ffn, pe2d, ple_norm.reshape(1, d), ple_w_gate.astype(BF16), ple_w.astype(BF16))


PEER_PICKS = PEER_HEADS * PEER_TOPK
PEER_EXPERTS = PEER_KEYS * PEER_KEYS
PEER_HALF = D_MODEL // 2
ROW_SUBLANES = PEER_HALF // 128
TABLE_PAD = 8
ROUTE_ROWS = 128
EXPERT_ROWS = 256
UP_BATCH = 32
DOWN_BATCH = 16


STAIR_COUNTS = tuple(PEER_TOPK // (a + 1) for a in range(PEER_TOPK))
STAIR_ROWS = -(-sum(STAIR_COUNTS) // 8) * 8
POS_PAD = 1 << 20


def _top_rows(work, row_id, k, big):
    vals, ids = [], []
    for _ in range(k):
        m = jnp.max(work, axis=0, keepdims=True)
        pos = jnp.min(jnp.where(work == m, row_id, big), axis=0, keepdims=True)
        vals.append(m)
        ids.append(pos)
        work = jnp.where(row_id == pos, -jnp.inf, work)
    return vals, ids


def _peer_route_kernel(hf_ref, wq_ref, k1_ref, k2_ref, pos_ref, idx_ref, gate_ref):
    rows = hf_ref.shape[0]
    qry = _dot(hf_ref[...].astype(BF16), wq_ref[...]).astype(BF16)
    k1, k2 = k1_ref[...], k2_ref[...]
    key_id = lax.broadcasted_iota(jnp.int32, (PEER_KEYS, rows), 0).astype(F32)
    cand_pos = pos_ref[...]
    pad = STAIR_ROWS - sum(STAIR_COUNTS)
    idx_rows, gate_rows = [], []
    for h in range(PEER_HEADS):
        q1 = qry[:, (2 * h) * PEER_KEY_DIM:(2 * h + 1) * PEER_KEY_DIM]
        q2 = qry[:, (2 * h + 1) * PEER_KEY_DIM:(2 * h + 2) * PEER_KEY_DIM]
        v1, i1 = _top_rows(_dot_nt(k1, q1), key_id, PEER_TOPK, float(PEER_KEYS))
        v2, i2 = _top_rows(_dot_nt(k2, q2), key_id, PEER_TOPK, float(PEER_KEYS))
        v2c = jnp.concatenate(v2, axis=0)
        i2c = jnp.concatenate(i2, axis=0)
        cand_s = jnp.concatenate([v1[a] + v2c[0:n] for a, n in enumerate(STAIR_COUNTS)]
                                 + [jnp.full((pad, rows), -jnp.inf, F32)], axis=0)
        cand_i = jnp.concatenate([i1[a] * PEER_KEYS + i2c[0:n] for a, n in enumerate(STAIR_COUNTS)]
                                 + [jnp.zeros((pad, rows), F32)], axis=0)
        top_s, pos = _top_rows(cand_s, cand_pos, PEER_TOPK, float(POS_PAD))
        e = [jnp.exp(s - top_s[0]) for s in top_s]
        den = e[0]
        for x in e[1:]:
            den = den + x
        inv = 1.0 / den
        for s, p in zip(e, pos):
            gate_rows.append(s * inv)
            idx_rows.append(jnp.sum(jnp.where(cand_pos == p, cand_i, 0.0), axis=0, keepdims=True))
    idx_ref[...] = (jnp.concatenate(idx_rows, axis=0).T * ROW_SUBLANES + TABLE_PAD).astype(jnp.int32)
    gate_ref[...] = jnp.concatenate(gate_rows, axis=0).T


def _peer_route(hf, w_query, keys1, keys2):
    t, d = hf.shape
    rows = min(ROUTE_ROWS, t)
    nq = w_query.shape[1]
    flat_pos = [a * PEER_TOPK + b for a, n in enumerate(STAIR_COUNTS) for b in range(n)]
    flat_pos += [POS_PAD] * (STAIR_ROWS - len(flat_pos))
    cand_pos = jnp.asarray(np.tile(np.asarray(flat_pos, np.float32)[:, None], (1, rows)))
    return pl.pallas_call(
        _peer_route_kernel,
        grid=(t // rows,),
        in_specs=[pl.BlockSpec((rows, d), lambda i: (i, 0)), _const_spec((d, nq)),
                  _const_spec((PEER_KEYS, PEER_KEY_DIM)), _const_spec((PEER_KEYS, PEER_KEY_DIM)),
                  _const_spec((STAIR_ROWS, rows))],
        out_specs=[pl.BlockSpec((rows, PEER_PICKS), lambda i: (i, 0)), pl.BlockSpec((rows, PEER_PICKS), lambda i: (i, 0))],
        out_shape=[jax.ShapeDtypeStruct((t, PEER_PICKS), jnp.int32), jax.ShapeDtypeStruct((t, PEER_PICKS), F32)],
        compiler_params=_params("parallel"),
        name="peer_route",
    )(hf, w_query.astype(BF16), keys1.astype(BF16), keys2.astype(BF16), cand_pos)


PACK_ROWS = 512


def _pack_kernel(t_ref, o_ref):
    x = t_ref[0]
    lo = lax.bitcast_convert_type(x[:, :PEER_HALF].astype(BF16).astype(F32), jnp.uint32) >> 16
    hi = lax.bitcast_convert_type(x[:, PEER_HALF:].astype(BF16).astype(F32), jnp.uint32)
    words = hi | lo
    for c in range(ROW_SUBLANES):
        o_ref[pl.ds(c, x.shape[0], stride=ROW_SUBLANES), :] = words[:, c * 128:(c + 1) * 128]


def _pack_table(tables, layer):
    _, e, d = tables.shape
    rows = min(PACK_ROWS, e)
    return pl.pallas_call(
        _pack_kernel,
        grid=(e // rows,),
        in_specs=[pl.BlockSpec((1, rows, d), lambda i: (layer, i, 0))],
        out_specs=pl.BlockSpec((rows * ROW_SUBLANES, 128), lambda i: (i, 0)),
        out_shape=jax.ShapeDtypeStruct((e * ROW_SUBLANES, 128), jnp.uint32),
        compiler_params=_params("parallel"),
        name="pack_expert_table",
    )(tables)


def _unpack_row(words):
    lo = lax.bitcast_convert_type(words << 16, F32)
    hi = lax.bitcast_convert_type(words & jnp.uint32(0xFFFF0000), F32)
    return lo, hi


def _load_table_once(tab_hbm, tab_vmem, sem):
    @pl.when(pl.program_id(0) == 0)
    def _():
        n = tab_hbm.shape[0]
        cp = pltpu.make_async_copy(tab_hbm, tab_vmem.at[pl.ds(TABLE_PAD, n), :], sem)
        cp.start()
        tab_vmem[0:TABLE_PAD, :] = jnp.zeros((TABLE_PAD, 128), jnp.uint32)
        tab_vmem[TABLE_PAD + n:2 * TABLE_PAD + n, :] = jnp.zeros((TABLE_PAD, 128), jnp.uint32)
        cp.wait()


def _stage_ids(idx_hbm, idx_smem, sems, count):
    step = pl.program_id(0)
    slot = step % 2

    def copy(s, buf):
        return pltpu.make_async_copy(idx_hbm.at[pl.ds(s * count, count)],
                                     idx_smem.at[pl.ds(buf * count, count)], sems.at[1 + buf])

    @pl.when(step == 0)
    def _():
        copy(0, 0).start()

    @pl.when(step + 1 < pl.num_programs(0))
    def _():
        copy(step + 1, 1 - slot).start()

    copy(step, slot).wait()
    return slot * count


def _peer_up_kernel(idx_hbm, x_ref, gate_ref, sel_ref, tab_hbm, o_ref, tab_ref, idx_smem, part_ref,
                    xx_ref, sems):
    rows = gate_ref.shape[0]
    _load_table_once(tab_hbm, tab_ref, sems.at[0])
    id0 = _stage_ids(idx_hbm, idx_smem, sems, rows * PEER_PICKS)
    half = PEER_HALF // 128
    batch = part_ref.shape[1] // 8
    r_id =lax.broadcasted_iota(jnp.int32, (8, PEER_PICKS), 0)
    c_id = lax.broadcasted_iota(jnp.int32, (8, PEER_PICKS), 1)
    row_pick = 4 * (r_id % 2) + 2 * ((r_id // 2) % 2) + (r_id // 4)
    diag = (c_id % 8 == row_pick)[None]
    sub_id = lax.broadcasted_iota(jnp.int32, (8, 128), 0)
    low2 = (sub_id % 4) < 2
    even = (sub_id % 2) == 0
    low = sub_id < ROW_SUBLANES
    group = 8
    unroll = 4

    def fold_rows(p0, p1, p2, p3):
        def half_fold(a, b, mask, s):
            x = jnp.where(mask, a, pltpu.roll(b, s, 0))
            y = jnp.where(mask, pltpu.roll(a, 8 - s, 0), b)
            return x + y
        return half_fold(half_fold(p0, p1, low2, 2), half_fold(p2, p3, low2, 2), even, 1)

    def token_batch(b, carry):
        base = pl.multiple_of(b * batch, batch)

        def token_group(tg, c):
            tg8 = pl.multiple_of(tg * group, group)
            id_refs = [idx_smem.at[pl.ds(pl.multiple_of(id0 + (base + tg8 + i) * PEER_PICKS, PEER_PICKS), PEER_PICKS)]
                       for i in range(group)]
            for i in range(group):
                x_lo = x_ref[base + tg8 + i, 0:half, :]
                x_hi = x_ref[base + tg8 + i, half:2 * half, :]
                xx_ref[i, 0:8, :] = jnp.concatenate([x_lo, x_lo], axis=0)
                xx_ref[i, 8:16, :] = jnp.concatenate([x_hi, x_hi], axis=0)

            def pick_step(k, cc):
                for u in range(unroll):
                    ku = k * unroll + u
                    k8 = pl.multiple_of(ku * 8, 8)
                    o = [k8 + j for j in range(8)]
                    dst = part_ref.at[ku, pl.ds(pl.multiple_of(tg8 * 8, 8 * group), 8 * group), :]
                    for i in range(group):
                        x_lo2 = xx_ref[i, 0:8, :]
                        x_hi2 = xx_ref[i, 8:16, :]
                        tiles = []
                        for j in range(0, 8, 2):
                            ra, rb = id_refs[i][o[j]], id_refs[i][o[j + 1]]
                            words = jnp.where(low, tab_ref[pl.ds(ra, 8), :], tab_ref[pl.ds(rb - ROW_SUBLANES, 8), :])
                            lo, hi = _unpack_row(words)
                            tiles.append(lo * x_lo2 + hi * x_hi2)
                        dst[i * 8:(i + 1) * 8, :] = fold_rows(*tiles)
                return cc

            lax.fori_loop(0, PEER_PICKS // 8 // unroll, pick_step, 0)
            return c

        lax.fori_loop(0, batch // group, token_group, 0)
        sums = jnp.zeros((batch * 8, PEER_PICKS), F32)
        for k in range(PEER_PICKS // 8):
            part = part_ref[k]
            sums = sums + _dot(part.astype(BF16), sel_ref[k])
        act = jnp.sum(jnp.where(diag, sums.reshape(batch, 8, PEER_PICKS), 0.0), axis=1)
        o_ref[pl.ds(base, batch), :] = gate_ref[pl.ds(base, batch), :] * _gelu_tanh(act)
        return carry

    lax.fori_loop(0, rows // batch, token_batch, 0)


def _peer_down_kernel(idx_hbm, w_ref, ones_ref, tab_hbm, o_ref, tab_ref, idx_smem, splat_ref, sems):
    rows = w_ref.shape[0]
    _load_table_once(tab_hbm, tab_ref, sems.at[0])
    id0 = _stage_ids(idx_hbm, idx_smem, sems, rows * PEER_PICKS)
    half = PEER_HALF // 128
    batch = splat_ref.shape[0]
    group = 4
    unroll = 4
    low = lax.broadcasted_iota(jnp.int32, (8, 128), 0) < ROW_SUBLANES
    eye = (lax.broadcasted_iota(jnp.int32, (PEER_PICKS, PEER_PICKS), 0)
           == lax.broadcasted_iota(jnp.int32, (PEER_PICKS, PEER_PICKS), 1)).astype(F32)
    ones = ones_ref[...]

    def token_batch(b, carry):
        base = pl.multiple_of(b * batch, batch)
        for i in range(batch):
            d = eye * w_ref[pl.ds(base + i, 1), :]
            splat_ref[i] = _dot(d.astype(BF16), ones)

        def token_group(tg, c):
            tg8 = pl.multiple_of(tg * group, group)
            id_refs = [idx_smem.at[pl.ds(pl.multiple_of(id0 + (base + tg8 + i) * PEER_PICKS, PEER_PICKS), PEER_PICKS)]
                       for i in range(group)]

            def pick_step(k, acc):
                acc = list(acc)
                for u in range(unroll):
                    k8 = pl.multiple_of((k * unroll + u) * 8, 8)
                    o = [k8 + j for j in range(8)]
                    sp = splat_ref.at[pl.ds(tg8, group), pl.ds(k8, 8), :]
                    for i in range(group):
                        for j in range(0, 8, 2):
                            ra, rb = id_refs[i][o[j]], id_refs[i][o[j + 1]]
                            words = jnp.where(low, tab_ref[pl.ds(ra, 8), :], tab_ref[pl.ds(rb - ROW_SUBLANES, 8), :])
                            lo, hi = _unpack_row(words)
                            w = jnp.where(low, jnp.broadcast_to(sp[i, j:j + 1, :], (8, 128)),
                                          jnp.broadcast_to(sp[i, j + 1:j + 2, :], (8, 128)))
                            acc[2 * i] = acc[2 * i] + w * lo
                            acc[2 * i + 1] = acc[2 * i + 1] + w * hi
                return tuple(acc)

            zero = jnp.zeros((8, 128), F32)
            acc = lax.fori_loop(0, PEER_PICKS // 8 // unroll, pick_step, (zero,) * (2 * group))
            for i in range(group):
                o_ref[base + tg8 + i, 0:half, :] = acc[2 * i][0:half] + acc[2 * i][half:2 * half]
                o_ref[base + tg8 + i, half:2 * half, :] = acc[2 * i + 1][0:half] + acc[2 * i + 1][half:2 * half]
            return c

        lax.fori_loop(0, batch // group, token_group, 0)
        return carry

    lax.fori_loop(0, rows // batch, token_batch, 0)


def _peer_experts(hf, idx, gate, expert_u, expert_v, layer):
    t, d = hf.shape
    rows = min(EXPERT_ROWS, t)
    sel = np.zeros((PEER_PICKS // 8, 128, PEER_PICKS), np.float32)
    for j in range(PEER_PICKS // 8):
        sel[j, :, 8 * j:8 * j + 8] = 1.0
    sel = jnp.asarray(sel, BF16)
    tab_shape = (PEER_EXPERTS * ROW_SUBLANES + 2 * TABLE_PAD, 128)
    picks = pl.BlockSpec((rows, PEER_PICKS), lambda i: (i, 0))
    hbm = pl.BlockSpec(memory_space=pl.ANY)
    tok3 = pl.BlockSpec((rows, d // 128, 128), lambda i: (i, 0, 0))
    idx_flat = idx.reshape(t * PEER_PICKS)
    ids_smem = pltpu.SMEM((2 * rows * PEER_PICKS,), jnp.int32)
    w = pl.pallas_call(
        _peer_up_kernel,
        grid=(t // rows,),
        in_specs=[hbm, tok3, picks, _const_spec(sel.shape), hbm],
        out_specs=picks,
        out_shape=jax.ShapeDtypeStruct((t, PEER_PICKS), F32),
        scratch_shapes=[pltpu.VMEM(tab_shape, jnp.uint32), ids_smem,
                        pltpu.VMEM((PEER_PICKS // 8, UP_BATCH * 8, 128), F32),
                        pltpu.VMEM((8, 16, 128), F32), pltpu.SemaphoreType.DMA((3,))],
        compiler_params=_params("arbitrary"),
        name="peer_up",
    )(idx_flat, hf.reshape(t, d // 128, 128), gate, sel, _pack_table(expert_u, layer))
    out = pl.pallas_call(
        _peer_down_kernel,
        grid=(t // rows,),
        in_specs=[hbm, picks, _const_spec((PEER_PICKS, 128)), hbm],
        out_specs=tok3,
        out_shape=jax.ShapeDtypeStruct((t, d // 128, 128), F32),
        scratch_shapes=[pltpu.VMEM(tab_shape, jnp.uint32), ids_smem,
                        pltpu.VMEM((DOWN_BATCH, PEER_PICKS, 128), F32), pltpu.SemaphoreType.DMA((3,))],
        compiler_params=_params("arbitrary"),
        name="peer_down",
    )(idx_flat, w, jnp.ones((PEER_PICKS, 128), BF16), _pack_table(expert_v, layer))
    return out.reshape(t, d)


def _peer(hf, w_query, keys1, keys2, expert_u, expert_v, layer):
    idx, gate = _peer_route(hf, w_query, keys1, keys2)
    return _peer_experts(hf, idx, gate, expert_u, expert_v, layer)


def _pad_w_in(w_in):
    d = w_in.shape[0]
    outs = [jnp.zeros((d, PROJ_COLS), BF16), jnp.zeros((d, ATT_COLS), BF16)]
    src = 0
    for size, (arr, dst) in zip(SRC_SIZES, SRC_DST):
        outs[arr] = outs[arr].at[:, dst:dst + size].set(w_in[:, src:src + size].astype(BF16))
        src += size
    return outs


def _layer(x2d, pe2d, b, seq, norm_mix, w_in, att_q_gain, att_k_gain, w_a_out, dn_conv, dn_a_log, dn_dt_bias,
           dn_out_gain, w_b_out, s5_a_re, s5_a_im, s5_log_dt, s5_b_re, s5_b_im, s5_c_re, s5_c_im, s5_d,
           w_c_val, w_c_gate, w_out, norm_ffn, peer_w_query, peer_keys1, peer_keys2, peer_u, peer_v,
           ple_w, ple_norm, ple_w_gate, layer):
    w_main, w_att = _pad_w_in(w_in)
    proj2 = _norm_matmul(x2d, norm_mix, w_main, tm=2048, tn=512)
    proj3 = proj2.reshape(b, seq, PROJ_COLS)
    proj_att = _norm_matmul(x2d, norm_mix, w_att, tn=ATT_COLS).reshape(b, seq, ATT_COLS)
    att = [_attention_group(proj_att, att_q_gain, att_k_gain, g) for g in range(len(ATT_GROUPS))]
    ob = _deltanet(proj3, dn_conv, dn_a_log, dn_dt_bias, dn_out_gain)
    s = _s5(proj3, s5_a_re, s5_a_im, s5_log_dt, s5_b_re, s5_b_im, s5_c_re, s5_c_im, s5_d)
    x1, hf = _combine(x2d, proj2, att, ob, s, w_a_out, w_b_out, w_c_val, w_c_gate, w_out, norm_ffn)
    ffn = _peer(hf, peer_w_query, peer_keys1, peer_keys2, peer_u, peer_v, layer)
    return _ple(x1, ffn, pe2d, ple_norm, ple_w_gate, ple_w)


def kernel(x, p, norm_mix, w_in, att_q_gain, att_k_gain, w_a_out, dn_conv, dn_a_log, dn_dt_bias, dn_out_gain, w_b_out, s5_a_re, s5_a_im, s5_log_dt, s5_b_re, s5_b_im, s5_c_re, s5_c_im, s5_d, w_c_val, w_c_gate, w_out, norm_ffn, peer_w_query, peer_keys1, peer_keys2, peer_u, peer_v, ple_w, ple_norm, ple_w_gate):
    b, seq, d = x.shape
    depth = p.shape[0]
    x2d = x.reshape(b * seq, d)
    for i in range(depth):
        x2d = _layer(x2d, p[i].reshape(b * seq, PLE_DIM), b, seq, norm_mix[i], w_in[i], att_q_gain[i], att_k_gain[i],
                     w_a_out[i], dn_conv[i], dn_a_log[i], dn_dt_bias[i], dn_out_gain[i], w_b_out[i], s5_a_re[i],
                     s5_a_im[i], s5_log_dt[i], s5_b_re[i], s5_b_im[i], s5_c_re[i], s5_c_im[i], s5_d[i], w_c_val[i],
                     w_c_gate[i], w_out[i], norm_ffn[i], peer_w_query[i], peer_keys1[i], peer_keys2[i], peer_u,
                     peer_v, ple_w[i], ple_norm[i], ple_w_gate[i], layer=i)
    return x2d.reshape(b, seq, d)
```

```python
import functools

import numpy as np
import jax
import jax.numpy as jnp
from jax import lax
from jax.experimental import pallas as pl
from jax.experimental.pallas import tpu as pltpu

F32 = jnp.float32
BF16 = jnp.bfloat16

D_MODEL = 1024
NORM_EPS = 1e-6
ATT_GROUPS = ((128, 1), (512, 4), (2048, 16))
ATT_SLOTS = 4
ATT_HEAD_DIM = 64
ATT_WIDTH = 256
ATT_BLOCK = 128
ALIBI_MAX_EXP = 8.0
DN_HEADS = 4
DN_HEAD_DIM = 128
DN_WIDTH = 512
DN_CONV = 4
DN_CHUNK = 64
S5_GROUP = 16
S5_GROUPS = 16
S5_WIDTH = 256
S5_STATE = 64
PEER_HEADS = 8
PEER_KEYS = 128
PEER_TOPK = 16
PEER_KEY_DIM = 128
PLE_DIM = 256

OFF_QKV, OFF_Z, OFF_GATES, OFF_U, OFF_BA = 0, 1536, 2048, 5120, 5376
PROJ_COLS = 5632
OFF_QA, OFF_KA, OFF_VA = 0, 768, 1024
ATT_COLS = 1280
SRC_SIZES = (768, 256, 256, 1536, 512, 4, 4, 256, 3072)
SRC_DST = ((1, OFF_QA), (1, OFF_KA), (1, OFF_VA), (0, OFF_QKV), (0, OFF_Z), (0, OFF_BA), (0, OFF_BA + 4),
           (0, OFF_U), (0, OFF_GATES))

VMEM_LIMIT = 56 * 1024 * 1024
NEG_BIG = -1e30


def _params(*sem):
    return pltpu.CompilerParams(dimension_semantics=sem, vmem_limit_bytes=VMEM_LIMIT)


def _const_spec(shape):
    nd = len(shape)
    return pl.BlockSpec(shape, lambda *_: (0,) * nd)


def _rms(x, gain):
    return x * lax.rsqrt(jnp.mean(x * x, axis=-1, keepdims=True) + NORM_EPS) * gain


def _sigmoid(x):
    return 1.0 / (1.0 + jnp.exp(-x))


def _gelu_tanh(x):
    return 0.5 * x * (1.0 + jnp.tanh(0.7978845608028654 * (x + 0.044715 * (x * x * x))))


def _dot(a, b, **kw):
    return jnp.dot(a, b, preferred_element_type=F32, **kw)


def _dot3(a, b):
    a_hi, b_hi = a.astype(BF16), b.astype(BF16)
    a_lo = (a - a_hi.astype(F32)).astype(BF16)
    b_lo = (b - b_hi.astype(F32)).astype(BF16)
    return _dot(a_hi, b_hi) + (_dot(a_hi, b_lo) + _dot(a_lo, b_hi))


def _dot_nt(a, b):
    return lax.dot_general(a, b, (((1,), (1,)), ((), ())), preferred_element_type=F32)


def _dot_tn(a, b):
    return lax.dot_general(a, b, (((0,), (0,)), ((), ())), preferred_element_type=F32)


def _norm_matmul_kernel(x_ref, g_ref, w_ref, o_ref, h_ref):
    @pl.when(pl.program_id(1) == 0)
    def _():
        h_ref[...] = _rms(x_ref[...], g_ref[...]).astype(BF16)

    o_ref[...] = _dot(h_ref[...], w_ref[...])


def _norm_matmul(x2d, gain, w_bf16, tm=1024, tn=1024):
    t, d = x2d.shape
    n = w_bf16.shape[1]
    tm = min(tm, t)
    tn = min(tn, n)
    return pl.pallas_call(
        _norm_matmul_kernel,
        grid=(t // tm, n // tn),
        in_specs=[
            pl.BlockSpec((tm, d), lambda i, j: (i, 0)),
            pl.BlockSpec((1, d), lambda i, j: (0, 0)),
            pl.BlockSpec((d, tn), lambda i, j: (0, j)),
        ],
        out_specs=pl.BlockSpec((tm, tn), lambda i, j: (i, j)),
        out_shape=jax.ShapeDtypeStruct((t, n), F32),
        scratch_shapes=[pltpu.VMEM((tm, d), BF16)],
        compiler_params=_params("parallel", "arbitrary"),
        name="norm_matmul",
    )(x2d, gain.reshape(1, d), w_bf16)


def _attn_kernel(q_ref, kp_ref, kc_ref, vp_ref, vc_ref, qg_ref, kg_ref, bd_ref, o_ref, lse_ref, *,
                 dilation, span, slopes):
    n = pl.program_id(2)
    bd = bd_ref[...]

    def head_norm(x, gain):
        x2 = x * x
        hi = x2.astype(BF16)
        lo = (x2 - hi.astype(F32)).astype(BF16)
        ss = _dot(hi, bd) + _dot(lo, bd)
        return x * lax.rsqrt(ss * (1.0 / ATT_HEAD_DIM) + NORM_EPS) * gain

    qpos = lax.broadcasted_iota(jnp.int32, (ATT_BLOCK, 2 * ATT_BLOCK), 0)
    kpos = lax.broadcasted_iota(jnp.int32, (ATT_BLOCK, 2 * ATT_BLOCK), 1)
    rel = qpos + ATT_BLOCK - kpos
    valid = (rel >= 0) & (rel <= span) & ((kpos >= ATT_BLOCK) | (n > 0))
    dist = (rel * dilation).astype(F32)
    lane = lax.broadcasted_iota(jnp.int32, (ATT_BLOCK, 128), 1)
    col_head = lax.broadcasted_iota(jnp.int32, (ATT_BLOCK, ATT_WIDTH), 1) // ATT_HEAD_DIM
    heads = range(ATT_SLOTS)

    q = head_norm(q_ref[0], qg_ref[...]) * (ATT_HEAD_DIM ** -0.5)
    k = head_norm(jnp.concatenate([kp_ref[0], kc_ref[0]], axis=0), kg_ref[...])
    v = jnp.concatenate([vp_ref[0], vc_ref[0]], axis=0)
    kb, vb = k.astype(BF16), v.astype(BF16)
    s = [_dot_nt(jnp.where(col_head == h, q, 0.0).astype(BF16), kb) for h in heads]
    s = [jnp.where(valid, s[h] - slopes[h] * dist, NEG_BIG) for h in heads]
    m = [jnp.max(s[h], axis=-1, keepdims=True) for h in heads]
    e = [jnp.exp(s[h] - m[h]) for h in heads]
    den = [jnp.sum(e[h], axis=-1, keepdims=True) for h in heads]
    pv = [_dot(e[h].astype(BF16), vb) for h in heads]
    out = jnp.zeros((ATT_BLOCK, ATT_WIDTH), F32)
    lse_all = jnp.zeros((ATT_BLOCK, 128), F32)
    for h in heads:
        out = jnp.where(col_head == h, pv[h] / den[h], out)
        lse_all = jnp.where(lane == h, m[h] + jnp.log(den[h]), lse_all)
    o_ref[0] = out
    lse_ref[0] = lse_all


def _attention_group(proj3, q_gain, k_gain, group):
    b, seq, _ = proj3.shape
    window, dilation = ATT_GROUPS[group]
    span = window // dilation
    sub = seq // dilation
    nb = sub // ATT_BLOCK
    view = proj3.reshape(b, sub, dilation * ATT_COLS)
    cb = ATT_COLS // ATT_WIDTH
    qc, kc, vc = OFF_QA // ATT_WIDTH + group, OFF_KA // ATT_WIDTH, OFF_VA // ATT_WIDTH
    n_heads = len(ATT_GROUPS) * ATT_SLOTS
    slopes = 2.0 ** (-ALIBI_MAX_EXP * np.arange(1, n_heads + 1) / n_heads)
    slopes = tuple(float(s) for s in slopes.reshape(len(ATT_GROUPS), ATT_SLOTS)[group])
    blk = (1, ATT_BLOCK, ATT_WIDTH)
    bd = (np.arange(ATT_WIDTH)[:, None] // ATT_HEAD_DIM == np.arange(ATT_WIDTH)[None, :] // ATT_HEAD_DIM)
    o, lse = pl.pallas_call(
        functools.partial(_attn_kernel, dilation=dilation, span=span, slopes=slopes),
        grid=(b, dilation, nb),
        in_specs=[
            pl.BlockSpec(blk, lambda i, r, n: (i, n, r * cb + qc)),
            pl.BlockSpec(blk, lambda i, r, n: (i, jnp.maximum(n - 1, 0), r * cb + kc)),
            pl.BlockSpec(blk, lambda i, r, n: (i, n, r * cb + kc)),
            pl.BlockSpec(blk, lambda i, r, n: (i, jnp.maximum(n - 1, 0), r * cb + vc)),
            pl.BlockSpec(blk, lambda i, r, n: (i, n, r * cb + vc)),
            _const_spec((1, ATT_WIDTH)),
            _const_spec((1, ATT_WIDTH)),
            _const_spec((ATT_WIDTH, ATT_WIDTH)),
        ],
        out_specs=[
            pl.BlockSpec(blk, lambda i, r, n: (i, n, r)),
            pl.BlockSpec((1, ATT_BLOCK, 128), lambda i, r, n: (i, n, r)),
        ],
        out_shape=[
            jax.ShapeDtypeStruct((b, sub, dilation * ATT_WIDTH), F32),
            jax.ShapeDtypeStruct((b, sub, dilation * 128), F32),
        ],
        compiler_params=_params("parallel", "parallel", "arbitrary"),
        name=f"dilated_attention_g{group}",
    )(view, view, view, view, view,
      jnp.tile(q_gain, ATT_SLOTS).reshape(1, ATT_WIDTH), jnp.tile(k_gain, ATT_SLOTS).reshape(1, ATT_WIDTH),
      jnp.asarray(bd, BF16))
    return o.reshape(b * seq, ATT_WIDTH), lse.reshape(b * seq, 128)


DN_ROWS = 256


def _deltanet_kernel(qkv_ref, z_ref, ba_ref, cw_ref, alog_ref, dtb_ref, og_ref, tril_ref, o_ref,
                     state_ref, tail_ref):
    c = DN_CHUNK
    rows = qkv_ref.shape[1]

    @pl.when(pl.program_id(1) == 0)
    def _():
        state_ref[...] = jnp.zeros_like(state_ref)
        tail_ref[...] = jnp.zeros_like(tail_ref)

    x = qkv_ref[0]
    xc = jnp.concatenate([tail_ref[...], x], axis=0)
    cw = cw_ref[...]
    acc = cw[DN_CONV - 1:DN_CONV] * x
    for j in range(DN_CONV - 1):
        acc = acc + cw[j:j + 1] * xc[5 + j:5 + j + rows]
    tail_ref[...] = x[rows - 8:rows]
    a = acc * _sigmoid(acc)

    ba = ba_ref[0]
    beta_all = _sigmoid(ba)
    sp = ba + dtb_ref[...]
    sp = jnp.maximum(sp, 0.0) + jnp.log(1.0 + jnp.exp(-jnp.abs(sp)))
    g_all = -(jnp.exp(alog_ref[...]) * sp)
    tril = tril_ref[...]
    ii = lax.broadcasted_iota(jnp.int32, (c, c), 0)
    jj = lax.broadcasted_iota(jnp.int32, (c, c), 1)
    incl = ii >= jj
    strict = ii > jj
    eye = (ii == jj).astype(F32)
    z = z_ref[0]
    og = og_ref[...]

    n_chunks = rows // c
    units = [(h, ci) for ci in range(n_chunks) for h in range(DN_HEADS)]

    loc = {}
    for h in range(DN_HEADS):
        hs = slice(h * DN_HEAD_DIM, (h + 1) * DN_HEAD_DIM)
        qh = a[:, hs]
        kh = a[:, DN_WIDTH + h * DN_HEAD_DIM:DN_WIDTH + (h + 1) * DN_HEAD_DIM]
        vh = a[:, 2 * DN_WIDTH + h * DN_HEAD_DIM:2 * DN_WIDTH + (h + 1) * DN_HEAD_DIM]
        qh = qh * lax.rsqrt(jnp.sum(qh * qh, axis=-1, keepdims=True) + NORM_EPS) * (DN_HEAD_DIM ** -0.5)
        kh = kh * lax.rsqrt(jnp.sum(kh * kh, axis=-1, keepdims=True) + NORM_EPS)
        beta = beta_all[:, h:h + 1]
        gh = jnp.broadcast_to(g_all[:, DN_HEADS + h:DN_HEADS + h + 1], (rows, DN_HEAD_DIM))
        for ci in range(n_chunks):
            rs = slice(ci * c, (ci + 1) * c)
            g_hi = gh[rs].astype(BF16)
            g_lo = (gh[rs] - g_hi.astype(F32)).astype(BF16)
            gc = _dot(tril, g_hi) + _dot(tril, g_lo)
            gc64 = gc[:, :c]
            diff = gc64 - gc64.T
            decay = jnp.where(incl, jnp.exp(jnp.where(incl, diff, 0.0)), 0.0)
            eg = jnp.exp(gc)
            g_last = gc[c - 1:c, :]
            kc_, qc_, vc_, bc_ = kh[rs], qh[rs], vh[rs], beta[rs]
            kbeta = kc_ * bc_
            kcb = kc_.astype(BF16)
            loc[h, ci] = dict(
                lower=jnp.where(strict, _dot_nt(kbeta.astype(BF16), kcb) * decay, 0.0),
                vb=(vc_ * bc_).astype(BF16), kbe=(kbeta * eg).astype(BF16),
                attn=jnp.where(incl, _dot_nt(qc_.astype(BF16), kcb) * decay, 0.0).astype(BF16),
                q_dec=(qc_ * eg).astype(BF16), k_dec=(kc_ * jnp.exp(g_last - gc)).astype(BF16),
                g_end=jnp.exp(g_last))

    t_inv = {k: eye - loc[k]["lower"] for k in units}
    pw = {k: _dot3(loc[k]["lower"], loc[k]["lower"]) for k in units}
    for it in range(5):
        t_inv = {k: t_inv[k] + _dot3(t_inv[k], pw[k]) for k in units}
        if it < 4:
            pw = {k: _dot3(pw[k], pw[k]) for k in units}
    uw = {}
    for k in units:
        t_b = t_inv[k].astype(BF16)
        uw[k] = (_dot(t_b, loc[k]["vb"]), _dot(t_b, loc[k]["kbe"]).astype(BF16))

    state = [state_ref[h] for h in range(DN_HEADS)]
    o_chunks = [[] for _ in range(DN_HEADS)]
    for ci in range(n_chunks):
        for h in range(DN_HEADS):
            d = loc[h, ci]
            u, w = uw[h, ci]
            sb = state[h].astype(BF16)
            vnb = (u - _dot(w, sb)).astype(BF16)
            o_chunks[h].append(_dot(d["q_dec"], sb) + _dot(d["attn"], vnb))
            state[h] = state[h] * d["g_end"] + _dot_tn(d["k_dec"], vnb)
    outs = []
    for h in range(DN_HEADS):
        state_ref[h] = state[h]
        zh = z[:, h * DN_HEAD_DIM:(h + 1) * DN_HEAD_DIM]
        outs.append(_rms(jnp.concatenate(o_chunks[h], axis=0), og) * (zh * _sigmoid(zh)))
    o_ref[0] = jnp.concatenate(outs, axis=-1)


def _deltanet(proj3, conv_w, a_log, dt_bias, out_gain):
    b, seq, _ = proj3.shape
    rows = min(DN_ROWS, seq)
    pad_a = jnp.zeros((1, 128), F32).at[0, DN_HEADS:2 * DN_HEADS].set(a_log)
    pad_d = jnp.zeros((1, 128), F32).at[0, DN_HEADS:2 * DN_HEADS].set(dt_bias)
    tril = jnp.asarray(np.tril(np.ones((DN_CHUNK, DN_CHUNK), np.float32)), BF16)
    out = pl.pallas_call(
        _deltanet_kernel,
        grid=(b, seq // rows),
        in_specs=[
            pl.BlockSpec((1, rows, 3 * DN_WIDTH), lambda i, t: (i, t, OFF_QKV // (3 * DN_WIDTH))),
            pl.BlockSpec((1, rows, DN_WIDTH), lambda i, t: (i, t, OFF_Z // DN_WIDTH)),
            pl.BlockSpec((1, rows, 128), lambda i, t: (i, t, OFF_BA // 128)),
            _const_spec((DN_CONV, 3 * DN_WIDTH)),
            _const_spec((1, 128)),
            _const_spec((1, 128)),
            _const_spec((1, DN_HEAD_DIM)),
            _const_spec((DN_CHUNK, DN_CHUNK)),
        ],
        out_specs=pl.BlockSpec((1, rows, DN_WIDTH), lambda i, t: (i, t, 0)),
        out_shape=jax.ShapeDtypeStruct((b, seq, DN_WIDTH), F32),
        scratch_shapes=[pltpu.VMEM((DN_HEADS, DN_HEAD_DIM, DN_HEAD_DIM), F32), pltpu.VMEM((8, 3 * DN_WIDTH), F32)],
        compiler_params=_params("parallel", "arbitrary"),
        name="gated_deltanet",
    )(proj3, proj3, proj3, conv_w, pad_a, pad_d, out_gain.reshape(1, DN_HEAD_DIM), tril)
    return out.reshape(b * seq, DN_WIDTH)


S5_ROWS = 128
S5_N = S5_GROUPS * S5_STATE


def _s5_kernel(u_ref, bbd_ref, pinv_r_ref, pinv_i_ref, pow_r_ref, pow_i_ref, a1_r_ref, a1_i_ref,
               cr_ref, ci_ref, d_ref, tril_ref, o_ref, sr_ref, si_ref):
    @pl.when(pl.program_id(1) == 0)
    def _():
        sr_ref[...] = jnp.zeros_like(sr_ref)
        si_ref[...] = jnp.zeros_like(si_ref)

    u = u_ref[0]
    bu = _dot(u.astype(BF16), bbd_ref[...])
    br, bi = bu[:, :S5_N], bu[:, S5_N:]
    pr, pi = pinv_r_ref[...], pinv_i_ref[...]
    zr = pr * br - pi * bi
    zi = pr * bi + pi * br
    tril = tril_ref[...]

    def cumsum_rows(zz):
        hi = zz.astype(BF16)
        lo = (zz - hi.astype(F32)).astype(BF16)
        return _dot(tril, hi) + _dot(tril, lo)

    cr = cumsum_rows(zr)
    ci = cumsum_rows(zi)
    wr, wi = pow_r_ref[...], pow_i_ref[...]
    a_r, a_i = a1_r_ref[...], a1_i_ref[...]
    sr, si = sr_ref[...], si_ref[...]
    cr = cr + (a_r * sr - a_i * si)
    ci = ci + (a_r * si + a_i * sr)
    xr = wr * cr - wi * ci
    xi = wr * ci + wi * cr
    rows = xr.shape[0]
    sr_ref[...] = xr[rows - 1:rows]
    si_ref[...] = xi[rows - 1:rows]
    y = _dot(xr.astype(BF16), cr_ref[...]) - _dot(xi.astype(BF16), ci_ref[...]) + d_ref[...] * u
    o_ref[0] = _gelu_tanh(y)


def _s5(proj3, a_re, a_im, log_dt, b_re, b_im, c_re, c_im, d_skip):
    b, seq, _ = proj3.shape
    rows = min(S5_ROWS, seq)
    g, n, p = S5_GROUPS, S5_STATE, S5_GROUP
    dt = jnp.exp(log_dt)[:, None]
    lr, li = a_re, a_im
    mag = jnp.exp(lr * dt)
    ab_r, ab_i = mag * jnp.cos(li * dt), mag * jnp.sin(li * dt)
    den = lr * lr + li * li
    nr, ni = ab_r - 1.0, ab_i
    cr, ci = (nr * lr + ni * li) / den, (ni * lr - nr * li) / den
    bb_r = cr[..., None] * b_re - ci[..., None] * b_im
    bb_i = cr[..., None] * b_im + ci[..., None] * b_re
    eye_g = jnp.eye(g, dtype=F32)
    bbd_r = jnp.einsum('gnp,gh->gphn', bb_r, eye_g).reshape(g * p, g * n)
    bbd_i = jnp.einsum('gnp,gh->gphn', bb_i, eye_g).reshape(g * p, g * n)
    bbd = jnp.concatenate([bbd_r, bbd_i], axis=1).astype(BF16)
    cbd_r = jnp.einsum('gpn,gh->gnhp', c_re, eye_g).reshape(g * n, g * p).astype(BF16)
    cbd_i = jnp.einsum('gpn,gh->gnhp', c_im, eye_g).reshape(g * n, g * p).astype(BF16)
    steps = jnp.arange(rows, dtype=F32)[:, None]
    lrd = (lr * dt).reshape(1, g * n)
    lid = (li * dt).reshape(1, g * n)

    def power(k):
        m = jnp.exp(lrd * k)
        return m * jnp.cos(lid * k), m * jnp.sin(lid * k)

    pinv_r, pinv_i = power(-steps)
    pow_r, pow_i = power(steps)
    a1_r, a1_i = power(1.0)
    tril = jnp.asarray(np.tril(np.ones((rows, rows), np.float32)), BF16)
    tab = _const_spec((rows, g * n))
    out = pl.pallas_call(
        _s5_kernel,
        grid=(b, seq // rows),
        in_specs=[
            pl.BlockSpec((1, rows, S5_WIDTH), lambda i, t: (i, t, OFF_U // S5_WIDTH)),
            _const_spec((S5_WIDTH, 2 * g * n)),
            tab, tab, tab, tab, _const_spec((1, g * n)), _const_spec((1, g * n)),
            _const_spec((g * n, S5_WIDTH)),
            _const_spec((g * n, S5_WIDTH)),
            _const_spec((1, S5_WIDTH)),
            _const_spec((rows, rows)),
        ],
        out_specs=pl.BlockSpec((1, rows, S5_WIDTH), lambda i, t: (i, t, 0)),
        out_shape=jax.ShapeDtypeStruct((b, seq, S5_WIDTH), F32),
        scratch_shapes=[pltpu.VMEM((1, g * n), F32), pltpu.VMEM((1, g * n), F32)],
        compiler_params=_params("parallel", "arbitrary"),
        name="s5_mixer",
    )(proj3, bbd, pinv_r, pinv_i, pow_r, pow_i, a1_r, a1_i, cbd_r, cbd_i, d_skip.reshape(1, S5_WIDTH), tril)
    return out.reshape(b * seq, S5_WIDTH)


def _combine_kernel(x_ref, ga_ref, gb_ref, gc_ref, o1_ref, o2_ref, o3_ref, l1_ref, l2_ref, l3_ref,
                    ob_ref, s_ref, wa_ref, wb_ref, wcv_ref, wcg_ref, wo_ref, ex_ref, nf_ref,
                    x1_ref, hf_ref):
    l1, l2, l3 = l1_ref[...], l2_ref[...], l3_ref[...]
    m = jnp.maximum(jnp.maximum(l1, l2), l3)
    e1, e2, e3 = jnp.exp(l1 - m), jnp.exp(l2 - m), jnp.exp(l3 - m)
    inv = 1.0 / (e1 + e2 + e3)
    ex = ex_ref[...]

    def expand(wt):
        hi = wt.astype(BF16)
        lo = (wt - hi.astype(F32)).astype(BF16)
        return _dot(hi, ex) + _dot(lo, ex)

    mix = expand(e1 * inv) * o1_ref[...] + expand(e2 * inv) * o2_ref[...] + expand(e3 * inv) * o3_ref[...]
    y_a = _dot(mix.astype(BF16), wa_ref[...])
    y_b = _dot(ob_ref[...].astype(BF16), wb_ref[...])
    sb = s_ref[...].astype(BF16)
    y_c = _dot(sb, wcv_ref[...]) * _sigmoid(_dot(sb, wcg_ref[...]))
    mixed = _sigmoid(ga_ref[...]) * y_a + _sigmoid(gb_ref[...]) * y_b + _sigmoid(gc_ref[...]) * y_c
    x1 = x_ref[...] + _dot(mixed.astype(BF16), wo_ref[...])
    x1_ref[...] = x1
    hf_ref[...] = _rms(x1, nf_ref[...])


def _combine(x2d, proj2, att, ob, s, w_a_out, w_b_out, w_c_val, w_c_gate, w_out, norm_ffn, tm=512):
    t, d = x2d.shape
    tm = min(tm, t)
    (o1, l1), (o2, l2), (o3, l3) = att
    ex = np.zeros((128, ATT_WIDTH), np.float32)
    for h in range(ATT_SLOTS):
        ex[h, h * ATT_HEAD_DIM:(h + 1) * ATT_HEAD_DIM] = 1.0
    row = lambda w: pl.BlockSpec((tm, w), lambda i: (i, 0))
    gate = lambda k: pl.BlockSpec((tm, d), lambda i: (i, OFF_GATES // d + k))
    bf = lambda w: w.astype(BF16)
    return pl.pallas_call(
        _combine_kernel,
        grid=(t // tm,),
        in_specs=[
            row(d), gate(0), gate(1), gate(2),
            row(ATT_WIDTH), row(ATT_WIDTH), row(ATT_WIDTH), row(128), row(128), row(128),
            row(DN_WIDTH), row(S5_WIDTH),
            _const_spec((ATT_WIDTH, d)), _const_spec((DN_WIDTH, d)), _const_spec((S5_WIDTH, d)),
            _const_spec((S5_WIDTH, d)), _const_spec((d, d)), _const_spec((128, ATT_WIDTH)), _const_spec((1, d)),
        ],
        out_specs=[row(d), row(d)],
        out_shape=[jax.ShapeDtypeStruct((t, d), F32), jax.ShapeDtypeStruct((t, d), F32)],
        compiler_params=_params("parallel"),
        name="branch_combine",
    )(x2d, proj2, proj2, proj2, o1, o2, o3, l1, l2, l3, ob, s,
      bf(w_a_out), bf(w_b_out), bf(w_c_val), bf(w_c_gate), bf(w_out), jnp.asarray(ex, BF16), norm_ffn.reshape(1, d))


def _ple_kernel(x_ref, f_ref, pe_ref, pn_ref, wg_ref, pw_ref, o_ref):
    x2 = x_ref[...] + f_ref[...]
    gate = _sigmoid(_dot(_rms(x2, pn_ref[...]).astype(BF16), wg_ref[...]))
    o_ref[...] = x2 + gate * _dot(pe_ref[...].astype(BF16), pw_ref[...])


def _ple(x1, ffn, pe2d, ple_norm, ple_w_gate, ple_w, tm=512):
    t, d = x1.shape
    tm = min(tm, t)
    row = lambda w: pl.BlockSpec((tm, w), lambda i: (i, 0))
    return pl.pallas_call(
        _ple_kernel,
        grid=(t // tm,),
        in_specs=[row(d), row(d), row(PLE_DIM), _const_spec((1, d)), _const_spec((d, d)), _const_spec((PLE_DIM, d))],
        out_specs=row(d),
        out_shape=jax.ShapeDtypeStruct((t, d), F32),
        compiler_params=_params("parallel"),
        name="ple_residual",
    )(x1, ffn, pe2d, ple_norm.reshape(1, d), ple_w_gate.astype(BF16), ple_w.astype(BF16))


PEER_PICKS = PEER_HEADS * PEER_TOPK
PEER_EXPERTS = PEER_KEYS * PEER_KEYS
PEER_HALF = D_MODEL // 2
ROW_SUBLANES = PEER_HALF // 128
TABLE_PAD = 8
ROUTE_ROWS = 128
EXPERT_ROWS = 512
UP_BATCH = 32
DOWN_BATCH = 16


STAIR_COUNTS = tuple(PEER_TOPK // (a + 1) for a in range(PEER_TOPK))
STAIR_ROWS = -(-sum(STAIR_COUNTS) // 8) * 8
POS_PAD = 1 << 20


def _top_rows(work, row_id, k, big):
    vals, ids = [], []
    for _ in range(k):
        m = jnp.max(work, axis=0, keepdims=True)
        pos = jnp.min(jnp.where(work == m, row_id, big), axis=0, keepdims=True)
        vals.append(m)
        ids.append(pos)
        work = jnp.where(row_id == pos, -jnp.inf, work)
    return vals, ids


def _peer_route_kernel(hf_ref, wq_ref, k1_ref, k2_ref, pos_ref, idx_ref, gate_ref):
    rows = hf_ref.shape[0]
    qry = _dot(hf_ref[...].astype(BF16), wq_ref[...]).astype(BF16)
    k1, k2 = k1_ref[...], k2_ref[...]
    key_id = lax.broadcasted_iota(jnp.int32, (PEER_KEYS, rows), 0).astype(F32)
    cand_pos = pos_ref[...]
    pad = STAIR_ROWS - sum(STAIR_COUNTS)
    idx_rows, gate_rows = [], []
    for h in range(PEER_HEADS):
        q1 = qry[:, (2 * h) * PEER_KEY_DIM:(2 * h + 1) * PEER_KEY_DIM]
        q2 = qry[:, (2 * h + 1) * PEER_KEY_DIM:(2 * h + 2) * PEER_KEY_DIM]
        v1, i1 = _top_rows(_dot_nt(k1, q1), key_id, PEER_TOPK, float(PEER_KEYS))
        v2, i2 = _top_rows(_dot_nt(k2, q2), key_id, PEER_TOPK, float(PEER_KEYS))
        v2c = jnp.concatenate(v2, axis=0)
        i2c = jnp.concatenate(i2, axis=0)
        cand_s = jnp.concatenate([v1[a] + v2c[0:n] for a, n in enumerate(STAIR_COUNTS)]
                                 + [jnp.full((pad, rows), -jnp.inf, F32)], axis=0)
        cand_i = jnp.concatenate([i1[a] * PEER_KEYS + i2c[0:n] for a, n in enumerate(STAIR_COUNTS)]
                                 + [jnp.zeros((pad, rows), F32)], axis=0)
        top_s, pos = _top_rows(cand_s, cand_pos, PEER_TOPK, float(POS_PAD))
        e = [jnp.exp(s - top_s[0]) for s in top_s]
        den = e[0]
        for x in e[1:]:
            den = den + x
        inv = 1.0 / den
        for s, p in zip(e, pos):
            gate_rows.append(s * inv)
            idx_rows.append(jnp.sum(jnp.where(cand_pos == p, cand_i, 0.0), axis=0, keepdims=True))
    idx_ref[...] = (jnp.concatenate(idx_rows, axis=0).T * ROW_SUBLANES + TABLE_PAD).astype(jnp.int32)
    gate_ref[...] = jnp.concatenate(gate_rows, axis=0).T


def _peer_route(hf, w_query, keys1, keys2):
    t, d = hf.shape
    rows = min(ROUTE_ROWS, t)
    nq = w_query.shape[1]
    flat_pos = [a * PEER_TOPK + b for a, n in enumerate(STAIR_COUNTS) for b in range(n)]
    flat_pos += [POS_PAD] * (STAIR_ROWS - len(flat_pos))
    cand_pos = jnp.asarray(np.tile(np.asarray(flat_pos, np.float32)[:, None], (1, rows)))
    return pl.pallas_call(
        _peer_route_kernel,
        grid=(t // rows,),
        in_specs=[pl.BlockSpec((rows, d), lambda i: (i, 0)), _const_spec((d, nq)),
                  _const_spec((PEER_KEYS, PEER_KEY_DIM)), _const_spec((PEER_KEYS, PEER_KEY_DIM)),
                  _const_spec((STAIR_ROWS, rows))],
        out_specs=[pl.BlockSpec((rows, PEER_PICKS), lambda i: (i, 0)), pl.BlockSpec((rows, PEER_PICKS), lambda i: (i, 0))],
        out_shape=[jax.ShapeDtypeStruct((t, PEER_PICKS), jnp.int32), jax.ShapeDtypeStruct((t, PEER_PICKS), F32)],
        compiler_params=_params("parallel"),
        name="peer_route",
    )(hf, w_query.astype(BF16), keys1.astype(BF16), keys2.astype(BF16), cand_pos)


PACK_ROWS = 512


def _pack_kernel(t_ref, o_ref):
    x = t_ref[0]
    lo = lax.bitcast_convert_type(x[:, :PEER_HALF].astype(BF16).astype(F32), jnp.uint32) >> 16
    hi = lax.bitcast_convert_type(x[:, PEER_HALF:].astype(BF16).astype(F32), jnp.uint32)
    words = hi | lo
    for c in range(ROW_SUBLANES):
        o_ref[pl.ds(c, x.shape[0], stride=ROW_SUBLANES), :] = words[:, c * 128:(c + 1) * 128]


def _pack_table(tables, layer):
    _, e, d = tables.shape
    rows = min(PACK_ROWS, e)
    return pl.pallas_call(
        _pack_kernel,
        grid=(e // rows,),
        in_specs=[pl.BlockSpec((1, rows, d), lambda i: (layer, i, 0))],
        out_specs=pl.BlockSpec((rows * ROW_SUBLANES, 128), lambda i: (i, 0)),
        out_shape=jax.ShapeDtypeStruct((e * ROW_SUBLANES, 128), jnp.uint32),
        compiler_params=_params("parallel"),
        name="pack_expert_table",
    )(tables)


def _unpack_row(words):
    lo = lax.bitcast_convert_type(words << 16, F32)
    hi = lax.bitcast_convert_type(words & jnp.uint32(0xFFFF0000), F32)
    return lo, hi


def _load_table_once(tab_hbm, tab_vmem, sem):
    @pl.when(pl.program_id(0) == 0)
    def _():
        n = tab_hbm.shape[0]
        cp = pltpu.make_async_copy(tab_hbm, tab_vmem.at[pl.ds(TABLE_PAD, n), :], sem)
        cp.start()
        tab_vmem[0:TABLE_PAD, :] = jnp.zeros((TABLE_PAD, 128), jnp.uint32)
        tab_vmem[TABLE_PAD + n:2 * TABLE_PAD + n, :] = jnp.zeros((TABLE_PAD, 128), jnp.uint32)
        cp.wait()


def _stage_ids(idx_hbm, idx_smem, sems, count):
    step = pl.program_id(0)
    slot = step % 2

    def copy(s, buf):
        return pltpu.make_async_copy(idx_hbm.at[pl.ds(s * count, count)],
                                     idx_smem.at[pl.ds(buf * count, count)], sems.at[1 + buf])

    @pl.when(step == 0)
    def _():
        copy(0, 0).start()

    @pl.when(step + 1 < pl.num_programs(0))
    def _():
        copy(step + 1, 1 - slot).start()

    copy(step, slot).wait()
    return slot * count


def _peer_up_kernel(idx_hbm, x_ref, gate_ref, sel_ref, tab_hbm, o_ref, tab_ref, idx_smem, part_ref,
                    xx_ref, sems):
    rows = gate_ref.shape[0]
    _load_table_once(tab_hbm, tab_ref, sems.at[0])
    id0 = _stage_ids(idx_hbm, idx_smem, sems, rows * PEER_PICKS)
    half = PEER_HALF // 128
    batch = part_ref.shape[1] // 8
    r_id =lax.broadcasted_iota(jnp.int32, (8, PEER_PICKS), 0)
    c_id = lax.broadcasted_iota(jnp.int32, (8, PEER_PICKS), 1)
    row_pick = 4 * (r_id % 2) + 2 * ((r_id // 2) % 2) + (r_id // 4)
    diag = (c_id % 8 == row_pick)[None]
    sub_id = lax.broadcasted_iota(jnp.int32, (8, 128), 0)
    low2 = (sub_id % 4) < 2
    even = (sub_id % 2) == 0
    low = sub_id < ROW_SUBLANES
    group = 8
    unroll = 4

    def fold_rows(p0, p1, p2, p3):
        def half_fold(a, b, mask, s):
            x = jnp.where(mask, a, pltpu.roll(b, s, 0))
            y = jnp.where(mask, pltpu.roll(a, 8 - s, 0), b)
            return x + y
        return half_fold(half_fold(p0, p1, low2, 2), half_fold(p2, p3, low2, 2), even, 1)

    def token_batch(b, carry):
        base = pl.multiple_of(b * batch, batch)

        def token_group(tg, c):
            tg8 = pl.multiple_of(tg * group, group)
            id_refs = [idx_smem.at[pl.ds(pl.multiple_of(id0 + (base + tg8 + i) * PEER_PICKS, PEER_PICKS), PEER_PICKS)]
                       for i in range(group)]
            for i in range(group):
                x_lo = x_ref[base + tg8 + i, 0:half, :]
                x_hi = x_ref[base + tg8 + i, half:2 * half, :]
                xx_ref[i, 0:8, :] = jnp.concatenate([x_lo, x_lo], axis=0)
                xx_ref[i, 8:16, :] = jnp.concatenate([x_hi, x_hi], axis=0)

            def pick_step(k, cc):
                for u in range(unroll):
                    ku = k * unroll + u
                    k8 = pl.multiple_of(ku * 8, 8)
                    o = [k8 + j for j in range(8)]
                    dst = part_ref.at[ku, pl.ds(pl.multiple_of(tg8 * 8, 8 * group), 8 * group), :]
                    for i in range(group):
                        x_lo2 = xx_ref[i, 0:8, :]
                        x_hi2 = xx_ref[i, 8:16, :]
                        tiles = []
                        for j in range(0, 8, 2):
                            ra, rb = id_refs[i][o[j]], id_refs[i][o[j + 1]]
                            words = jnp.where(low, tab_ref[pl.ds(ra, 8), :], tab_ref[pl.ds(rb - ROW_SUBLANES, 8), :])
                            lo, hi = _unpack_row(words)
                            tiles.append(lo * x_lo2 + hi * x_hi2)
                        dst[i * 8:(i + 1) * 8, :] = fold_rows(*tiles)
                return cc

            lax.fori_loop(0, PEER_PICKS // 8 // unroll, pick_step, 0)
            return c

        lax.fori_loop(0, batch // group, token_group, 0)
        sums = jnp.zeros((batch * 8, PEER_PICKS), F32)
        for k in range(PEER_PICKS // 8):
            part = part_ref[k]
            sums = sums + _dot(part.astype(BF16), sel_ref[k])
        act = jnp.sum(jnp.where(diag, sums.reshape(batch, 8, PEER_PICKS), 0.0), axis=1)
        o_ref[pl.ds(base, batch), :] = gate_ref[pl.ds(base, batch), :] * _gelu_tanh(act)
        return carry

    lax.fori_loop(0, rows // batch, token_batch, 0)


def _peer_down_kernel(idx_hbm, w_ref, ones_ref, tab_hbm, o_ref, tab_ref, idx_smem, splat_ref, sems):
    rows = w_ref.shape[0]
    _load_table_once(tab_hbm, tab_ref, sems.at[0])
    id0 = _stage_ids(idx_hbm, idx_smem, sems, rows * PEER_PICKS)
    half = PEER_HALF // 128
    batch = splat_ref.shape[0]
    group = 4
    unroll = 4
    low = lax.broadcasted_iota(jnp.int32, (8, 128), 0) < ROW_SUBLANES
    eye = (lax.broadcasted_iota(jnp.int32, (PEER_PICKS, PEER_PICKS), 0)
           == lax.broadcasted_iota(jnp.int32, (PEER_PICKS, PEER_PICKS), 1)).astype(F32)
    ones = ones_ref[...]

    def token_batch(b, carry):
        base = pl.multiple_of(b * batch, batch)
        for i in range(batch):
            d = eye * w_ref[pl.ds(base + i, 1), :]
            splat_ref[i] = _dot(d.astype(BF16), ones)

        def token_group(tg, c):
            tg8 = pl.multiple_of(tg * group, group)
            id_refs = [idx_smem.at[pl.ds(pl.multiple_of(id0 + (base + tg8 + i) * PEER_PICKS, PEER_PICKS), PEER_PICKS)]
                       for i in range(group)]

            def pick_step(k, acc):
                acc = list(acc)
                for u in range(unroll):
                    k8 = pl.multiple_of((k * unroll + u) * 8, 8)
                    o = [k8 + j for j in range(8)]
                    sp = splat_ref.at[pl.ds(tg8, group), pl.ds(k8, 8), :]
                    for i in range(group):
                        for j in range(0, 8, 2):
                            ra, rb = id_refs[i][o[j]], id_refs[i][o[j + 1]]
                            words = jnp.where(low, tab_ref[pl.ds(ra, 8), :], tab_ref[pl.ds(rb - ROW_SUBLANES, 8), :])
                            lo, hi = _unpack_row(words)
                            w = jnp.where(low, jnp.broadcast_to(sp[i, j:j + 1, :], (8, 128)),
                                          jnp.broadcast_to(sp[i, j + 1:j + 2, :], (8, 128)))
                            acc[2 * i] = acc[2 * i] + w * lo
                            acc[2 * i + 1] = acc[2 * i + 1] + w * hi
                return tuple(acc)

            zero = jnp.zeros((8, 128), F32)
            acc = lax.fori_loop(0, PEER_PICKS // 8 // unroll, pick_step, (zero,) * (2 * group))
            for i in range(group):
                o_ref[base + tg8 + i, 0:half, :] = acc[2 * i][0:half] + acc[2 * i][half:2 * half]
                o_ref[base + tg8 + i, half:2 * half, :] = acc[2 * i + 1][0:half] + acc[2 * i + 1][half:2 * half]
            return c

        lax.fori_loop(0, batch // group, token_group, 0)
        return carry

    lax.fori_loop(0, rows // batch, token_batch, 0)


def _peer_experts(hf, idx, gate, expert_u, expert_v, layer):
    t, d = hf.shape
    rows = min(EXPERT_ROWS, t)
    sel = np.zeros((PEER_PICKS // 8, 128, PEER_PICKS), np.float32)
    for j in range(PEER_PICKS // 8):
        sel[j, :, 8 * j:8 * j + 8] = 1.0
    sel = jnp.asarray(sel, BF16)
    tab_shape = (PEER_EXPERTS * ROW_SUBLANES + 2 * TABLE_PAD, 128)
    picks = pl.BlockSpec((rows, PEER_PICKS), lambda i: (i, 0))
    hbm = pl.BlockSpec(memory_space=pl.ANY)
    tok3 = pl.BlockSpec((rows, d // 128, 128), lambda i: (i, 0, 0))
    idx_flat = idx.reshape(t * PEER_PICKS)
    ids_smem = pltpu.SMEM((2 * rows * PEER_PICKS,), jnp.int32)
    w = pl.pallas_call(
        _peer_up_kernel,
        grid=(t // rows,),
        in_specs=[hbm, tok3, picks, _const_spec(sel.shape), hbm],
        out_specs=picks,
        out_shape=jax.ShapeDtypeStruct((t, PEER_PICKS), F32),
        scratch_shapes=[pltpu.VMEM(tab_shape, jnp.uint32), ids_smem,
                        pltpu.VMEM((PEER_PICKS // 8, UP_BATCH * 8, 128), F32),
                        pltpu.VMEM((8, 16, 128), F32), pltpu.SemaphoreType.DMA((3,))],
        compiler_params=_params("arbitrary"),
        name="peer_up",
    )(idx_flat, hf.reshape(t, d // 128, 128), gate, sel, _pack_table(expert_u, layer))
    out = pl.pallas_call(
        _peer_down_kernel,
        grid=(t // rows,),
        in_specs=[hbm, picks, _const_spec((PEER_PICKS, 128)), hbm],
        out_specs=tok3,
        out_shape=jax.ShapeDtypeStruct((t, d // 128, 128), F32),
        scratch_shapes=[pltpu.VMEM(tab_shape, jnp.uint32), ids_smem,
                        pltpu.VMEM((DOWN_BATCH, PEER_PICKS, 128), F32), pltpu.SemaphoreType.DMA((3,))],
        compiler_params=_params("arbitrary"),
        name="peer_down",
    )(idx_flat, w, jnp.ones((PEER_PICKS, 128), BF16), _pack_table(expert_v, layer))
    return out.reshape(t, d)


def _peer(hf, w_query, keys1, keys2, expert_u, expert_v, layer):
    idx, gate = _peer_route(hf, w_query, keys1, keys2)
    return _peer_experts(hf, idx, gate, expert_u, expert_v, layer)


def _pad_w_in(w_in):
    d = w_in.shape[0]
    outs = [jnp.zeros((d, PROJ_COLS), BF16), jnp.zeros((d, ATT_COLS), BF16)]
    src = 0
    for size, (arr, dst) in zip(SRC_SIZES, SRC_DST):
        outs[arr] = outs[arr].at[:, dst:dst + size].set(w_in[:, src:src + size].astype(BF16))
        src += size
    return outs


def _layer(x2d, pe2d, b, seq, norm_mix, w_in, att_q_gain, att_k_gain, w_a_out, dn_conv, dn_a_log, dn_dt_bias,
           dn_out_gain, w_b_out, s5_a_re, s5_a_im, s5_log_dt, s5_b_re, s5_b_im, s5_c_re, s5_c_im, s5_d,
           w_c_val, w_c_gate, w_out, norm_ffn, peer_w_query, peer_keys1, peer_keys2, peer_u, peer_v,
           ple_w, ple_norm, ple_w_gate, layer):
    w_main, w_att = _pad_w_in(w_in)
    proj2 = _norm_matmul(x2d, norm_mix, w_main, tm=2048, tn=512)
    proj3 = proj2.reshape(b, seq, PROJ_COLS)
    proj_att = _norm_matmul(x2d, norm_mix, w_att, tn=ATT_COLS).reshape(b, seq, ATT_COLS)
    att = [_attention_group(proj_att, att_q_gain, att_k_gain, g) for g in range(len(ATT_GROUPS))]
    ob = _deltanet(proj3, dn_conv, dn_a_log, dn_dt_bias, dn_out_gain)
    s = _s5(proj3, s5_a_re, s5_a_im, s5_log_dt, s5_b_re, s5_b_im, s5_c_re, s5_c_im, s5_d)
    x1, hf = _combine(x2d, proj2, att, ob, s, w_a_out, w_b_out, w_c_val, w_c_gate, w_out, norm_ffn)
    ffn = _peer(hf, peer_w_query, peer_keys1, peer_keys2, peer_u, peer_v, layer)
    return _ple(x1, ffn, pe2d, ple_norm, ple_w_gate, ple_w)


def kernel(x, p, norm_mix, w_in, att_q_gain, att_k_gain, w_a_out, dn_conv, dn_a_log, dn_dt_bias, dn_out_gain, w_b_out, s5_a_re, s5_a_im, s5_log_dt, s5_b_re, s5_b_im, s5_c_re, s5_c_im, s5_d, w_c_val, w_c_gate, w_out, norm_ffn, peer_w_query, peer_keys1, peer_keys2, peer_u, peer_v, ple_w, ple_norm, ple_w_gate):
    b, seq, d = x.shape
    depth = p.shape[0]
    x2d = x.reshape(b * seq, d)
    for i in range(depth):
        x2d = _layer(x2d, p[i].reshape(b * seq, PLE_DIM), b, seq, norm_mix[i], w_in[i], att_q_gain[i], att_k_gain[i],
                     w_a_out[i], dn_conv[i], dn_a_log[i], dn_dt_bias[i], dn_out_gain[i], w_b_out[i], s5_a_re[i],
                     s5_a_im[i], s5_log_dt[i], s5_b_re[i], s5_b_im[i], s5_c_re[i], s5_c_im[i], s5_d[i], w_c_val[i],
                     w_c_gate[i], w_out[i], norm_ffn[i], peer_w_query[i], peer_keys1[i], peer_keys2[i], peer_u,
                     peer_v, ple_w[i], ple_norm[i], ple_w_gate[i], layer=i)
    return x2d.reshape(b, seq, d)
```

```python
import functools

import numpy as np
import jax
import jax.numpy as jnp
from jax import lax
from jax.experimental import pallas as pl
from jax.experimental.pallas import tpu as pltpu

F32 = jnp.float32
BF16 = jnp.bfloat16

D_MODEL = 1024
NORM_EPS = 1e-6
ATT_GROUPS = ((128, 1), (512, 4), (2048, 16))
ATT_SLOTS = 4
ATT_HEAD_DIM = 64
ATT_WIDTH = 256
ATT_BLOCK = 128
ALIBI_MAX_EXP = 8.0
DN_HEADS = 4
DN_HEAD_DIM = 128
DN_WIDTH = 512
DN_CONV = 4
DN_CHUNK = 64
S5_GROUP = 16
S5_GROUPS = 16
S5_WIDTH = 256
S5_STATE = 64
PEER_HEADS = 8
PEER_KEYS = 128
PEER_TOPK = 16
PEER_KEY_DIM = 128
PLE_DIM = 256

OFF_QKV, OFF_Z, OFF_GATES, OFF_U, OFF_BA = 0, 1536, 2048, 5120, 5376
PROJ_COLS = 5632
OFF_QA = (0, 256, 1024)
OFF_KA, OFF_VA = 512, 768
ATT_COLS = 1280
ATT_SLICE = {0: (0, ATT_COLS), 1: (256, 1024), 2: (512, 1280)}
SRC_SIZES = (256, 256, 256, 256, 256, 1536, 512, 4, 4, 256, 3072)
SRC_DST = ((1, OFF_QA[0]), (1, OFF_QA[1]), (1, OFF_QA[2]), (1, OFF_KA), (1, OFF_VA), (0, OFF_QKV), (0, OFF_Z),
           (0, OFF_BA), (0, OFF_BA + 4), (0, OFF_U), (0, OFF_GATES))

VMEM_LIMIT = 56 * 1024 * 1024
NEG_BIG = -1e30


def _params(*sem):
    return pltpu.CompilerParams(dimension_semantics=sem, vmem_limit_bytes=VMEM_LIMIT)


def _const_spec(shape):
    nd = len(shape)
    return pl.BlockSpec(shape, lambda *_: (0,) * nd)


def _rms(x, gain):
    return x * lax.rsqrt(jnp.mean(x * x, axis=-1, keepdims=True) + NORM_EPS) * gain


def _sigmoid(x):
    return 1.0 / (1.0 + jnp.exp(-x))


def _gelu_tanh(x):
    return 0.5 * x * (1.0 + jnp.tanh(0.7978845608028654 * (x + 0.044715 * (x * x * x))))


def _dot(a, b, **kw):
    return jnp.dot(a, b, preferred_element_type=F32, **kw)


def _dot3(a, b):
    a_hi, b_hi = a.astype(BF16), b.astype(BF16)
    a_lo = (a - a_hi.astype(F32)).astype(BF16)
    b_lo = (b - b_hi.astype(F32)).astype(BF16)
    return _dot(a_hi, b_hi) + (_dot(a_hi, b_lo) + _dot(a_lo, b_hi))


def _dot_nt(a, b):
    return lax.dot_general(a, b, (((1,), (1,)), ((), ())), preferred_element_type=F32)


def _dot_tn(a, b):
    return lax.dot_general(a, b, (((0,), (0,)), ((), ())), preferred_element_type=F32)


def _norm_matmul_kernel(x_ref, g_ref, w_ref, o_ref, h_ref):
    @pl.when(pl.program_id(1) == 0)
    def _():
        h_ref[...] = _rms(x_ref[...], g_ref[...]).astype(BF16)

    o_ref[...] = _dot(h_ref[...], w_ref[...])


def _norm_matmul(x2d, gain, w_bf16, tm=1024, tn=1024):
    t, d = x2d.shape
    n = w_bf16.shape[1]
    tm = min(tm, t)
    tn = min(tn, n)
    return pl.pallas_call(
        _norm_matmul_kernel,
        grid=(t // tm, n // tn),
        in_specs=[
            pl.BlockSpec((tm, d), lambda i, j: (i, 0)),
            pl.BlockSpec((1, d), lambda i, j: (0, 0)),
            pl.BlockSpec((d, tn), lambda i, j: (0, j)),
        ],
        out_specs=pl.BlockSpec((tm, tn), lambda i, j: (i, j)),
        out_shape=jax.ShapeDtypeStruct((t, n), F32),
        scratch_shapes=[pltpu.VMEM((tm, d), BF16)],
        compiler_params=_params("parallel", "arbitrary"),
        name="norm_matmul",
    )(x2d, gain.reshape(1, d), w_bf16)


def _attn_kernel(q_ref, kp_ref, kc_ref, vp_ref, vc_ref, qg_ref, kg_ref, bd_ref, o_ref, lse_ref, *,
                 dilation, span, slopes):
    n = pl.program_id(2)
    bd = bd_ref[...]

    def head_norm(x, gain):
        x2 = x * x
        hi = x2.astype(BF16)
        lo = (x2 - hi.astype(F32)).astype(BF16)
        ss = _dot(hi, bd) + _dot(lo, bd)
        return x * lax.rsqrt(ss * (1.0 / ATT_HEAD_DIM) + NORM_EPS) * gain

    qpos = lax.broadcasted_iota(jnp.int32, (ATT_BLOCK, 2 * ATT_BLOCK), 0)
    kpos = lax.broadcasted_iota(jnp.int32, (ATT_BLOCK, 2 * ATT_BLOCK), 1)
    rel = qpos + ATT_BLOCK - kpos
    valid = (rel >= 0) & (rel <= span) & ((kpos >= ATT_BLOCK) | (n > 0))
    dist = (rel * dilation).astype(F32)
    lane = lax.broadcasted_iota(jnp.int32, (ATT_BLOCK, 128), 1)
    col_head = lax.broadcasted_iota(jnp.int32, (ATT_BLOCK, ATT_WIDTH), 1) // ATT_HEAD_DIM
    heads = range(ATT_SLOTS)

    q = head_norm(q_ref[0], qg_ref[...]) * (ATT_HEAD_DIM ** -0.5)
    k = head_norm(jnp.concatenate([kp_ref[0], kc_ref[0]], axis=0), kg_ref[...])
    v = jnp.concatenate([vp_ref[0], vc_ref[0]], axis=0)
    kb, vb = k.astype(BF16), v.astype(BF16)
    s = [_dot_nt(jnp.where(col_head == h, q, 0.0).astype(BF16), kb) for h in heads]
    s = [jnp.where(valid, s[h] - slopes[h] * dist, NEG_BIG) for h in heads]
    m = [jnp.max(s[h], axis=-1, keepdims=True) for h in heads]
    e = [jnp.exp(s[h] - m[h]) for h in heads]
    den = [jnp.sum(e[h], axis=-1, keepdims=True) for h in heads]
    pv = [_dot(e[h].astype(BF16), vb) for h in heads]
    out = jnp.zeros((ATT_BLOCK, ATT_WIDTH), F32)
    lse_all = jnp.zeros((ATT_BLOCK, 128), F32)
    for h in heads:
        out = jnp.where(col_head == h, pv[h] / den[h], out)
        lse_all = jnp.where(lane == h, m[h] + jnp.log(den[h]), lse_all)
    o_ref[0] = out
    lse_ref[0] = lse_all


def _attention_group(proj3, q_gain, k_gain, group):
    b, seq, _ = proj3.shape
    window, dilation = ATT_GROUPS[group]
    span = window // dilation
    sub = seq // dilation
    nb = sub // ATT_BLOCK
    lo, hi = ATT_SLICE[group]
    view = proj3[:, :, lo:hi].reshape(b, sub, dilation * (hi - lo))
    cb = (hi - lo) // ATT_WIDTH
    qc, kc, vc = ((off - lo) // ATT_WIDTH for off in (OFF_QA[group], OFF_KA, OFF_VA))
    n_heads = len(ATT_GROUPS) * ATT_SLOTS
    slopes = 2.0 ** (-ALIBI_MAX_EXP * np.arange(1, n_heads + 1) / n_heads)
    slopes = tuple(float(s) for s in slopes.reshape(len(ATT_GROUPS), ATT_SLOTS)[group])
    blk = (1, ATT_BLOCK, ATT_WIDTH)
    bd = (np.arange(ATT_WIDTH)[:, None] // ATT_HEAD_DIM == np.arange(ATT_WIDTH)[None, :] // ATT_HEAD_DIM)
    o, lse = pl.pallas_call(
        functools.partial(_attn_kernel, dilation=dilation, span=span, slopes=slopes),
        grid=(b, dilation, nb),
        in_specs=[
            pl.BlockSpec(blk, lambda i, r, n: (i, n, r * cb + qc)),
            pl.BlockSpec(blk, lambda i, r, n: (i, jnp.maximum(n - 1, 0), r * cb + kc)),
            pl.BlockSpec(blk, lambda i, r, n: (i, n, r * cb + kc)),
            pl.BlockSpec(blk, lambda i, r, n: (i, jnp.maximum(n - 1, 0), r * cb + vc)),
            pl.BlockSpec(blk, lambda i, r, n: (i, n, r * cb + vc)),
            _const_spec((1, ATT_WIDTH)),
            _const_spec((1, ATT_WIDTH)),
            _const_spec((ATT_WIDTH, ATT_WIDTH)),
        ],
        out_specs=[
            pl.BlockSpec(blk, lambda i, r, n: (i, n, r)),
            pl.BlockSpec((1, ATT_BLOCK, 128), lambda i, r, n: (i, n, r)),
        ],
        out_shape=[
            jax.ShapeDtypeStruct((b, sub, dilation * ATT_WIDTH), F32),
            jax.ShapeDtypeStruct((b, sub, dilation * 128), F32),
        ],
        compiler_params=_params("parallel", "parallel", "arbitrary"),
        name=f"dilated_attention_g{group}",
    )(view, view, view, view, view,
      jnp.tile(q_gain, ATT_SLOTS).reshape(1, ATT_WIDTH), jnp.tile(k_gain, ATT_SLOTS).reshape(1, ATT_WIDTH),
      jnp.asarray(bd, BF16))
    return o.reshape(b * seq, ATT_WIDTH), lse.reshape(b * seq, 128)


DN_ROWS = 256


def _deltanet_kernel(qkv_ref, z_ref, ba_ref, cw_ref, alog_ref, dtb_ref, og_ref, tril_ref, o_ref,
                     state_ref, tail_ref):
    c = DN_CHUNK
    rows = qkv_ref.shape[1]

    @pl.when(pl.program_id(1) == 0)
    def _():
        state_ref[...] = jnp.zeros_like(state_ref)
        tail_ref[...] = jnp.zeros_like(tail_ref)

    x = qkv_ref[0]
    xc = jnp.concatenate([tail_ref[...], x], axis=0)
    cw = cw_ref[...]
    acc = cw[DN_CONV - 1:DN_CONV] * x
    for j in range(DN_CONV - 1):
        acc = acc + cw[j:j + 1] * xc[5 + j:5 + j + rows]
    tail_ref[...] = x[rows - 8:rows]
    a = acc * _sigmoid(acc)

    ba = ba_ref[0]
    beta_all = _sigmoid(ba)
    sp = ba + dtb_ref[...]
    sp = jnp.maximum(sp, 0.0) + jnp.log(1.0 + jnp.exp(-jnp.abs(sp)))
    g_all = -(jnp.exp(alog_ref[...]) * sp)
    tril = tril_ref[...]
    ii = lax.broadcasted_iota(jnp.int32, (c, c), 0)
    jj = lax.broadcasted_iota(jnp.int32, (c, c), 1)
    incl = ii >= jj
    strict = ii > jj
    eye = (ii == jj).astype(F32)
    z = z_ref[0]
    og = og_ref[...]

    n_chunks = rows // c
    units = [(h, ci) for ci in range(n_chunks) for h in range(DN_HEADS)]

    loc = {}
    for h in range(DN_HEADS):
        hs = slice(h * DN_HEAD_DIM, (h + 1) * DN_HEAD_DIM)
        qh = a[:, hs]
        kh = a[:, DN_WIDTH + h * DN_HEAD_DIM:DN_WIDTH + (h + 1) * DN_HEAD_DIM]
        vh = a[:, 2 * DN_WIDTH + h * DN_HEAD_DIM:2 * DN_WIDTH + (h + 1) * DN_HEAD_DIM]
        qh = qh * lax.rsqrt(jnp.sum(qh * qh, axis=-1, keepdims=True) + NORM_EPS) * (DN_HEAD_DIM ** -0.5)
        kh = kh * lax.rsqrt(jnp.sum(kh * kh, axis=-1, keepdims=True) + NORM_EPS)
        beta = beta_all[:, h:h + 1]
        gh = jnp.broadcast_to(g_all[:, DN_HEADS + h:DN_HEADS + h + 1], (rows, DN_HEAD_DIM))
        for ci in range(n_chunks):
            rs = slice(ci * c, (ci + 1) * c)
            g_hi = gh[rs].astype(BF16)
            g_lo = (gh[rs] - g_hi.astype(F32)).astype(BF16)
            gc = _dot(tril, g_hi) + _dot(tril, g_lo)
            gc64 = gc[:, :c]
            diff = gc64 - gc64.T
            decay = jnp.where(incl, jnp.exp(jnp.where(incl, diff, 0.0)), 0.0)
            eg = jnp.exp(gc)
            g_last = gc[c - 1:c, :]
            kc_, qc_, vc_, bc_ = kh[rs], qh[rs], vh[rs], beta[rs]
            kbeta = kc_ * bc_
            kcb = kc_.astype(BF16)
            loc[h, ci] = dict(
                lower=jnp.where(strict, _dot_nt(kbeta.astype(BF16), kcb) * decay, 0.0),
                vb=(vc_ * bc_).astype(BF16), kbe=(kbeta * eg).astype(BF16),
                attn=jnp.where(incl, _dot_nt(qc_.astype(BF16), kcb) * decay, 0.0).astype(BF16),
                q_dec=(qc_ * eg).astype(BF16), k_dec=(kc_ * jnp.exp(g_last - gc)).astype(BF16),
                g_end=jnp.exp(g_last))

    t_inv = {k: eye - loc[k]["lower"] for k in units}
    pw = {k: _dot3(loc[k]["lower"], loc[k]["lower"]) for k in units}
    for it in range(5):
        t_inv = {k: t_inv[k] + _dot3(t_inv[k], pw[k]) for k in units}
        if it < 4:
            pw = {k: _dot3(pw[k], pw[k]) for k in units}
    uw = {}
    for k in units:
        t_b = t_inv[k].astype(BF16)
        uw[k] = (_dot(t_b, loc[k]["vb"]), _dot(t_b, loc[k]["kbe"]).astype(BF16))

    state = [state_ref[h] for h in range(DN_HEADS)]
    o_chunks = [[] for _ in range(DN_HEADS)]
    for ci in range(n_chunks):
        for h in range(DN_HEADS):
            d = loc[h, ci]
            u, w = uw[h, ci]
            sb = state[h].astype(BF16)
            vnb = (u - _dot(w, sb)).astype(BF16)
            o_chunks[h].append(_dot(d["q_dec"], sb) + _dot(d["attn"], vnb))
            state[h] = state[h] * d["g_end"] + _dot_tn(d["k_dec"], vnb)
    outs = []
    for h in range(DN_HEADS):
        state_ref[h] = state[h]
        zh = z[:, h * DN_HEAD_DIM:(h + 1) * DN_HEAD_DIM]
        outs.append(_rms(jnp.concatenate(o_chunks[h], axis=0), og) * (zh * _sigmoid(zh)))
    o_ref[0] = jnp.concatenate(outs, axis=-1)


def _deltanet(proj3, conv_w, a_log, dt_bias, out_gain):
    b, seq, _ = proj3.shape
    rows = min(DN_ROWS, seq)
    pad_a = jnp.zeros((1, 128), F32).at[0, DN_HEADS:2 * DN_HEADS].set(a_log)
    pad_d = jnp.zeros((1, 128), F32).at[0, DN_HEADS:2 * DN_HEADS].set(dt_bias)
    tril = jnp.asarray(np.tril(np.ones((DN_CHUNK, DN_CHUNK), np.float32)), BF16)
    out = pl.pallas_call(
        _deltanet_kernel,
        grid=(b, seq // rows),
        in_specs=[
            pl.BlockSpec((1, rows, 3 * DN_WIDTH), lambda i, t: (i, t, OFF_QKV // (3 * DN_WIDTH))),
            pl.BlockSpec((1, rows, DN_WIDTH), lambda i, t: (i, t, OFF_Z // DN_WIDTH)),
            pl.BlockSpec((1, rows, 128), lambda i, t: (i, t, OFF_BA // 128)),
            _const_spec((DN_CONV, 3 * DN_WIDTH)),
            _const_spec((1, 128)),
            _const_spec((1, 128)),
            _const_spec((1, DN_HEAD_DIM)),
            _const_spec((DN_CHUNK, DN_CHUNK)),
        ],
        out_specs=pl.BlockSpec((1, rows, DN_WIDTH), lambda i, t: (i, t, 0)),
        out_shape=jax.ShapeDtypeStruct((b, seq, DN_WIDTH), F32),
        scratch_shapes=[pltpu.VMEM((DN_HEADS, DN_HEAD_DIM, DN_HEAD_DIM), F32), pltpu.VMEM((8, 3 * DN_WIDTH), F32)],
        compiler_params=_params("parallel", "arbitrary"),
        name="gated_deltanet",
    )(proj3, proj3, proj3, conv_w, pad_a, pad_d, out_gain.reshape(1, DN_HEAD_DIM), tril)
    return out.reshape(b * seq, DN_WIDTH)


S5_ROWS = 128
S5_N = S5_GROUPS * S5_STATE


def _s5_kernel(u_ref, bbd_ref, pinv_r_ref, pinv_i_ref, pow_r_ref, pow_i_ref, a1_r_ref, a1_i_ref,
               cr_ref, ci_ref, d_ref, tril_ref, o_ref, sr_ref, si_ref):
    @pl.when(pl.program_id(1) == 0)
    def _():
        sr_ref[...] = jnp.zeros_like(sr_ref)
        si_ref[...] = jnp.zeros_like(si_ref)

    u = u_ref[0]
    bu = _dot(u.astype(BF16), bbd_ref[...])
    br, bi = bu[:, :S5_N], bu[:, S5_N:]
    pr, pi = pinv_r_ref[...], pinv_i_ref[...]
    zr = pr * br - pi * bi
    zi = pr * bi + pi * br
    tril = tril_ref[...]

    def cumsum_rows(zz):
        hi = zz.astype(BF16)
        lo = (zz - hi.astype(F32)).astype(BF16)
        return _dot(tril, hi) + _dot(tril, lo)

    cr = cumsum_rows(zr)
    ci = cumsum_rows(zi)
    wr, wi = pow_r_ref[...], pow_i_ref[...]
    a_r, a_i = a1_r_ref[...], a1_i_ref[...]
    sr, si = sr_ref[...], si_ref[...]
    cr = cr + (a_r * sr - a_i * si)
    ci = ci + (a_r * si + a_i * sr)
    xr = wr * cr - wi * ci
    xi = wr * ci + wi * cr
    rows = xr.shape[0]
    sr_ref[...] = xr[rows - 1:rows]
    si_ref[...] = xi[rows - 1:rows]
    y = _dot(xr.astype(BF16), cr_ref[...]) - _dot(xi.astype(BF16), ci_ref[...]) + d_ref[...] * u
    o_ref[0] = _gelu_tanh(y)


def _s5(proj3, a_re, a_im, log_dt, b_re, b_im, c_re, c_im, d_skip):
    b, seq, _ = proj3.shape
    rows = min(S5_ROWS, seq)
    g, n, p = S5_GROUPS, S5_STATE, S5_GROUP
    dt = jnp.exp(log_dt)[:, None]
    lr, li = a_re, a_im
    mag = jnp.exp(lr * dt)
    ab_r, ab_i = mag * jnp.cos(li * dt), mag * jnp.sin(li * dt)
    den = lr * lr + li * li
    nr, ni = ab_r - 1.0, ab_i
    cr, ci = (nr * lr + ni * li) / den, (ni * lr - nr * li) / den
    bb_r = cr[..., None] * b_re - ci[..., None] * b_im
    bb_i = cr[..., None] * b_im + ci[..., None] * b_re
    eye_g = jnp.eye(g, dtype=F32)
    bbd_r = jnp.einsum('gnp,gh->gphn', bb_r, eye_g).reshape(g * p, g * n)
    bbd_i = jnp.einsum('gnp,gh->gphn', bb_i, eye_g).reshape(g * p, g * n)
    bbd = jnp.concatenate([bbd_r, bbd_i], axis=1).astype(BF16)
    cbd_r = jnp.einsum('gpn,gh->gnhp', c_re, eye_g).reshape(g * n, g * p).astype(BF16)
    cbd_i = jnp.einsum('gpn,gh->gnhp', c_im, eye_g).reshape(g * n, g * p).astype(BF16)
    steps = jnp.arange(rows, dtype=F32)[:, None]
    lrd = (lr * dt).reshape(1, g * n)
    lid = (li * dt).reshape(1, g * n)

    def power(k):
        m = jnp.exp(lrd * k)
        return m * jnp.cos(lid * k), m * jnp.sin(lid * k)

    pinv_r, pinv_i = power(-steps)
    pow_r, pow_i = power(steps)
    a1_r, a1_i = power(1.0)
    tril = jnp.asarray(np.tril(np.ones((rows, rows), np.float32)), BF16)
    tab = _const_spec((rows, g * n))
    out = pl.pallas_call(
        _s5_kernel,
        grid=(b, seq // rows),
        in_specs=[
            pl.BlockSpec((1, rows, S5_WIDTH), lambda i, t: (i, t, OFF_U // S5_WIDTH)),
            _const_spec((S5_WIDTH, 2 * g * n)),
            tab, tab, tab, tab, _const_spec((1, g * n)), _const_spec((1, g * n)),
            _const_spec((g * n, S5_WIDTH)),
            _const_spec((g * n, S5_WIDTH)),
            _const_spec((1, S5_WIDTH)),
            _const_spec((rows, rows)),
        ],
        out_specs=pl.BlockSpec((1, rows, S5_WIDTH), lambda i, t: (i, t, 0)),
        out_shape=jax.ShapeDtypeStruct((b, seq, S5_WIDTH), F32),
        scratch_shapes=[pltpu.VMEM((1, g * n), F32), pltpu.VMEM((1, g * n), F32)],
        compiler_params=_params("parallel", "arbitrary"),
        name="s5_mixer",
    )(proj3, bbd, pinv_r, pinv_i, pow_r, pow_i, a1_r, a1_i, cbd_r, cbd_i, d_skip.reshape(1, S5_WIDTH), tril)
    return out.reshape(b * seq, S5_WIDTH)


def _combine_kernel(x_ref, ga_ref, gb_ref, gc_ref, o1_ref, o2_ref, o3_ref, l1_ref, l2_ref, l3_ref,
                    ob_ref, s_ref, wa_ref, wb_ref, wcv_ref, wcg_ref, wo_ref, ex_ref, nf_ref,
                    x1_ref, hf_ref):
    l1, l2, l3 = l1_ref[...], l2_ref[...], l3_ref[...]
    m = jnp.maximum(jnp.maximum(l1, l2), l3)
    e1, e2, e3 = jnp.exp(l1 - m), jnp.exp(l2 - m), jnp.exp(l3 - m)
    inv = 1.0 / (e1 + e2 + e3)
    ex = ex_ref[...]

    def expand(wt):
        hi = wt.astype(BF16)
        lo = (wt - hi.astype(F32)).astype(BF16)
        return _dot(hi, ex) + _dot(lo, ex)

    mix = expand(e1 * inv) * o1_ref[...] + expand(e2 * inv) * o2_ref[...] + expand(e3 * inv) * o3_ref[...]
    y_a = _dot(mix.astype(BF16), wa_ref[...])
    y_b = _dot(ob_ref[...].astype(BF16), wb_ref[...])
    sb = s_ref[...].astype(BF16)
    y_c = _dot(sb, wcv_ref[...]) * _sigmoid(_dot(sb, wcg_ref[...]))
    mixed = _sigmoid(ga_ref[...]) * y_a + _sigmoid(gb_ref[...]) * y_b + _sigmoid(gc_ref[...]) * y_c
    x1 = x_ref[...] + _dot(mixed.astype(BF16), wo_ref[...])
    x1_ref[...] = x1
    hf_ref[...] = _rms(x1, nf_ref[...])


def _combine(x2d, proj2, att, ob, s, w_a_out, w_b_out, w_c_val, w_c_gate, w_out, norm_ffn, tm=512):
    t, d = x2d.shape
    tm = min(tm, t)
    (o1, l1), (o2, l2), (o3, l3) = att
    ex = np.zeros((128, ATT_WIDTH), np.float32)
    for h in range(ATT_SLOTS):
        ex[h, h * ATT_HEAD_DIM:(h + 1) * ATT_HEAD_DIM] = 1.0
    row = lambda w: pl.BlockSpec((tm, w), lambda i: (i, 0))
    gate = lambda k: pl.BlockSpec((tm, d), lambda i: (i, OFF_GATES // d + k))
    bf = lambda w: w.astype(BF16)
    return pl.pallas_call(
        _combine_kernel,
        grid=(t // tm,),
        in_specs=[
            row(d), gate(0), gate(1), gate(2),
            row(ATT_WIDTH), row(ATT_WIDTH), row(ATT_WIDTH), row(128), row(128), row(128),
            row(DN_WIDTH), row(S5_WIDTH),
            _const_spec((ATT_WIDTH, d)), _const_spec((DN_WIDTH, d)), _const_spec((S5_WIDTH, d)),
            _const_spec((S5_WIDTH, d)), _const_spec((d, d)), _const_spec((128, ATT_WIDTH)), _const_spec((1, d)),
        ],
        out_specs=[row(d), row(d)],
        out_shape=[jax.ShapeDtypeStruct((t, d), F32), jax.ShapeDtypeStruct((t, d), F32)],
        compiler_params=_params("parallel"),
        name="branch_combine",
    )(x2d, proj2, proj2, proj2, o1, o2, o3, l1, l2, l3, ob, s,
      bf(w_a_out), bf(w_b_out), bf(w_c_val), bf(w_c_gate), bf(w_out), jnp.asarray(ex, BF16), norm_ffn.reshape(1, d))


def _ple_kernel(x_ref, f_ref, pe_ref, pn_ref, wg_ref, pw_ref, o_ref):
    x2 = x_ref[...] + f_ref[...]
    gate = _sigmoid(_dot(_rms(x2, pn_ref[...]).astype(BF16), wg_ref[...]))
    o_ref[...] = x2 + gate * _dot(pe_ref[...].astype(BF16), pw_ref[...])


def _ple(x1, ffn, pe2d, ple_norm, ple_w_gate, ple_w, tm=512):
    t, d = x1.shape
    tm = min(tm, t)
    row = lambda w: pl.BlockSpec((tm, w), lambda i: (i, 0))
    return pl.pallas_call(
        _ple_kernel,
        grid=(t // tm,),
        in_specs=[row(d), row(d), row(PLE_DIM), _const_spec((1, d)), _const_spec((d, d)), _const_spec((PLE_DIM, d))],
        out_specs=row(d),
        out_shape=jax.ShapeDtypeStruct((t, d), F32),
        compiler_params=_params("parallel"),
        name="ple_residual",
    )(x1, ffn, pe2d, ple_norm.reshape(1, d), ple_w_gate.astype(BF16), ple_w.astype(BF16))


PEER_PICKS = PEER_HEADS * PEER_TOPK
PEER_EXPERTS = PEER_KEYS * PEER_KEYS
PEER_HALF = D_MODEL // 2
ROW_SUBLANES = PEER_HALF // 128
TABLE_PAD = 8
ROUTE_ROWS = 128
EXPERT_ROWS = 256
UP_BATCH = 32
DOWN_BATCH = 16


STAIR_COUNTS = tuple(PEER_TOPK // (a + 1) for a in range(PEER_TOPK))
STAIR_ROWS = -(-sum(STAIR_COUNTS) // 8) * 8
POS_PAD = 1 << 20


def _top_rows(work, row_id, k, big):
    vals, ids = [], []
    for _ in range(k):
        m = jnp.max(work, axis=0, keepdims=True)
        pos = jnp.min(jnp.where(work == m, row_id, big), axis=0, keepdims=True)
        vals.append(m)
        ids.append(pos)
        work = jnp.where(row_id == pos, -jnp.inf, work)
    return vals, ids


def _peer_route_kernel(hf_ref, wq_ref, k1_ref, k2_ref, pos_ref, idx_ref, gate_ref):
    rows = hf_ref.shape[0]
    qry = _dot(hf_ref[...].astype(BF16), wq_ref[...]).astype(BF16)
    k1, k2 = k1_ref[...], k2_ref[...]
    key_id = lax.broadcasted_iota(jnp.int32, (PEER_KEYS, rows), 0).astype(F32)
    cand_pos = pos_ref[...]
    pad = STAIR_ROWS - sum(STAIR_COUNTS)
    idx_rows, gate_rows = [], []
    for h in range(PEER_HEADS):
        q1 = qry[:, (2 * h) * PEER_KEY_DIM:(2 * h + 1) * PEER_KEY_DIM]
        q2 = qry[:, (2 * h + 1) * PEER_KEY_DIM:(2 * h + 2) * PEER_KEY_DIM]
        v1, i1 = _top_rows(_dot_nt(k1, q1), key_id, PEER_TOPK, float(PEER_KEYS))
        v2, i2 = _top_rows(_dot_nt(k2, q2), key_id, PEER_TOPK, float(PEER_KEYS))
        v2c = jnp.concatenate(v2, axis=0)
        i2c = jnp.concatenate(i2, axis=0)
        cand_s = jnp.concatenate([v1[a] + v2c[0:n] for a, n in enumerate(STAIR_COUNTS)]
                                 + [jnp.full((pad, rows), -jnp.inf, F32)], axis=0)
        cand_i = jnp.concatenate([i1[a] * PEER_KEYS + i2c[0:n] for a, n in enumerate(STAIR_COUNTS)]
                                 + [jnp.zeros((pad, rows), F32)], axis=0)
        top_s, pos = _top_rows(cand_s, cand_pos, PEER_TOPK, float(POS_PAD))
        e = [jnp.exp(s - top_s[0]) for s in top_s]
        den = e[0]
        for x in e[1:]:
            den = den + x
        inv = 1.0 / den
        for s, p in zip(e, pos):
            gate_rows.append(s * inv)
            idx_rows.append(jnp.sum(jnp.where(cand_pos == p, cand_i, 0.0), axis=0, keepdims=True))
    idx_ref[...] = (jnp.concatenate(idx_rows, axis=0).T * ROW_SUBLANES + TABLE_PAD).astype(jnp.int32)
    gate_ref[...] = jnp.concatenate(gate_rows, axis=0).T


def _peer_route(hf, w_query, keys1, keys2):
    t, d = hf.shape
    rows = min(ROUTE_ROWS, t)
    nq = w_query.shape[1]
    flat_pos = [a * PEER_TOPK + b for a, n in enumerate(STAIR_COUNTS) for b in range(n)]
    flat_pos += [POS_PAD] * (STAIR_ROWS - len(flat_pos))
    cand_pos = jnp.asarray(np.tile(np.asarray(flat_pos, np.float32)[:, None], (1, rows)))
    return pl.pallas_call(
        _peer_route_kernel,
        grid=(t // rows,),
        in_specs=[pl.BlockSpec((rows, d), lambda i: (i, 0)), _const_spec((d, nq)),
                  _const_spec((PEER_KEYS, PEER_KEY_DIM)), _const_spec((PEER_KEYS, PEER_KEY_DIM)),
                  _const_spec((STAIR_ROWS, rows))],
        out_specs=[pl.BlockSpec((rows, PEER_PICKS), lambda i: (i, 0)), pl.BlockSpec((rows, PEER_PICKS), lambda i: (i, 0))],
        out_shape=[jax.ShapeDtypeStruct((t, PEER_PICKS), jnp.int32), jax.ShapeDtypeStruct((t, PEER_PICKS), F32)],
        compiler_params=_params("parallel"),
        name="peer_route",
    )(hf, w_query.astype(BF16), keys1.astype(BF16), keys2.astype(BF16), cand_pos)


PACK_ROWS = 512


def _pack_kernel(t_ref, o_ref):
    x = t_ref[0]
    lo = lax.bitcast_convert_type(x[:, :PEER_HALF].astype(BF16).astype(F32), jnp.uint32) >> 16
    hi = lax.bitcast_convert_type(x[:, PEER_HALF:].astype(BF16).astype(F32), jnp.uint32)
    words = hi | lo
    for c in range(ROW_SUBLANES):
        o_ref[pl.ds(c, x.shape[0], stride=ROW_SUBLANES), :] = words[:, c * 128:(c + 1) * 128]


def _pack_table(tables, layer):
    _, e, d = tables.shape
    rows = min(PACK_ROWS, e)
    return pl.pallas_call(
        _pack_kernel,
        grid=(e // rows,),
        in_specs=[pl.BlockSpec((1, rows, d), lambda i: (layer, i, 0))],
        out_specs=pl.BlockSpec((rows * ROW_SUBLANES, 128), lambda i: (i, 0)),
        out_shape=jax.ShapeDtypeStruct((e * ROW_SUBLANES, 128), jnp.uint32),
        compiler_params=_params("parallel"),
        name="pack_expert_table",
    )(tables)


def _unpack_row(words):
    lo = lax.bitcast_convert_type(words << 16, F32)
    hi = lax.bitcast_convert_type(words & jnp.uint32(0xFFFF0000), F32)
    return lo, hi


def _load_table_once(tab_hbm, tab_vmem, sem):
    @pl.when(pl.program_id(0) == 0)
    def _():
        n = tab_hbm.shape[0]
        cp = pltpu.make_async_copy(tab_hbm, tab_vmem.at[pl.ds(TABLE_PAD, n), :], sem)
        cp.start()
        tab_vmem[0:TABLE_PAD, :] = jnp.zeros((TABLE_PAD, 128), jnp.uint32)
        tab_vmem[TABLE_PAD + n:2 * TABLE_PAD + n, :] = jnp.zeros((TABLE_PAD, 128), jnp.uint32)
        cp.wait()


def _stage_ids(idx_hbm, idx_smem, sems, count):
    step = pl.program_id(0)
    slot = step % 2

    def copy(s, buf):
        return pltpu.make_async_copy(idx_hbm.at[pl.ds(s * count, count)],
                                     idx_smem.at[pl.ds(buf * count, count)], sems.at[1 + buf])

    @pl.when(step == 0)
    def _():
        copy(0, 0).start()

    @pl.when(step + 1 < pl.num_programs(0))
    def _():
        copy(step + 1, 1 - slot).start()

    copy(step, slot).wait()
    return slot * count


def _peer_up_kernel(idx_hbm, x_ref, gate_ref, sel_ref, tab_hbm, o_ref, tab_ref, idx_smem, part_ref,
                    xx_ref, sems):
    rows = gate_ref.shape[0]
    _load_table_once(tab_hbm, tab_ref, sems.at[0])
    id0 = _stage_ids(idx_hbm, idx_smem, sems, rows * PEER_PICKS)
    half = PEER_HALF // 128
    batch = part_ref.shape[1] // 8
    r_id =lax.broadcasted_iota(jnp.int32, (8, PEER_PICKS), 0)
    c_id = lax.broadcasted_iota(jnp.int32, (8, PEER_PICKS), 1)
    row_pick = 4 * (r_id % 2) + 2 * ((r_id // 2) % 2) + (r_id // 4)
    diag = (c_id % 8 == row_pick)[None]
    sub_id = lax.broadcasted_iota(jnp.int32, (8, 128), 0)
    low2 = (sub_id % 4) < 2
    even = (sub_id % 2) == 0
    low = sub_id < ROW_SUBLANES
    group = 8
    unroll = 4

    def fold_rows(p0, p1, p2, p3):
        def half_fold(a, b, mask, s):
            x = jnp.where(mask, a, pltpu.roll(b, s, 0))
            y = jnp.where(mask, pltpu.roll(a, 8 - s, 0), b)
            return x + y
        return half_fold(half_fold(p0, p1, low2, 2), half_fold(p2, p3, low2, 2), even, 1)

    def token_batch(b, carry):
        base = pl.multiple_of(b * batch, batch)

        def token_group(tg, c):
            tg8 = pl.multiple_of(tg * group, group)
            id_refs = [idx_smem.at[pl.ds(pl.multiple_of(id0 + (base + tg8 + i) * PEER_PICKS, PEER_PICKS), PEER_PICKS)]
                       for i in range(group)]
            for i in range(group):
                x_lo = x_ref[base + tg8 + i, 0:half, :]
                x_hi = x_ref[base + tg8 + i, half:2 * half, :]
                xx_ref[i, 0:8, :] = jnp.concatenate([x_lo, x_lo], axis=0)
                xx_ref[i, 8:16, :] = jnp.concatenate([x_hi, x_hi], axis=0)

            def pick_step(k, cc):
                for u in range(unroll):
                    ku = k * unroll + u
                    k8 = pl.multiple_of(ku * 8, 8)
                    o = [k8 + j for j in range(8)]
                    dst = part_ref.at[ku, pl.ds(pl.multiple_of(tg8 * 8, 8 * group), 8 * group), :]
                    for i in range(group):
                        x_lo2 = xx_ref[i, 0:8, :]
                        x_hi2 = xx_ref[i, 8:16, :]
                        tiles = []
                        for j in range(0, 8, 2):
                            ra, rb = id_refs[i][o[j]], id_refs[i][o[j + 1]]
                            words = jnp.where(low, tab_ref[pl.ds(ra, 8), :], tab_ref[pl.ds(rb - ROW_SUBLANES, 8), :])
                            lo, hi = _unpack_row(words)
                            tiles.append(lo * x_lo2 + hi * x_hi2)
                        dst[i * 8:(i + 1) * 8, :] = fold_rows(*tiles)
                return cc

            lax.fori_loop(0, PEER_PICKS // 8 // unroll, pick_step, 0)
            return c

        lax.fori_loop(0, batch // group, token_group, 0)
        sums = jnp.zeros((batch * 8, PEER_PICKS), F32)
        for k in range(PEER_PICKS // 8):
            part = part_ref[k]
            sums = sums + _dot(part.astype(BF16), sel_ref[k])
        act = jnp.sum(jnp.where(diag, sums.reshape(batch, 8, PEER_PICKS), 0.0), axis=1)
        o_ref[pl.ds(base, batch), :] = gate_ref[pl.ds(base, batch), :] * _gelu_tanh(act)
        return carry

    lax.fori_loop(0, rows // batch, token_batch, 0)


def _peer_down_kernel(idx_hbm, w_ref, ones_ref, tab_hbm, o_ref, tab_ref, idx_smem, splat_ref, sems):
    rows = w_ref.shape[0]
    _load_table_once(tab_hbm, tab_ref, sems.at[0])
    id0 = _stage_ids(idx_hbm, idx_smem, sems, rows * PEER_PICKS)
    half = PEER_HALF // 128
    batch = splat_ref.shape[0]
    group = 4
    unroll = 4
    low = lax.broadcasted_iota(jnp.int32, (8, 128), 0) < ROW_SUBLANES
    eye = (lax.broadcasted_iota(jnp.int32, (PEER_PICKS, PEER_PICKS), 0)
           == lax.broadcasted_iota(jnp.int32, (PEER_PICKS, PEER_PICKS), 1)).astype(F32)
    ones = ones_ref[...]

    def token_batch(b, carry):
        base = pl.multiple_of(b * batch, batch)
        for i in range(batch):
            d = eye * w_ref[pl.ds(base + i, 1), :]
            splat_ref[i] = _dot(d.astype(BF16), ones)

        def token_group(tg, c):
            tg8 = pl.multiple_of(tg * group, group)
            id_refs = [idx_smem.at[pl.ds(pl.multiple_of(id0 + (base + tg8 + i) * PEER_PICKS, PEER_PICKS), PEER_PICKS)]
                       for i in range(group)]

            def pick_step(k, acc):
                acc = list(acc)
                for u in range(unroll):
                    k8 = pl.multiple_of((k * unroll + u) * 8, 8)
                    o = [k8 + j for j in range(8)]
                    sp = splat_ref.at[pl.ds(tg8, group), pl.ds(k8, 8), :]
                    for i in range(group):
                        for j in range(0, 8, 2):
                            ra, rb = id_refs[i][o[j]], id_refs[i][o[j + 1]]
                            words = jnp.where(low, tab_ref[pl.ds(ra, 8), :], tab_ref[pl.ds(rb - ROW_SUBLANES, 8), :])
                            lo, hi = _unpack_row(words)
                            w = jnp.where(low, jnp.broadcast_to(sp[i, j:j + 1, :], (8, 128)),
                                          jnp.broadcast_to(sp[i, j + 1:j + 2, :], (8, 128)))
                            acc[2 * i] = acc[2 * i] + w * lo
                            acc[2 * i + 1] = acc[2 * i + 1] + w * hi
                return tuple(acc)

            zero = jnp.zeros((8, 128), F32)
            acc = lax.fori_loop(0, PEER_PICKS // 8 // unroll, pick_step, (zero,) * (2 * group))
            for i in range(group):
                o_ref[base + tg8 + i, 0:half, :] = acc[2 * i][0:half] + acc[2 * i][half:2 * half]
                o_ref[base + tg8 + i, half:2 * half, :] = acc[2 * i + 1][0:half] + acc[2 * i + 1][half:2 * half]
            return c

        lax.fori_loop(0, batch // group, token_group, 0)
        return carry

    lax.fori_loop(0, rows // batch, token_batch, 0)


def _peer_experts(hf, idx, gate, expert_u, expert_v, layer):
    t, d = hf.shape
    rows = min(EXPERT_ROWS, t)
    sel = np.zeros((PEER_PICKS // 8, 128, PEER_PICKS), np.float32)
    for j in range(PEER_PICKS // 8):
        sel[j, :, 8 * j:8 * j + 8] = 1.0
    sel = jnp.asarray(sel, BF16)
    tab_shape = (PEER_EXPERTS * ROW_SUBLANES + 2 * TABLE_PAD, 128)
    picks = pl.BlockSpec((rows, PEER_PICKS), lambda i: (i, 0))
    hbm = pl.BlockSpec(memory_space=pl.ANY)
    tok3 = pl.BlockSpec((rows, d // 128, 128), lambda i: (i, 0, 0))
    idx_flat = idx.reshape(t * PEER_PICKS)
    ids_smem = pltpu.SMEM((2 * rows * PEER_PICKS,), jnp.int32)
    w = pl.pallas_call(
        _peer_up_kernel,
        grid=(t // rows,),
        in_specs=[hbm, tok3, picks, _const_spec(sel.shape), hbm],
        out_specs=picks,
        out_shape=jax.ShapeDtypeStruct((t, PEER_PICKS), F32),
        scratch_shapes=[pltpu.VMEM(tab_shape, jnp.uint32), ids_smem,
                        pltpu.VMEM((PEER_PICKS // 8, UP_BATCH * 8, 128), F32),
                        pltpu.VMEM((8, 16, 128), F32), pltpu.SemaphoreType.DMA((3,))],
        compiler_params=_params("arbitrary"),
        name="peer_up",
    )(idx_flat, hf.reshape(t, d // 128, 128), gate, sel, _pack_table(expert_u, layer))
    out = pl.pallas_call(
        _peer_down_kernel,
        grid=(t // rows,),
        in_specs=[hbm, picks, _const_spec((PEER_PICKS, 128)), hbm],
        out_specs=tok3,
        out_shape=jax.ShapeDtypeStruct((t, d // 128, 128), F32),
        scratch_shapes=[pltpu.VMEM(tab_shape, jnp.uint32), ids_smem,
                        pltpu.VMEM((DOWN_BATCH, PEER_PICKS, 128), F32), pltpu.SemaphoreType.DMA((3,))],
        compiler_params=_params("arbitrary"),
        name="peer_down",
    )(idx_flat, w, jnp.ones((PEER_PICKS, 128), BF16), _pack_table(expert_v, layer))
    return out.reshape(t, d)


def _peer(hf, w_query, keys1, keys2, expert_u, expert_v, layer):
    idx, gate = _peer_route(hf, w_query, keys1, keys2)
    return _peer_experts(hf, idx, gate, expert_u, expert_v, layer)


def _pad_w_in(w_in):
    d = w_in.shape[0]
    outs = [jnp.zeros((d, PROJ_COLS), BF16), jnp.zeros((d, ATT_COLS), BF16)]
    src = 0
    for size, (arr, dst) in zip(SRC_SIZES, SRC_DST):
        outs[arr] = outs[arr].at[:, dst:dst + size].set(w_in[:, src:src + size].astype(BF16))
        src += size
    return outs


def _layer(x2d, pe2d, b, seq, norm_mix, w_in, att_q_gain, att_k_gain, w_a_out, dn_conv, dn_a_log, dn_dt_bias,
           dn_out_gain, w_b_out, s5_a_re, s5_a_im, s5_log_dt, s5_b_re, s5_b_im, s5_c_re, s5_c_im, s5_d,
           w_c_val, w_c_gate, w_out, norm_ffn, peer_w_query, peer_keys1, peer_keys2, peer_u, peer_v,
           ple_w, ple_norm, ple_w_gate, layer):
    w_main, w_att = _pad_w_in(w_in)
    proj2 = _norm_matmul(x2d, norm_mix, w_main, tm=2048, tn=512)
    proj3 = proj2.reshape(b, seq, PROJ_COLS)
    proj_att = _norm_matmul(x2d, norm_mix, w_att, tn=ATT_COLS).reshape(b, seq, ATT_COLS)
    att = [_attention_group(proj_att, att_q_gain, att_k_gain, g) for g in range(len(ATT_GROUPS))]
    ob = _deltanet(proj3, dn_conv, dn_a_log, dn_dt_bias, dn_out_gain)
    s = _s5(proj3, s5_a_re, s5_a_im, s5_log_dt, s5_b_re, s5_b_im, s5_c_re, s5_c_im, s5_d)
    x1, hf = _combine(x2d, proj2, att, ob, s, w_a_out, w_b_out, w_c_val, w_c_gate, w_out, norm_ffn)
    ffn = _peer(hf, peer_w_query, peer_keys1, peer_keys2, peer_u, peer_v, layer)
    return _ple(x1, ffn, pe2d, ple_norm, ple_w_gate, ple_w)


def kernel(x, p, norm_mix, w_in, att_q_gain, att_k_gain, w_a_out, dn_conv, dn_a_log, dn_dt_bias, dn_out_gain, w_b_out, s5_a_re, s5_a_im, s5_log_dt, s5_b_re, s5_b_im, s5_c_re, s5_c_im, s5_d, w_c_val, w_c_gate, w_out, norm_ffn, peer_w_query, peer_keys1, peer_keys2, peer_u, peer_v, ple_w, ple_norm, ple_w_gate):
    b, seq, d = x.shape
    depth = p.shape[0]
    x2d = x.reshape(b * seq, d)
    for i in range(depth):
        x2d = _layer(x2d, p[i].reshape(b * seq, PLE_DIM), b, seq, norm_mix[i], w_in[i], att_q_gain[i], att_k_gain[i],
                     w_a_out[i], dn_conv[i], dn_a_log[i], dn_dt_bias[i], dn_out_gain[i], w_b_out[i], s5_a_re[i],
                     s5_a_im[i], s5_log_dt[i], s5_b_re[i], s5_b_im[i], s5_c_re[i], s5_c_im[i], s5_d[i], w_c_val[i],
                     w_c_gate[i], w_out[i], norm_ffn[i], peer_w_query[i], peer_keys1[i], peer_keys2[i], peer_u,
                     peer_v, ple_w[i], ple_norm[i], ple_w_gate[i], layer=i)
    return x2d.reshape(b, seq, d)
```
